```python
import jax, jax.numpy as jnp
from jax import lax
import numpy as np

D_MODEL = 1024
BATCH = 8
SEQ = 2048
DEPTH = 2
DEC_BATCH = 128
DEC_SEQ = 1
PAST_LEN = 16384
PAGE_SIZE = 128

GLA_HEADS = 4
GLA_DK = D_MODEL // 8
GLA_DV = D_MODEL // 4
GLA_RANK = 16
GLA_TAU = 16.0
GLA_CHUNK = 64
QK_TOT = GLA_HEADS * GLA_DK
V_TOT = GLA_HEADS * GLA_DV
CONV_WIDTH = D_MODEL // 2
CONV_K = 3
ALPHA = (2 * DEPTH) ** 0.25
BETA = (8 * DEPTH) ** -0.25
EPS = 1e-5
SPLITS = (QK_TOT, QK_TOT, V_TOT, V_TOT, GLA_RANK, CONV_WIDTH, CONV_WIDTH, CONV_WIDTH, CONV_WIDTH, D_MODEL, D_MODEL)
VALUE_SLOTS = (2, 7)
N_IN = sum(SPLITS)

kernel_name = "gla_shortconv_gated_merge_deepnorm_step"


def _split_in(z):
    out = []
    off = 0
    for n in SPLITS:
        out.append(z[..., off:off + n])
        off += n
    return out


def _layernorm(x, g, b):
    xf = x.astype(jnp.float32)
    mu = jnp.mean(xf, axis=-1, keepdims=True)
    var = jnp.mean(jnp.square(xf - mu), axis=-1, keepdims=True)
    return ((xf - mu) * lax.rsqrt(var + EPS) * g + b).astype(x.dtype)


def _head_rmsnorm(o, g):
    return o * lax.rsqrt(jnp.mean(jnp.square(o), axis=-1, keepdims=True) + EPS) * g


def _gla_chunked(q, k, v, log_a, s0):
    bsz, L = q.shape[0], q.shape[1]
    C = min(GLA_CHUNK, L)
    n = -(-L // C)
    pad = n * C - L

    def prep(t):
        t = jnp.pad(t.astype(jnp.float32), ((0, 0), (0, pad), (0, 0), (0, 0)))
        return t.reshape(bsz, n, C, t.shape[2], t.shape[3]).transpose(0, 3, 1, 2, 4)

    q, k, v, g = prep(q), prep(k), prep(v), prep(log_a)
    q = q * (GLA_DK ** -0.5)
    b = jnp.cumsum(g, axis=3)
    b_last = b[:, :, :, -1:, :]
    q_in = q * jnp.exp(b)
    k_in = k * jnp.exp(-b)
    k_st = k * jnp.exp(b_last - b)
    mask = jnp.tril(jnp.ones((C, C), dtype=bool))
    att = jnp.where(mask, jnp.einsum('bhnik,bhnjk->bhnij', q_in, k_in), 0.0)
    o_intra = jnp.einsum('bhnij,bhnjv->bhniv', att, v)
    d_state = jnp.einsum('bhnjk,bhnjv->bhnkv', k_st, v)
    decay = jnp.exp(b_last[:, :, :, 0, :])

    def step(S, inp):
        d, ds = inp
        return d[..., None] * S + ds, S

    s_fin, s_prev = lax.scan(step, s0.astype(jnp.float32),
                             (jnp.moveaxis(decay, 2, 0), jnp.moveaxis(d_state, 2, 0)))
    s_prev = jnp.moveaxis(s_prev, 0, 2)
    o = o_intra + jnp.einsum('bhnik,bhnkv->bhniv', q_in, s_prev)
    o = o.transpose(0, 2, 3, 1, 4).reshape(bsz, n * C, GLA_HEADS, GLA_DV)[:, :L]
    return o, s_fin


def _short_conv(z, buf, w):
    L = z.shape[1]
    zz = jnp.concatenate([buf.astype(z.dtype), z], axis=1)
    y = zz[:, 0:L] * w[0]
    for i in range(1, CONV_K):
        y = y + zz[:, i:i + L] * w[i]
    return y, zz[:, -(CONV_K - 1):]


def _layer(x, c, s_gla, s_conv, w_ada, b_ada, w_in, w_a2, b_a, gla_norm_g, conv_w, w_pa, w_pb, w_o, ln_g, ln_b):
    bsz, L, _ = x.shape
    shift, scale, gate = jnp.split(c @ w_ada + b_ada, 3, axis=-1)
    u = x * (1.0 + scale[:, None, :]) + shift[:, None, :]
    (q, k, v, g_gla, a_lr, cb, cc, ch, g_conv, m_gla, m_conv) = _split_in(u @ w_in)
    log_a = jax.nn.log_sigmoid((a_lr @ w_a2 + b_a).astype(jnp.float32)) / GLA_TAU
    hk = lambda t: t.reshape(bsz, L, GLA_HEADS, -1)
    o, s_gla_new = _gla_chunked(hk(q), hk(k), hk(v), hk(log_a), s_gla)
    o = _head_rmsnorm(o, gla_norm_g).reshape(bsz, L, V_TOT).astype(x.dtype) * jax.nn.silu(g_gla)
    yc, s_conv_new = _short_conv(cc * ch, s_conv, conv_w)
    yc = cb * yc * jax.nn.silu(g_conv)
    merged = jax.nn.sigmoid(m_gla) * (o @ w_pa) + jax.nn.sigmoid(m_conv) * (yc @ w_pb)
    out = merged @ w_o
    y = _layernorm(ALPHA * x + gate[:, None, :] * out, ln_g, ln_b)
    return y, s_gla_new, s_conv_new


def setup_inputs(seed: int = 0) -> dict:
    key = jax.random.key(seed)
    ks = jax.random.split(key, 20)
    f32 = jnp.float32
    nrm = lambda kk, shape, s: jax.random.normal(kk, shape, f32) * s
    col_scale = jnp.concatenate([jnp.full((n,), BETA if i in VALUE_SLOTS else 1.0, f32)
                                 for i, n in enumerate(SPLITS)])
    return {
        "x_prompt": nrm(ks[0], (BATCH, SEQ, D_MODEL), 1.0),
        "x_sample": nrm(ks[1], (DEC_BATCH, DEC_SEQ, D_MODEL), 1.0),
        "c_prompt": nrm(ks[2], (BATCH, D_MODEL), 1.0),
        "c_sample": nrm(ks[3], (DEC_BATCH, D_MODEL), 1.0),
        "state_gla": nrm(ks[4], (DEPTH, DEC_BATCH, GLA_HEADS, GLA_DK, GLA_DV), 0.5),
        "state_conv": nrm(ks[5], (DEPTH, DEC_BATCH, CONV_K - 1, CONV_WIDTH), 1.0),
        "w_ada": nrm(ks[6], (DEPTH, D_MODEL, 3 * D_MODEL), 0.5 * D_MODEL ** -0.5),
        "b_ada": nrm(ks[7], (DEPTH, 3 * D_MODEL), 0.02),
        "w_in": nrm(ks[8], (DEPTH, D_MODEL, N_IN), D_MODEL ** -0.5) * col_scale,
        "w_a2": nrm(ks[9], (DEPTH, GLA_RANK, QK_TOT), GLA_RANK ** -0.5),
        "b_a": nrm(ks[10], (DEPTH, QK_TOT), 0.1),
        "gla_norm_g": 1.0 + nrm(ks[11], (DEPTH, GLA_DV), 0.02),
        "conv_w": nrm(ks[12], (DEPTH, CONV_K, CONV_WIDTH), CONV_K ** -0.5),
        "w_pa": nrm(ks[13], (DEPTH, V_TOT, D_MODEL), BETA * V_TOT ** -0.5),
        "w_pb": nrm(ks[14], (DEPTH, CONV_WIDTH, D_MODEL), BETA * CONV_WIDTH ** -0.5),
        "w_o": nrm(ks[15], (DEPTH, D_MODEL, D_MODEL), BETA * D_MODEL ** -0.5),
        "ln_g": 1.0 + nrm(ks[16], (DEPTH, D_MODEL), 0.02),
        "ln_b": nrm(ks[17], (DEPTH, D_MODEL), 0.02),
    }


def reference(x_prompt, x_sample, c_prompt, c_sample, state_gla, state_conv, w_ada, b_ada, w_in, w_a2, b_a,
              gla_norm_g, conv_w, w_pa, w_pb, w_o, ln_g, ln_b):
    hp, hs = x_prompt, x_sample
    bp = x_prompt.shape[0]
    gla_p, conv_p, gla_s, conv_s = [], [], [], []
    for l in range(DEPTH):
        wl = (w_ada[l], b_ada[l], w_in[l], w_a2[l], b_a[l], gla_norm_g[l], conv_w[l],
              w_pa[l], w_pb[l], w_o[l], ln_g[l], ln_b[l])
        s0 = jnp.zeros((bp, GLA_HEADS, GLA_DK, GLA_DV), jnp.float32)
        buf0 = jnp.zeros((bp, CONV_K - 1, CONV_WIDTH), x_prompt.dtype)
        hp, sg, sc = _layer(hp, c_prompt, s0, buf0, *wl)
        gla_p.append(sg)
        conv_p.append(sc)
        hs, sg2, sc2 = _layer(hs, c_sample, state_gla[l], state_conv[l], *wl)
        gla_s.append(sg2)
        conv_s.append(sc2)
    return (hp, hs, jnp.stack(gla_p), jnp.stack(conv_p), jnp.stack(gla_s), jnp.stack(conv_s))
```

```python
import functools

import jax
import jax.numpy as jnp
from jax import lax
from jax.experimental import pallas as pl
from jax.experimental.pallas import tpu as pltpu

F32 = jnp.float32
BF16 = jnp.bfloat16

GLA_TAU = 16.0
GLA_CHUNK = 64
EPS = 1e-5

LANES = 128
SUBLANES = 8
VMEM_PHYSICAL_BYTES = 64 * 1024 * 1024

PROMPT_TILE = 256


def _log_sigmoid(x):
    return jnp.minimum(x, 0.0) - jnp.log1p(jnp.exp(-jnp.abs(x)))


def _silu(x):
    return x * jax.nn.sigmoid(x)


class _Layout:
    def __init__(self, d_model, heads, dk, dv, rank, conv_w):
        self.d = d_model
        self.h = heads
        self.dk = dk
        self.dv = dv
        self.rank = rank
        self.cw = conv_w
        self.qk = heads * dk
        self.vt = heads * dv
        self.rank_pad = -(-rank // LANES) * LANES
        self.off_q = 0
        self.off_k = self.off_q + self.qk
        self.off_v = self.off_k + self.qk
        self.off_gg = self.off_v + self.vt
        self.off_a = self.off_gg + self.vt
        self.off_cb = self.off_a + self.rank_pad
        self.off_cc = self.off_cb + conv_w
        self.off_ch = self.off_cc + conv_w
        self.off_gc = self.off_ch + conv_w
        self.off_mg = self.off_gc + conv_w
        self.off_mc = self.off_mg + d_model
        self.n_pad = self.off_mc + d_model
        self.groups = (
            (self.off_q, self.off_v), (self.off_v, self.off_gg), (self.off_gg, self.off_a),
            (self.off_a, self.off_cb), (self.off_cb, self.off_ch), (self.off_ch, self.off_mg),
            (self.off_mg, self.off_mc), (self.off_mc, self.n_pad))


def _pad_w_in(w, lay):
    a_end = lay.off_a + lay.rank
    pad = jnp.zeros((w.shape[0], lay.rank_pad - lay.rank), w.dtype)
    return jnp.concatenate([w[:, :a_end], pad, w[:, a_end:]], axis=1).astype(BF16)


def _pad_w_a2(w, lay):
    pad = jnp.zeros((lay.rank_pad - lay.rank, w.shape[1]), w.dtype)
    return jnp.concatenate([w, pad], axis=0).astype(BF16)


def _ada_kernel(c_ref, w_ref, b_ref, o_ref):
    c = c_ref[...].astype(BF16)
    w = w_ref[0].astype(BF16)
    o_ref[0] = jnp.dot(c, w, preferred_element_type=F32) + b_ref[0]


def _ada_mod(c_all, w_ada, b_ada):
    depth, d, n3 = w_ada.shape
    rows = c_all.shape[0]
    tn = d
    return pl.pallas_call(
        _ada_kernel,
        grid=(depth, n3 // tn),
        in_specs=[
            pl.BlockSpec((rows, d), lambda l, j: (0, 0)),
            pl.BlockSpec((1, d, tn), lambda l, j: (l, 0, j)),
            pl.BlockSpec((1, 1, tn), lambda l, j: (l, 0, j)),
        ],
        out_specs=pl.BlockSpec((1, rows, tn), lambda l, j: (l, 0, j)),
        out_shape=jax.ShapeDtypeStruct((depth, rows, n3), F32),
        compiler_params=pltpu.CompilerParams(
            dimension_semantics=("arbitrary", "arbitrary")),
        name="ada_mod",
    )(c_all, w_ada, b_ada.reshape(depth, 1, n3))


def _prompt_kernel(x_ref, mod_ref, w_in_ref, w_a2_ref, b_a_ref, gng_ref, cw_ref,
                   w_pa_ref, w_pb_ref, w_o_ref, lng_ref, lnb_ref,
                   y_ref, sg_ref, sc_ref,
                   u_scr, z_scr, la_scr, p_scr, s_scr, og_scr, yc_scr, mg_scr,
                   *, lay, tile, alpha):
    t = pl.program_id(1)
    n_t = pl.num_programs(1)
    d, h_n, dk, dv, cw = lay.d, lay.h, lay.dk, lay.dv, lay.cw
    ck = cw_ref.shape[0]
    halo = SUBLANES

    @pl.when(t == 0)
    def _():
        s_scr[...] = jnp.zeros_like(s_scr)
        p_scr[0:halo, :] = jnp.zeros((halo, cw), F32)

    shift = mod_ref[0, :, 0:d]
    scale = mod_ref[0, :, d:2 * d]
    gate = mod_ref[0, :, 2 * d:3 * d]

    u_scr[...] = (x_ref[0] * (1.0 + scale) + shift).astype(BF16)
    for c0, c1 in lay.groups:
        z_scr[:, c0:c1] = jnp.dot(u_scr[...], w_in_ref[:, c0:c1],
                                  preferred_element_type=F32)

    a_lr = z_scr[:, lay.off_a:lay.off_a + lay.rank_pad].astype(BF16)
    la = jnp.dot(a_lr, w_a2_ref[...], preferred_element_type=F32) + b_a_ref[...]
    la_scr[...] = _log_sigmoid(la) * (1.0 / GLA_TAU)

    row = lax.broadcasted_iota(jnp.int32, (GLA_CHUNK, lay.qk), 0)
    causal = (lax.broadcasted_iota(jnp.int32, (GLA_CHUNK, GLA_CHUNK), 0)
              >= lax.broadcasted_iota(jnp.int32, (GLA_CHUNK, GLA_CHUNK), 1))
    q_scale = dk ** -0.5
    gng = gng_ref[...]

    for c in range(tile // GLA_CHUNK):
        r0 = c * GLA_CHUNK
        r1 = r0 + GLA_CHUNK
        b = la_scr[r0:r1, :]
        s = 1
        while s < GLA_CHUNK:
            b = b + jnp.where(row >= s, pltpu.roll(b, s, 0), 0.0)
            s *= 2
        bl = b[GLA_CHUNK - 1:GLA_CHUNK, :]
        q_in = z_scr[r0:r1, lay.off_q:lay.off_q + lay.qk] * q_scale * jnp.exp(b)
        k = z_scr[r0:r1, lay.off_k:lay.off_k + lay.qk]
        k_in = k * jnp.exp(-b)
        k_st = k * jnp.exp(bl - b)
        decay = jnp.exp(bl)
        for h in range(h_n):
            ks = slice(h * dk, (h + 1) * dk)
            qh = q_in[:, ks].astype(BF16)
            v = z_scr[r0:r1, lay.off_v + h * dv:lay.off_v + (h + 1) * dv].astype(BF16)
            att = lax.dot_general(qh, k_in[:, ks].astype(BF16),
                                  (((1,), (1,)), ((), ())), preferred_element_type=F32)
            att = jnp.where(causal, att, 0.0).astype(BF16)
            s_prev = s_scr[h]
            o = (jnp.dot(att, v, preferred_element_type=F32)
                 + jnp.dot(qh, s_prev.astype(BF16), preferred_element_type=F32))
            d_state = lax.dot_general(k_st[:, ks].astype(BF16), v,
                                      (((0,), (0,)), ((), ())), preferred_element_type=F32)
            dec = jnp.transpose(jnp.broadcast_to(decay[:, ks], (dk, dk)))
            dec = jnp.concatenate([dec] * (dv // dk), axis=1)
            s_scr[h] = dec * s_prev + d_state
            ms = jnp.mean(o * o, axis=-1, keepdims=True)
            on = o * lax.rsqrt(ms + EPS) * gng
            gg = z_scr[r0:r1, lay.off_gg + h * dv:lay.off_gg + (h + 1) * dv]
            og_scr[r0:r1, h * dv:(h + 1) * dv] = (on * _silu(gg)).astype(BF16)
        p = (z_scr[r0:r1, lay.off_cc:lay.off_cc + cw]
             * z_scr[r0:r1, lay.off_ch:lay.off_ch + cw])
        p_scr[halo + r0:halo + r1, :] = p
        yc = cw_ref[ck - 1:ck, :] * p
        for i in range(ck - 1):
            lag = ck - 1 - i
            yc = yc + cw_ref[i:i + 1, :] * p_scr[halo + r0 - lag:halo + r1 - lag, :]
        yc = (z_scr[r0:r1, lay.off_cb:lay.off_cb + cw] * yc
              * _silu(z_scr[r0:r1, lay.off_gc:lay.off_gc + cw]))
        yc_scr[r0:r1, :] = yc.astype(BF16)

    p_scr[0:halo, :] = p_scr[tile:tile + halo, :]

    z_scr[:, 0:d] = jnp.dot(og_scr[...], w_pa_ref[...], preferred_element_type=F32)
    z_scr[:, d:2 * d] = jnp.dot(yc_scr[...], w_pb_ref[...], preferred_element_type=F32)
    for c in range(tile // GLA_CHUNK):
        r0 = c * GLA_CHUNK
        r1 = r0 + GLA_CHUNK
        merged = (jax.nn.sigmoid(z_scr[r0:r1, lay.off_mg:lay.off_mg + d]) * z_scr[r0:r1, 0:d]
                  + jax.nn.sigmoid(z_scr[r0:r1, lay.off_mc:lay.off_mc + d]) * z_scr[r0:r1, d:2 * d])
        mg_scr[r0:r1, :] = merged.astype(BF16)
    z_scr[:, 2 * d:3 * d] = jnp.dot(mg_scr[...], w_o_ref[...], preferred_element_type=F32)

    lng = lng_ref[...]
    lnb = lnb_ref[...]
    for c in range(tile // GLA_CHUNK):
        r0 = c * GLA_CHUNK
        r1 = r0 + GLA_CHUNK
        hres = alpha * x_ref[0, r0:r1, :] + gate * z_scr[r0:r1, 2 * d:3 * d]
        mu = jnp.mean(hres, axis=-1, keepdims=True)
        dev = hres - mu
        var = jnp.mean(dev * dev, axis=-1, keepdims=True)
        y_ref[0, r0:r1, :] = dev * lax.rsqrt(var + EPS) * lng + lnb

    @pl.when(t == n_t - 1)
    def _():
        sg_ref[0] = s_scr[...]
        sc_ref[0] = p_scr[halo - (ck - 1):halo, :]


def _prompt_vmem_bytes(lay, tile, ck):
    weights = 2 * (lay.d * lay.n_pad + lay.rank_pad * lay.qk + lay.vt * lay.d
                   + lay.cw * lay.d + lay.d * lay.d)
    io = 2 * 2 * tile * lay.d * 4 + 2 * (lay.h * lay.dk * lay.dv * 4)
    scratch = (tile * lay.d * 2 + tile * lay.n_pad * 4 + tile * lay.qk * 4
               + (tile + SUBLANES) * lay.cw * 4 + lay.h * lay.dk * lay.dv * 4
               + tile * lay.vt * 2 + tile * lay.cw * 2 + tile * lay.d * 2)
    temporaries = tile * lay.n_pad * 4
    return min(weights + io + scratch + temporaries, VMEM_PHYSICAL_BYTES * 15 // 16)


def _prompt_layer(x, mod, w_in_p, w_a2_p, b_a, gng, conv_w, w_pa, w_pb, w_o, ln_g, ln_b,
                  lay, alpha):
    bsz, seq, d = x.shape
    tile = PROMPT_TILE
    assert seq % tile == 0 and tile % GLA_CHUNK == 0
    ck = conv_w.shape[0]
    n_t = seq // tile

    def const(shape):
        return pl.BlockSpec(shape, lambda b, t: (0,) * len(shape),
                            pipeline_mode=pl.Buffered(1))

    kern = functools.partial(_prompt_kernel, lay=lay, tile=tile, alpha=alpha)
    return pl.pallas_call(
        kern,
        grid=(bsz, n_t),
        in_specs=[
            pl.BlockSpec((1, tile, d), lambda b, t: (b, t, 0)),
            pl.BlockSpec((1, 1, 3 * d), lambda b, t: (b, 0, 0)),
            const((d, lay.n_pad)),
            const((lay.rank_pad, lay.qk)),
            const((1, lay.qk)),
            const((1, lay.dv)),
            const((ck, lay.cw)),
            const((lay.vt, d)),
            const((lay.cw, d)),
            const((d, d)),
            const((1, d)),
            const((1, d)),
        ],
        out_specs=[
            pl.BlockSpec((1, tile, d), lambda b, t: (b, t, 0)),
            pl.BlockSpec((1, lay.h, lay.dk, lay.dv), lambda b, t: (b, 0, 0, 0)),
            pl.BlockSpec((1, ck - 1, lay.cw), lambda b, t: (b, 0, 0)),
        ],
        out_shape=[
            jax.ShapeDtypeStruct((bsz, seq, d), F32),
            jax.ShapeDtypeStruct((bsz, lay.h, lay.dk, lay.dv), F32),
            jax.ShapeDtypeStruct((bsz, ck - 1, lay.cw), F32),
        ],
        scratch_shapes=[
            pltpu.VMEM((tile, d), BF16),
            pltpu.VMEM((tile, lay.n_pad), F32),
            pltpu.VMEM((tile, lay.qk), F32),
            pltpu.VMEM((tile + SUBLANES, lay.cw), F32),
            pltpu.VMEM((lay.h, lay.dk, lay.dv), F32),
            pltpu.VMEM((tile, lay.vt), BF16),
            pltpu.VMEM((tile, lay.cw), BF16),
            pltpu.VMEM((tile, d), BF16),
        ],
        compiler_params=pltpu.CompilerParams(
            dimension_semantics=("arbitrary", "arbitrary"),
            vmem_limit_bytes=_prompt_vmem_bytes(lay, tile, ck)),
        name="prompt_layer",
    )(x, mod, w_in_p, w_a2_p, b_a, gng, conv_w, w_pa, w_pb, w_o, ln_g, ln_b)


def _sample_proj_kernel(x_ref, mod_ref, w_in_ref, w_a2_ref, b_a_ref, z_ref, a_ref, *, lay):
    d = lay.d
    shift = mod_ref[:, 0:d]
    scale = mod_ref[:, d:2 * d]
    u = (x_ref[...] * (1.0 + scale) + shift).astype(BF16)
    for c0, c1 in lay.groups:
        z_ref[:, c0:c1] = jnp.dot(u, w_in_ref[:, c0:c1], preferred_element_type=F32)
    a_lr = z_ref[:, lay.off_a:lay.off_a + lay.rank_pad].astype(BF16)
    la = jnp.dot(a_lr, w_a2_ref[...], preferred_element_type=F32) + b_a_ref[...]
    a_ref[...] = jnp.exp(_log_sigmoid(la) * (1.0 / GLA_TAU))


def _sample_proj(x, mod, w_in_p, w_a2_p, b_a, lay):
    n, d = x.shape
    kern = functools.partial(_sample_proj_kernel, lay=lay)
    vmem = 2 * (2 * d * lay.n_pad + 4 * n * lay.n_pad) + 16 * 1024 * 1024
    return pl.pallas_call(
        kern,
        out_shape=[jax.ShapeDtypeStruct((n, lay.n_pad), F32),
                   jax.ShapeDtypeStruct((n, lay.qk), F32)],
        compiler_params=pltpu.CompilerParams(vmem_limit_bytes=vmem),
        name="sample_proj",
    )(x, mod, w_in_p, w_a2_p, b_a)


def _sample_state_kernel(akq_ref, v_ref, s_ref, s_out_ref, o_ref, *, lay, bb):
    h_n, dk, dv = lay.h, lay.dk, lay.dv
    q_scale = dk ** -0.5
    rows = akq_ref.shape[1]
    for i in range(bb):
        m = jnp.concatenate([akq_ref[i], jnp.zeros((dk - rows, dk), F32)], axis=0)
        mt = jnp.transpose(m)
        for h in range(h_n):
            a_col = mt[:, h:h + 1]
            k_col = mt[:, h_n + h:h_n + h + 1]
            q_col = mt[:, 2 * h_n + h:2 * h_n + h + 1]
            v = v_ref[i, :, h * dv:(h + 1) * dv]
            s_new = a_col * s_ref[i, h] + k_col * v
            s_out_ref[i, h] = s_new
            o_ref[i, :, h * dv:(h + 1) * dv] = q_scale * jnp.sum(q_col * s_new, axis=0,
                                                                 keepdims=True)


def _sample_state(akq, v3, state, lay):
    n = state.shape[0]
    bb = 4
    assert n % bb == 0
    rows = akq.shape[1]
    kern = functools.partial(_sample_state_kernel, lay=lay, bb=bb)
    return pl.pallas_call(
        kern,
        grid=(n // bb,),
        in_specs=[
            pl.BlockSpec((bb, rows, lay.dk), lambda i: (i, 0, 0)),
            pl.BlockSpec((bb, 1, lay.vt), lambda i: (i, 0, 0)),
            pl.BlockSpec((bb, lay.h, lay.dk, lay.dv), lambda i: (i, 0, 0, 0)),
        ],
        out_specs=[
            pl.BlockSpec((bb, lay.h, lay.dk, lay.dv), lambda i: (i, 0, 0, 0)),
            pl.BlockSpec((bb, 1, lay.vt), lambda i: (i, 0, 0)),
        ],
        out_shape=[jax.ShapeDtypeStruct(state.shape, F32),
                   jax.ShapeDtypeStruct((n, 1, lay.vt), F32)],
        compiler_params=pltpu.CompilerParams(dimension_semantics=("arbitrary",)),
        name="sample_state",
    )(akq, v3, state)


def _sample_out_kernel(o_ref, z_ref, x_ref, mod_ref, buf_ref, gng_ref, cw_ref,
                       w_pa_ref, w_pb_ref, w_o_ref, lng_ref, lnb_ref,
                       y_ref, buf_out_ref, og_scr, *, lay, alpha):
    d, h_n, dv, cw = lay.d, lay.h, lay.dv, lay.cw
    ck = cw_ref.shape[0]
    gate = mod_ref[:, 2 * d:3 * d]
    gng = gng_ref[...]
    for h in range(h_n):
        o = o_ref[:, h * dv:(h + 1) * dv]
        ms = jnp.mean(o * o, axis=-1, keepdims=True)
        on = o * lax.rsqrt(ms + EPS) * gng
        gg = z_ref[:, lay.off_gg + h * dv:lay.off_gg + (h + 1) * dv]
        og_scr[:, h * dv:(h + 1) * dv] = (on * _silu(gg)).astype(BF16)
    pa = jnp.dot(og_scr[...], w_pa_ref[...], preferred_element_type=F32)

    p = z_ref[:, lay.off_cc:lay.off_cc + cw] * z_ref[:, lay.off_ch:lay.off_ch + cw]
    yc = cw_ref[ck - 1:ck, :] * p
    for i in range(ck - 1):
        yc = yc + cw_ref[i:i + 1, :] * buf_ref[:, i * cw:(i + 1) * cw]
    yc = z_ref[:, lay.off_cb:lay.off_cb + cw] * yc * _silu(z_ref[:, lay.off_gc:lay.off_gc + cw])
    pb = jnp.dot(yc.astype(BF16), w_pb_ref[...], preferred_element_type=F32)
    for i in range(ck - 2):
        buf_out_ref[:, i * cw:(i + 1) * cw] = buf_ref[:, (i + 1) * cw:(i + 2) * cw]
    buf_out_ref[:, (ck - 2) * cw:(ck - 1) * cw] = p

    merged = (jax.nn.sigmoid(z_ref[:, lay.off_mg:lay.off_mg + d]) * pa
              + jax.nn.sigmoid(z_ref[:, lay.off_mc:lay.off_mc + d]) * pb)
    out = jnp.dot(merged.astype(BF16), w_o_ref[...], preferred_element_type=F32)
    hres = alpha * x_ref[...] + gate * out
    mu = jnp.mean(hres, axis=-1, keepdims=True)
    dev = hres - mu
    var = jnp.mean(dev * dev, axis=-1, keepdims=True)
    y_ref[...] = dev * lax.rsqrt(var + EPS) * lng_ref[...] + lnb_ref[...]


def _sample_out(o, z, x, mod, buf, gng, conv_w, w_pa, w_pb, w_o, ln_g, ln_b, lay, alpha):
    n, d = x.shape
    kern = functools.partial(_sample_out_kernel, lay=lay, alpha=alpha)
    return pl.pallas_call(
        kern,
        out_shape=[jax.ShapeDtypeStruct((n, d), F32),
                   jax.ShapeDtypeStruct(buf.shape, F32)],
        scratch_shapes=[pltpu.VMEM((n, lay.vt), BF16)],
        compiler_params=pltpu.CompilerParams(vmem_limit_bytes=40 * 1024 * 1024),
        name="sample_out",
    )(o, z, x, mod, buf, gng, conv_w, w_pa, w_pb, w_o, ln_g, ln_b)


def _sample_layer(x, mod, state, buf, w_in_p, w_a2_p, b_a, gng, conv_w, w_pa, w_pb, w_o,
                  ln_g, ln_b, lay, alpha):
    n = x.shape[0]
    z, a = _sample_proj(x, mod, w_in_p, w_a2_p, b_a, lay)
    q = z[:, lay.off_q:lay.off_q + lay.qk].reshape(n, lay.h, lay.dk)
    k = z[:, lay.off_k:lay.off_k + lay.qk].reshape(n, lay.h, lay.dk)
    akq = jnp.concatenate([a.reshape(n, lay.h, lay.dk), k, q], axis=1)
    pad_rows = -akq.shape[1] % SUBLANES
    if pad_rows:
        akq = jnp.concatenate([akq, jnp.zeros((n, pad_rows, lay.dk), F32)], axis=1)
    v3 = z[:, lay.off_v:lay.off_v + lay.vt].reshape(n, 1, lay.vt)
    s_new, o = _sample_state(akq, v3, state, lay)
    ck = conv_w.shape[0]
    y, buf_new = _sample_out(o.reshape(n, lay.vt), z, x, mod, buf.reshape(n, (ck - 1) * lay.cw),
                             gng, conv_w, w_pa, w_pb, w_o, ln_g, ln_b, lay, alpha)
    return y, s_new, buf_new.reshape(buf.shape)


def kernel(x_prompt, x_sample, c_prompt, c_sample, state_gla, state_conv, w_ada, b_ada, w_in,
           w_a2, b_a, gla_norm_g, conv_w, w_pa, w_pb, w_o, ln_g, ln_b):
    depth, n_s, heads, dk, dv = state_gla.shape
    bp, seq, d = x_prompt.shape
    rank = w_a2.shape[1]
    lay = _Layout(d, heads, dk, dv, rank, conv_w.shape[2])
    alpha = (2 * depth) ** 0.25

    mod = _ada_mod(jnp.concatenate([c_prompt, c_sample], axis=0), w_ada, b_ada)

    hp = x_prompt
    hs = x_sample.reshape(n_s, d)
    gla_p, conv_p, gla_s, conv_s = [], [], [], []
    for l in range(depth):
        w_in_p = _pad_w_in(w_in[l], lay)
        w_a2_p = _pad_w_a2(w_a2[l], lay)
        shared = (b_a[l].reshape(1, -1), gla_norm_g[l].reshape(1, -1), conv_w[l],
                  w_pa[l].astype(BF16), w_pb[l].astype(BF16), w_o[l].astype(BF16),
                  ln_g[l].reshape(1, -1), ln_b[l].reshape(1, -1))
        b_a_l, gng_l, cw_l, w_pa_l, w_pb_l, w_o_l, lng_l, lnb_l = shared
        hp, sg, sc = _prompt_layer(hp, mod[l, :bp].reshape(bp, 1, 3 * d), w_in_p, w_a2_p,
                                   b_a_l, gng_l, cw_l, w_pa_l, w_pb_l, w_o_l, lng_l, lnb_l,
                                   lay, alpha)
        gla_p.append(sg)
        conv_p.append(sc)
        hs, sg2, sc2 = _sample_layer(hs, mod[l, bp:], state_gla[l], state_conv[l], w_in_p,
                                     w_a2_p, b_a_l, gng_l, cw_l, w_pa_l, w_pb_l, w_o_l,
                                     lng_l, lnb_l, lay, alpha)
        gla_s.append(sg2)
        conv_s.append(sc2)
    return (hp, hs.reshape(x_sample.shape), jnp.stack(gla_p), jnp.stack(conv_p),
            jnp.stack(gla_s), jnp.stack(conv_s))
```

```python
import functools

import jax
import jax.numpy as jnp
from jax import lax
from jax.experimental import pallas as pl
from jax.experimental.pallas import tpu as pltpu

F32 = jnp.float32
BF16 = jnp.bfloat16

GLA_TAU = 16.0
GLA_CHUNK = 64
EPS = 1e-5

LANES = 128
SUBLANES = 8
VMEM_PHYSICAL_BYTES = 64 * 1024 * 1024

PROMPT_TILE = 256


def _log_sigmoid(x):
    return jnp.minimum(x, 0.0) - jnp.log1p(jnp.exp(-jnp.abs(x)))


def _silu(x):
    return x * jax.nn.sigmoid(x)


class _Layout:
    def __init__(self, d_model, heads, dk, dv, rank, conv_w):
        self.d = d_model
        self.h = heads
        self.dk = dk
        self.dv = dv
        self.rank = rank
        self.cw = conv_w
        self.qk = heads * dk
        self.vt = heads * dv
        self.rank_pad = -(-rank // LANES) * LANES
        self.off_q = 0
        self.off_k = self.off_q + self.qk
        self.off_v = self.off_k + self.qk
        self.off_gg = self.off_v + self.vt
        self.off_a = self.off_gg + self.vt
        self.off_cb = self.off_a + self.rank_pad
        self.off_cc = self.off_cb + conv_w
        self.off_ch = self.off_cc + conv_w
        self.off_gc = self.off_ch + conv_w
        self.off_mg = self.off_gc + conv_w
        self.off_mc = self.off_mg + d_model
        self.n_pad = self.off_mc + d_model
        self.n_head = self.off_a
        self.n_tail = self.n_pad - self.off_cb
        self.groups = (
            (0, self.off_q, self.off_v, self.off_q),
            (0, self.off_v, self.off_gg, self.off_v),
            (0, self.off_gg, self.off_a, self.off_gg),
            (1, 0, self.rank_pad, self.off_a),
            (2, 0, 2 * conv_w, self.off_cb),
            (2, 2 * conv_w, 4 * conv_w, self.off_ch),
            (2, 4 * conv_w, 4 * conv_w + d_model, self.off_mg),
            (2, 4 * conv_w + d_model, self.n_tail, self.off_mc))


def _split_w_in(w, lay):
    a_end = lay.off_a + lay.rank
    pad = jnp.zeros((w.shape[0], lay.rank_pad - lay.rank), w.dtype)
    low = jnp.concatenate([w[:, lay.off_a:a_end], pad], axis=1)
    return w[:, :lay.off_a].astype(BF16), low.astype(BF16), w[:, a_end:].astype(BF16)


def _pad_w_a2(w, lay):
    pad = jnp.zeros((lay.rank_pad - lay.rank, w.shape[1]), w.dtype)
    return jnp.concatenate([w, pad], axis=0).astype(BF16)


def _ada_kernel(c_ref, w_ref, b_ref, o_ref):
    c = c_ref[...].astype(BF16)
    w = w_ref[0].astype(BF16)
    o_ref[0] = jnp.dot(c, w, preferred_element_type=F32) + b_ref[0]


def _ada_mod(c_all, w_ada, b_ada):
    depth, d, n3 = w_ada.shape
    rows = c_all.shape[0]
    tn = d
    return pl.pallas_call(
        _ada_kernel,
        grid=(depth, n3 // tn),
        in_specs=[
            pl.BlockSpec((rows, d), lambda l, j: (0, 0)),
            pl.BlockSpec((1, d, tn), lambda l, j: (l, 0, j)),
            pl.BlockSpec((1, 1, tn), lambda l, j: (l, 0, j)),
        ],
        out_specs=pl.BlockSpec((1, rows, tn), lambda l, j: (l, 0, j)),
        out_shape=jax.ShapeDtypeStruct((depth, rows, n3), F32),
        compiler_params=pltpu.CompilerParams(
            dimension_semantics=("arbitrary", "arbitrary")),
        name="ada_mod",
    )(c_all, w_ada, b_ada.reshape(depth, 1, n3))


def _prompt_kernel(x_ref, mod_ref, w_hd_ref, w_lr_ref, w_tl_ref, w_a2_ref, b_a_ref, gng_ref, cw_ref,
                   w_pa_ref, w_pb_ref, w_o_ref, lng_ref, lnb_ref,
                   y_ref, sg_ref, sc_ref,
                   u_scr, z_scr, la_scr, p_scr, s_scr, og_scr, yc_scr, mg_scr,
                   *, lay, tile, alpha):
    t = pl.program_id(1)
    n_t = pl.num_programs(1)
    d, h_n, dk, dv, cw = lay.d, lay.h, lay.dk, lay.dv, lay.cw
    ck = cw_ref.shape[0]
    halo = SUBLANES

    @pl.when(t == 0)
    def _():
        s_scr[...] = jnp.zeros_like(s_scr)
        p_scr[0:halo, :] = jnp.zeros((halo, cw), F32)

    shift = mod_ref[0, :, 0:d]
    scale = mod_ref[0, :, d:2 * d]
    gate = mod_ref[0, :, 2 * d:3 * d]

    u_scr[...] = (x_ref[0] * (1.0 + scale) + shift).astype(BF16)
    w_in_refs = (w_hd_ref, w_lr_ref, w_tl_ref)
    for piece, s0, s1, d0 in lay.groups:
        z_scr[:, d0:d0 + s1 - s0] = jnp.dot(u_scr[...], w_in_refs[piece][:, s0:s1],
                                            preferred_element_type=F32)

    a_lr = z_scr[:, lay.off_a:lay.off_a + lay.rank_pad].astype(BF16)
    la = jnp.dot(a_lr, w_a2_ref[...], preferred_element_type=F32) + b_a_ref[...]
    la_scr[...] = _log_sigmoid(la) * (1.0 / GLA_TAU)

    row = lax.broadcasted_iota(jnp.int32, (GLA_CHUNK, lay.qk), 0)
    causal = (lax.broadcasted_iota(jnp.int32, (GLA_CHUNK, GLA_CHUNK), 0)
              >= lax.broadcasted_iota(jnp.int32, (GLA_CHUNK, GLA_CHUNK), 1))
    q_scale = dk ** -0.5
    gng = gng_ref[...]

    for c in range(tile // GLA_CHUNK):
        r0 = c * GLA_CHUNK
        r1 = r0 + GLA_CHUNK
        b = la_scr[r0:r1, :]
        s = 1
        while s < GLA_CHUNK:
            b = b + jnp.where(row >= s, pltpu.roll(b, s, 0), 0.0)
            s *= 2
        bl = b[GLA_CHUNK - 1:GLA_CHUNK, :]
        q_in = z_scr[r0:r1, lay.off_q:lay.off_q + lay.qk] * q_scale * jnp.exp(b)
        k = z_scr[r0:r1, lay.off_k:lay.off_k + lay.qk]
        k_in = k * jnp.exp(-b)
        k_st = k * jnp.exp(bl - b)
        decay = jnp.exp(bl)
        for h in range(h_n):
            ks = slice(h * dk, (h + 1) * dk)
            qh = q_in[:, ks].astype(BF16)
            v = z_scr[r0:r1, lay.off_v + h * dv:lay.off_v + (h + 1) * dv].astype(BF16)
            att = lax.dot_general(qh, k_in[:, ks].astype(BF16),
                                  (((1,), (1,)), ((), ())), preferred_element_type=F32)
            att = jnp.where(causal, att, 0.0).astype(BF16)
            s_prev = s_scr[h]
            o = (jnp.dot(att, v, preferred_element_type=F32)
                 + jnp.dot(qh, s_prev.astype(BF16), preferred_element_type=F32))
            d_state = lax.dot_general(k_st[:, ks].astype(BF16), v,
                                      (((0,), (0,)), ((), ())), preferred_element_type=F32)
            dec = jnp.transpose(jnp.broadcast_to(decay[:, ks], (dk, dk)))
            dec = jnp.concatenate([dec] * (dv // dk), axis=1)
            s_scr[h] = dec * s_prev + d_state
            ms = jnp.mean(o * o, axis=-1, keepdims=True)
            on = o * lax.rsqrt(ms + EPS) * gng
            gg = z_scr[r0:r1, lay.off_gg + h * dv:lay.off_gg + (h + 1) * dv]
            og_scr[r0:r1, h * dv:(h + 1) * dv] = (on * _silu(gg)).astype(BF16)
        p = (z_scr[r0:r1, lay.off_cc:lay.off_cc + cw]
             * z_scr[r0:r1, lay.off_ch:lay.off_ch + cw])
        p_scr[halo + r0:halo + r1, :] = p
        yc = cw_ref[ck - 1:ck, :] * p
        for i in range(ck - 1):
            lag = ck - 1 - i
            yc = yc + cw_ref[i:i + 1, :] * p_scr[halo + r0 - lag:halo + r1 - lag, :]
        yc = (z_scr[r0:r1, lay.off_cb:lay.off_cb + cw] * yc
              * _silu(z_scr[r0:r1, lay.off_gc:lay.off_gc + cw]))
        yc_scr[r0:r1, :] = yc.astype(BF16)

    p_scr[0:halo, :] = p_scr[tile:tile + halo, :]

    z_scr[:, 0:d] = jnp.dot(og_scr[...], w_pa_ref[...], preferred_element_type=F32)
    z_scr[:, d:2 * d] = jnp.dot(yc_scr[...], w_pb_ref[...], preferred_element_type=F32)
    for c in range(tile // GLA_CHUNK):
        r0 = c * GLA_CHUNK
        r1 = r0 + GLA_CHUNK
        merged = (jax.nn.sigmoid(z_scr[r0:r1, lay.off_mg:lay.off_mg + d]) * z_scr[r0:r1, 0:d]
                  + jax.nn.sigmoid(z_scr[r0:r1, lay.off_mc:lay.off_mc + d]) * z_scr[r0:r1, d:2 * d])
        mg_scr[r0:r1, :] = merged.astype(BF16)
    z_scr[:, 2 * d:3 * d] = jnp.dot(mg_scr[...], w_o_ref[...], preferred_element_type=F32)

    lng = lng_ref[...]
    lnb = lnb_ref[...]
    for c in range(tile // GLA_CHUNK):
        r0 = c * GLA_CHUNK
        r1 = r0 + GLA_CHUNK
        hres = alpha * x_ref[0, r0:r1, :] + gate * z_scr[r0:r1, 2 * d:3 * d]
        mu = jnp.mean(hres, axis=-1, keepdims=True)
        dev = hres - mu
        var = jnp.mean(dev * dev, axis=-1, keepdims=True)
        y_ref[0, r0:r1, :] = dev * lax.rsqrt(var + EPS) * lng + lnb

    @pl.when(t == n_t - 1)
    def _():
        sg_ref[0] = s_scr[...]
        sc_ref[0] = p_scr[halo - (ck - 1):halo, :]


def _prompt_vmem_bytes(lay, tile, ck):
    weights = 2 * (lay.d * lay.n_pad + lay.rank_pad * lay.qk + lay.vt * lay.d
                   + lay.cw * lay.d + lay.d * lay.d)
    io = 2 * 2 * tile * lay.d * 4 + 2 * (lay.h * lay.dk * lay.dv * 4)
    scratch = (tile * lay.d * 2 + tile * lay.n_pad * 4 + tile * lay.qk * 4
               + (tile + SUBLANES) * lay.cw * 4 + lay.h * lay.dk * lay.dv * 4
               + tile * lay.vt * 2 + tile * lay.cw * 2 + tile * lay.d * 2)
    temporaries = tile * lay.n_pad * 4
    return min(weights + io + scratch + temporaries, VMEM_PHYSICAL_BYTES * 15 // 16)


def _prompt_layer(x, mod, w_in_pieces, w_a2_p, b_a, gng, conv_w, w_pa, w_pb, w_o, ln_g, ln_b,
                  lay, alpha):
    bsz, seq, d = x.shape
    tile = PROMPT_TILE
    assert seq % tile == 0 and tile % GLA_CHUNK == 0
    ck = conv_w.shape[0]
    n_t = seq // tile

    def const(shape):
        return pl.BlockSpec(shape, lambda b, t: (0,) * len(shape),
                            pipeline_mode=pl.Buffered(1))

    kern = functools.partial(_prompt_kernel, lay=lay, tile=tile, alpha=alpha)
    return pl.pallas_call(
        kern,
        grid=(bsz, n_t),
        in_specs=[
            pl.BlockSpec((1, tile, d), lambda b, t: (b, t, 0)),
            pl.BlockSpec((1, 1, 3 * d), lambda b, t: (b, 0, 0)),
            const((d, lay.n_head)),
            const((d, lay.rank_pad)),
            const((d, lay.n_tail)),
            const((lay.rank_pad, lay.qk)),
            const((1, lay.qk)),
            const((1, lay.dv)),
            const((ck, lay.cw)),
            const((lay.vt, d)),
            const((lay.cw, d)),
            const((d, d)),
            const((1, d)),
            const((1, d)),
        ],
        out_specs=[
            pl.BlockSpec((1, tile, d), lambda b, t: (b, t, 0)),
            pl.BlockSpec((1, lay.h, lay.dk, lay.dv), lambda b, t: (b, 0, 0, 0)),
            pl.BlockSpec((1, ck - 1, lay.cw), lambda b, t: (b, 0, 0)),
        ],
        out_shape=[
            jax.ShapeDtypeStruct((bsz, seq, d), F32),
            jax.ShapeDtypeStruct((bsz, lay.h, lay.dk, lay.dv), F32),
            jax.ShapeDtypeStruct((bsz, ck - 1, lay.cw), F32),
        ],
        scratch_shapes=[
            pltpu.VMEM((tile, d), BF16),
            pltpu.VMEM((tile, lay.n_pad), F32),
            pltpu.VMEM((tile, lay.qk), F32),
            pltpu.VMEM((tile + SUBLANES, lay.cw), F32),
            pltpu.VMEM((lay.h, lay.dk, lay.dv), F32),
            pltpu.VMEM((tile, lay.vt), BF16),
            pltpu.VMEM((tile, lay.cw), BF16),
            pltpu.VMEM((tile, d), BF16),
        ],
        compiler_params=pltpu.CompilerParams(
            dimension_semantics=("arbitrary", "arbitrary"),
            vmem_limit_bytes=_prompt_vmem_bytes(lay, tile, ck)),
        name="prompt_layer",
    )(x, mod, *w_in_pieces, w_a2_p, b_a, gng, conv_w, w_pa, w_pb, w_o, ln_g, ln_b)


def _sample_proj_kernel(x_ref, mod_ref, w_hd_ref, w_lr_ref, w_tl_ref, w_a2_ref, b_a_ref,
                        z_ref, a_ref, *, lay):
    d = lay.d
    shift = mod_ref[:, 0:d]
    scale = mod_ref[:, d:2 * d]
    u = (x_ref[...] * (1.0 + scale) + shift).astype(BF16)
    w_in_refs = (w_hd_ref, w_lr_ref, w_tl_ref)
    for piece, s0, s1, d0 in lay.groups:
        z_ref[:, d0:d0 + s1 - s0] = jnp.dot(u, w_in_refs[piece][:, s0:s1],
                                            preferred_element_type=F32)
    a_lr = z_ref[:, lay.off_a:lay.off_a + lay.rank_pad].astype(BF16)
    la = jnp.dot(a_lr, w_a2_ref[...], preferred_element_type=F32) + b_a_ref[...]
    a_ref[...] = jnp.exp(_log_sigmoid(la) * (1.0 / GLA_TAU))


def _sample_proj(x, mod, w_in_pieces, w_a2_p, b_a, lay):
    n, d = x.shape
    kern = functools.partial(_sample_proj_kernel, lay=lay)
    vmem = 2 * (2 * d * lay.n_pad + 4 * n * lay.n_pad) + 16 * 1024 * 1024
    return pl.pallas_call(
        kern,
        out_shape=[jax.ShapeDtypeStruct((n, lay.n_pad), F32),
                   jax.ShapeDtypeStruct((n, lay.qk), F32)],
        compiler_params=pltpu.CompilerParams(vmem_limit_bytes=vmem),
        name="sample_proj",
    )(x, mod, *w_in_pieces, w_a2_p, b_a)


def _sample_state_kernel(a_ref, qkv_ref, s_ref, *rest, lay, bb):
    s_out_ref, o_ref = rest[-2:]
    h_n, dk, dv = lay.h, lay.dk, lay.dv
    q_scale = dk ** -0.5
    for i in range(bb):
        rows = [a_ref[i, :, h * dk:(h + 1) * dk] for h in range(h_n)]
        rows += [qkv_ref[i, :, lay.off_k + h * dk:lay.off_k + (h + 1) * dk] for h in range(h_n)]
        rows += [qkv_ref[i, :, lay.off_q + h * dk:lay.off_q + (h + 1) * dk] for h in range(h_n)]
        rows.append(jnp.zeros((dk - len(rows), dk), F32))
        mt = jnp.transpose(jnp.concatenate(rows, axis=0))
        for h in range(h_n):
            a_col = mt[:, h:h + 1]
            k_col = mt[:, h_n + h:h_n + h + 1]
            q_col = mt[:, 2 * h_n + h:2 * h_n + h + 1]
            v = qkv_ref[i, :, lay.off_v + h * dv:lay.off_v + (h + 1) * dv]
            s_new = a_col * s_ref[i, h] + k_col * v
            s_out_ref[i, h] = s_new
            o_ref[i, :, h * dv:(h + 1) * dv] = q_scale * jnp.sum(q_col * s_new, axis=0,
                                                                 keepdims=True)


def _sample_state(a3, z3, state_all, prev_out, layer, lay):
    depth, n = state_all.shape[:2]
    bb = 4
    assert n % bb == 0
    kern = functools.partial(_sample_state_kernel, lay=lay, bb=bb)
    state_spec = pl.BlockSpec((None, bb, lay.h, lay.dk, lay.dv), lambda i: (layer, i, 0, 0, 0))
    in_specs = [
        pl.BlockSpec((bb, 1, lay.qk), lambda i: (i, 0, 0)),
        pl.BlockSpec((bb, 1, lay.off_gg), lambda i: (i, 0, 0)),
        state_spec,
    ]
    args = [a3, z3, state_all]
    aliases = {}
    if prev_out is not None:
        in_specs.append(pl.BlockSpec(memory_space=pl.ANY))
        args.append(prev_out)
        aliases = {3: 0}
    return pl.pallas_call(
        kern,
        grid=(n // bb,),
        in_specs=in_specs,
        out_specs=[state_spec, pl.BlockSpec((bb, 1, lay.vt), lambda i: (i, 0, 0))],
        out_shape=[jax.ShapeDtypeStruct(state_all.shape, F32),
                   jax.ShapeDtypeStruct((n, 1, lay.vt), F32)],
        input_output_aliases=aliases,
        compiler_params=pltpu.CompilerParams(dimension_semantics=("arbitrary",)),
        name="sample_state",
    )(*args)


def _sample_out_kernel(o_ref, z_ref, x_ref, mod_ref, buf_ref, gng_ref, cw_ref,
                       w_pa_ref, w_pb_ref, w_o_ref, lng_ref, lnb_ref,
                       y_ref, buf_out_ref, og_scr, *, lay, alpha):
    d, h_n, dv, cw = lay.d, lay.h, lay.dv, lay.cw
    ck = cw_ref.shape[0]
    gate = mod_ref[:, 2 * d:3 * d]
    gng = gng_ref[...]
    for h in range(h_n):
        o = o_ref[:, h * dv:(h + 1) * dv]
        ms = jnp.mean(o * o, axis=-1, keepdims=True)
        on = o * lax.rsqrt(ms + EPS) * gng
        gg = z_ref[:, lay.off_gg + h * dv:lay.off_gg + (h + 1) * dv]
        og_scr[:, h * dv:(h + 1) * dv] = (on * _silu(gg)).astype(BF16)
    pa = jnp.dot(og_scr[...], w_pa_ref[...], preferred_element_type=F32)

    p = z_ref[:, lay.off_cc:lay.off_cc + cw] * z_ref[:, lay.off_ch:lay.off_ch + cw]
    yc = cw_ref[ck - 1:ck, :] * p
    for i in range(ck - 1):
        yc = yc + cw_ref[i:i + 1, :] * buf_ref[:, i * cw:(i + 1) * cw]
    yc = z_ref[:, lay.off_cb:lay.off_cb + cw] * yc * _silu(z_ref[:, lay.off_gc:lay.off_gc + cw])
    pb = jnp.dot(yc.astype(BF16), w_pb_ref[...], preferred_element_type=F32)
    for i in range(ck - 2):
        buf_out_ref[:, i * cw:(i + 1) * cw] = buf_ref[:, (i + 1) * cw:(i + 2) * cw]
    buf_out_ref[:, (ck - 2) * cw:(ck - 1) * cw] = p

    merged = (jax.nn.sigmoid(z_ref[:, lay.off_mg:lay.off_mg + d]) * pa
              + jax.nn.sigmoid(z_ref[:, lay.off_mc:lay.off_mc + d]) * pb)
    out = jnp.dot(merged.astype(BF16), w_o_ref[...], preferred_element_type=F32)
    hres = alpha * x_ref[...] + gate * out
    mu = jnp.mean(hres, axis=-1, keepdims=True)
    dev = hres - mu
    var = jnp.mean(dev * dev, axis=-1, keepdims=True)
    y_ref[...] = dev * lax.rsqrt(var + EPS) * lng_ref[...] + lnb_ref[...]


def _sample_out(o, z, x, mod, buf, gng, conv_w, w_pa, w_pb, w_o, ln_g, ln_b, lay, alpha):
    n, d = x.shape
    kern = functools.partial(_sample_out_kernel, lay=lay, alpha=alpha)
    return pl.pallas_call(
        kern,
        out_shape=[jax.ShapeDtypeStruct((n, d), F32),
                   jax.ShapeDtypeStruct(buf.shape, F32)],
        scratch_shapes=[pltpu.VMEM((n, lay.vt), BF16)],
        compiler_params=pltpu.CompilerParams(vmem_limit_bytes=40 * 1024 * 1024),
        name="sample_out",
    )(o, z, x, mod, buf, gng, conv_w, w_pa, w_pb, w_o, ln_g, ln_b)


def _sample_layer(x, mod, state_all, prev_state_out, layer, buf, w_in_pieces, w_a2_p, b_a, gng,
                  conv_w, w_pa, w_pb, w_o, ln_g, ln_b, lay, alpha):
    n = x.shape[0]
    z, a = _sample_proj(x, mod, w_in_pieces, w_a2_p, b_a, lay)
    state_out, o = _sample_state(a.reshape(n, 1, lay.qk), z.reshape(n, 1, lay.n_pad),
                                 state_all, prev_state_out, layer, lay)
    ck = conv_w.shape[0]
    y, buf_new = _sample_out(o.reshape(n, lay.vt), z, x, mod, buf.reshape(n, (ck - 1) * lay.cw),
                             gng, conv_w, w_pa, w_pb, w_o, ln_g, ln_b, lay, alpha)
    return y, state_out, buf_new.reshape(buf.shape)


def kernel(x_prompt, x_sample, c_prompt, c_sample, state_gla, state_conv, w_ada, b_ada, w_in,
           w_a2, b_a, gla_norm_g, conv_w, w_pa, w_pb, w_o, ln_g, ln_b):
    depth, n_s, heads, dk, dv = state_gla.shape
    bp, seq, d = x_prompt.shape
    rank = w_a2.shape[1]
    lay = _Layout(d, heads, dk, dv, rank, conv_w.shape[2])
    alpha = (2 * depth) ** 0.25

    mod = _ada_mod(jnp.concatenate([c_prompt, c_sample], axis=0), w_ada, b_ada)

    hp = x_prompt
    hs = x_sample.reshape(n_s, d)
    gla_p, conv_p, conv_s = [], [], []
    gla_s = None
    for l in range(depth):
        w_in_p = _split_w_in(w_in[l], lay)
        w_a2_p = _pad_w_a2(w_a2[l], lay)
        shared = (b_a[l].reshape(1, -1), gla_norm_g[l].reshape(1, -1), conv_w[l],
                  w_pa[l].astype(BF16), w_pb[l].astype(BF16), w_o[l].astype(BF16),
                  ln_g[l].reshape(1, -1), ln_b[l].reshape(1, -1))
        b_a_l, gng_l, cw_l, w_pa_l, w_pb_l, w_o_l, lng_l, lnb_l = shared
        hp, sg, sc = _prompt_layer(hp, mod[l, :bp].reshape(bp, 1, 3 * d), w_in_p, w_a2_p,
                                   b_a_l, gng_l, cw_l, w_pa_l, w_pb_l, w_o_l, lng_l, lnb_l,
                                   lay, alpha)
        gla_p.append(sg)
        conv_p.append(sc)
        hs, gla_s, sc2 = _sample_layer(hs, mod[l, bp:], state_gla, gla_s, l, state_conv[l],
                                       w_in_p, w_a2_p, b_a_l, gng_l, cw_l, w_pa_l, w_pb_l,
                                       w_o_l, lng_l, lnb_l, lay, alpha)
        conv_s.append(sc2)
    return (hp, hs.reshape(x_sample.shape), jnp.stack(gla_p), jnp.stack(conv_p),
            gla_s, jnp.stack(conv_s))
```

```python
import functools

import jax
import jax.numpy as jnp
from jax import lax
from jax.experimental import pallas as pl
from jax.experimental.pallas import tpu as pltpu

F32 = jnp.float32
BF16 = jnp.bfloat16

GLA_TAU = 16.0
GLA_CHUNK = 64
EPS = 1e-5

LANES = 128
SUBLANES = 8
VMEM_PHYSICAL_BYTES = 64 * 1024 * 1024
MIB = 1024 * 1024

PROMPT_TILE = 256
SAMPLE_STATE_ROWS = 8
PREP_ROWS = 256


def _log_sigmoid(x):
    return jnp.minimum(x, 0.0) - jnp.log1p(jnp.exp(-jnp.abs(x)))


def _silu(x):
    return x * jax.nn.sigmoid(x)


class _Layout:
    def __init__(self, d_model, heads, dk, dv, rank, conv_w):
        self.d = d_model
        self.h = heads
        self.dk = dk
        self.dv = dv
        self.rank = rank
        self.cw = conv_w
        self.qk = heads * dk
        self.vt = heads * dv
        self.rank_pad = -(-rank // LANES) * LANES
        self.off_q = 0
        self.off_k = self.off_q + self.qk
        self.off_v = self.off_k + self.qk
        self.off_gg = self.off_v + self.vt
        self.off_a = self.off_gg + self.vt
        self.off_cb = self.off_a + self.rank_pad
        self.off_cc = self.off_cb + conv_w
        self.off_ch = self.off_cc + conv_w
        self.off_gc = self.off_ch + conv_w
        self.off_mg = self.off_gc + conv_w
        self.off_mc = self.off_mg + d_model
        self.n_pad = self.off_mc + d_model
        self.n_head = self.off_a
        self.n_tail = self.n_pad - self.off_cb
        self.groups = (
            (0, self.off_q, self.off_v, self.off_q),
            (0, self.off_v, self.off_gg, self.off_v),
            (0, self.off_gg, self.off_a, self.off_gg),
            (1, 0, self.rank_pad, self.off_a),
            (2, 0, 2 * conv_w, self.off_cb),
            (2, 2 * conv_w, 4 * conv_w, self.off_ch),
            (2, 4 * conv_w, 4 * conv_w + d_model, self.off_mg),
            (2, 4 * conv_w + d_model, self.n_tail, self.off_mc))


def _layer_spec(shape, layer, single_buffer=False):
    zeros = (0,) * len(shape)
    kwargs = {"pipeline_mode": pl.Buffered(1)} if single_buffer else {}
    return pl.BlockSpec((None,) + tuple(shape), lambda *_: (layer,) + zeros, **kwargs)


def _split_w_in_kernel(w_ref, hd_ref, lr_ref, tl_ref, *, lay):
    a_end = lay.off_a + lay.rank
    hd_ref[...] = w_ref[:, 0:lay.off_a].astype(BF16)
    low = w_ref[:, lay.off_a:lay.off_a + lay.rank_pad]
    lane = lax.broadcasted_iota(jnp.int32, low.shape, 1)
    lr_ref[...] = jnp.where(lane < lay.rank, low, 0.0).astype(BF16)
    tl_ref[...] = w_ref[:, a_end:a_end + lay.n_tail].astype(BF16)


def _split_w_in(w_in, lay):
    depth, d, n_in = w_in.shape
    assert d % PREP_ROWS == 0 and n_in == lay.off_a + lay.rank + lay.n_tail
    kern = functools.partial(_split_w_in_kernel, lay=lay)
    out_cols = (lay.n_head, lay.rank_pad, lay.n_tail)
    return pl.pallas_call(
        kern,
        grid=(depth, d // PREP_ROWS),
        in_specs=[pl.BlockSpec((None, PREP_ROWS, n_in), lambda l, r: (l, r, 0))],
        out_specs=[pl.BlockSpec((None, PREP_ROWS, c), lambda l, r: (l, r, 0)) for c in out_cols],
        out_shape=[jax.ShapeDtypeStruct((depth, d, c), BF16) for c in out_cols],
        compiler_params=pltpu.CompilerParams(
            dimension_semantics=("arbitrary", "arbitrary"),
            vmem_limit_bytes=4 * PREP_ROWS * n_in * 4 + 16 * MIB),
        name="split_w_in",
    )(w_in)


def _cast_kernel(w_ref, o_ref):
    o_ref[...] = w_ref[...].astype(BF16)


def _cast_bf16(w):
    depth, rows, cols = w.shape
    return pl.pallas_call(
        _cast_kernel,
        grid=(depth,),
        in_specs=[pl.BlockSpec((None, rows, cols), lambda l: (l, 0, 0))],
        out_specs=pl.BlockSpec((None, rows, cols), lambda l: (l, 0, 0)),
        out_shape=jax.ShapeDtypeStruct(w.shape, BF16),
        compiler_params=pltpu.CompilerParams(
            dimension_semantics=("arbitrary",),
            vmem_limit_bytes=2 * rows * cols * 6 + 16 * MIB),
        name="cast_bf16",
    )(w)


def _ada_kernel(cp_ref, cs_ref, w_ref, b_ref, op_ref, os_ref):
    w = w_ref[...].astype(BF16)
    b = b_ref[...]
    op_ref[...] = jnp.dot(cp_ref[...].astype(BF16), w, preferred_element_type=F32) + b
    os_ref[...] = jnp.dot(cs_ref[...].astype(BF16), w, preferred_element_type=F32) + b


def _ada_mod(c_prompt, c_sample, w_ada, b_ada):
    depth, d, n3 = w_ada.shape
    bp, ns = c_prompt.shape[0], c_sample.shape[0]
    tn = d
    return pl.pallas_call(
        _ada_kernel,
        grid=(depth, n3 // tn),
        in_specs=[
            pl.BlockSpec((bp, d), lambda l, j: (0, 0)),
            pl.BlockSpec((ns, d), lambda l, j: (0, 0)),
            pl.BlockSpec((None, d, tn), lambda l, j: (l, 0, j)),
            pl.BlockSpec((None, 1, tn), lambda l, j: (l, 0, j)),
        ],
        out_specs=[pl.BlockSpec((None, bp, tn), lambda l, j: (l, 0, j)),
                   pl.BlockSpec((None, ns, tn), lambda l, j: (l, 0, j))],
        out_shape=[jax.ShapeDtypeStruct((depth, bp, n3), F32),
                   jax.ShapeDtypeStruct((depth, ns, n3), F32)],
        compiler_params=pltpu.CompilerParams(
            dimension_semantics=("arbitrary", "arbitrary")),
        name="ada_mod",
    )(c_prompt, c_sample, w_ada, b_ada.reshape(depth, 1, n3))


def _prompt_kernel(x_ref, mod_ref, w_hd_ref, w_lr_ref, w_tl_ref, w_a2_ref, b_a_ref, gng_ref,
                   cw_ref, w_pa_ref, w_pb_ref, w_o_ref, lng_ref, lnb_ref, *rest,
                   lay, tile, alpha):
    (y_ref, sg_ref, sc_ref,
     u_scr, z_scr, la_scr, p_scr, s_scr, og_scr, yc_scr, mg_scr) = rest[-11:]
    bi = pl.program_id(0)
    t = pl.program_id(1)
    n_t = pl.num_programs(1)
    d, h_n, dk, dv, cw = lay.d, lay.h, lay.dk, lay.dv, lay.cw
    ck = cw_ref.shape[0]
    halo = SUBLANES

    @pl.when(t == 0)
    def _():
        s_scr[...] = jnp.zeros_like(s_scr)
        p_scr[0:halo, :] = jnp.zeros((halo, cw), F32)

    shift = mod_ref[pl.ds(bi, 1), 0:d]
    scale = mod_ref[pl.ds(bi, 1), d:2 * d]
    gate = mod_ref[pl.ds(bi, 1), 2 * d:3 * d]

    u_scr[...] = (x_ref[0] * (1.0 + scale) + shift).astype(BF16)
    w_in_refs = (w_hd_ref, w_lr_ref, w_tl_ref)
    for piece, s0, s1, d0 in lay.groups:
        z_scr[:, d0:d0 + s1 - s0] = jnp.dot(u_scr[...], w_in_refs[piece][:, s0:s1],
                                            preferred_element_type=F32)

    a_lr = z_scr[:, lay.off_a:lay.off_a + lay.rank_pad].astype(BF16)
    la = jnp.dot(a_lr, w_a2_ref[...], preferred_element_type=F32) + b_a_ref[...]
    la_scr[...] = _log_sigmoid(la) * (1.0 / GLA_TAU)

    row = lax.broadcasted_iota(jnp.int32, (GLA_CHUNK, lay.qk), 0)
    causal = (lax.broadcasted_iota(jnp.int32, (GLA_CHUNK, GLA_CHUNK), 0)
              >= lax.broadcasted_iota(jnp.int32, (GLA_CHUNK, GLA_CHUNK), 1))
    q_scale = dk ** -0.5
    gng = gng_ref[...]

    for c in range(tile // GLA_CHUNK):
        r0 = c * GLA_CHUNK
        r1 = r0 + GLA_CHUNK
        b = la_scr[r0:r1, :]
        s = 1
        while s < GLA_CHUNK:
            b = b + jnp.where(row >= s, pltpu.roll(b, s, 0), 0.0)
            s *= 2
        bl = b[GLA_CHUNK - 1:GLA_CHUNK, :]
        q_in = z_scr[r0:r1, lay.off_q:lay.off_q + lay.qk] * q_scale * jnp.exp(b)
        k = z_scr[r0:r1, lay.off_k:lay.off_k + lay.qk]
        k_in = k * jnp.exp(-b)
        k_st = k * jnp.exp(bl - b)
        decay = jnp.exp(bl)
        for h in range(h_n):
            ks = slice(h * dk, (h + 1) * dk)
            qh = q_in[:, ks].astype(BF16)
            v = z_scr[r0:r1, lay.off_v + h * dv:lay.off_v + (h + 1) * dv].astype(BF16)
            att = lax.dot_general(qh, k_in[:, ks].astype(BF16),
                                  (((1,), (1,)), ((), ())), preferred_element_type=F32)
            att = jnp.where(causal, att, 0.0).astype(BF16)
            s_prev = s_scr[h]
            o = (jnp.dot(att, v, preferred_element_type=F32)
                 + jnp.dot(qh, s_prev.astype(BF16), preferred_element_type=F32))
            d_state = lax.dot_general(k_st[:, ks].astype(BF16), v,
                                      (((0,), (0,)), ((), ())), preferred_element_type=F32)
            dec = jnp.transpose(jnp.broadcast_to(decay[:, ks], (dk, dk)))
            dec = jnp.concatenate([dec] * (dv // dk), axis=1)
            s_scr[h] = dec * s_prev + d_state
            ms = jnp.mean(o * o, axis=-1, keepdims=True)
            on = o * lax.rsqrt(ms + EPS) * gng
            gg = z_scr[r0:r1, lay.off_gg + h * dv:lay.off_gg + (h + 1) * dv]
            og_scr[r0:r1, h * dv:(h + 1) * dv] = (on * _silu(gg)).astype(BF16)
        p = (z_scr[r0:r1, lay.off_cc:lay.off_cc + cw]
             * z_scr[r0:r1, lay.off_ch:lay.off_ch + cw])
        p_scr[halo + r0:halo + r1, :] = p
        yc = cw_ref[ck - 1:ck, :] * p
        for i in range(ck - 1):
            lag = ck - 1 - i
            yc = yc + cw_ref[i:i + 1, :] * p_scr[halo + r0 - lag:halo + r1 - lag, :]
        yc = (z_scr[r0:r1, lay.off_cb:lay.off_cb + cw] * yc
              * _silu(z_scr[r0:r1, lay.off_gc:lay.off_gc + cw]))
        yc_scr[r0:r1, :] = yc.astype(BF16)

    p_scr[0:halo, :] = p_scr[tile:tile + halo, :]

    z_scr[:, 0:d] = jnp.dot(og_scr[...], w_pa_ref[...], preferred_element_type=F32)
    z_scr[:, d:2 * d] = jnp.dot(yc_scr[...], w_pb_ref[...], preferred_element_type=F32)
    for c in range(tile // GLA_CHUNK):
        r0 = c * GLA_CHUNK
        r1 = r0 + GLA_CHUNK
        merged = (jax.nn.sigmoid(z_scr[r0:r1, lay.off_mg:lay.off_mg + d]) * z_scr[r0:r1, 0:d]
                  + jax.nn.sigmoid(z_scr[r0:r1, lay.off_mc:lay.off_mc + d]) * z_scr[r0:r1, d:2 * d])
        mg_scr[r0:r1, :] = merged.astype(BF16)
    z_scr[:, 2 * d:3 * d] = jnp.dot(mg_scr[...], w_o_ref[...], preferred_element_type=F32)

    lng = lng_ref[...]
    lnb = lnb_ref[...]
    for c in range(tile // GLA_CHUNK):
        r0 = c * GLA_CHUNK
        r1 = r0 + GLA_CHUNK
        hres = alpha * x_ref[0, r0:r1, :] + gate * z_scr[r0:r1, 2 * d:3 * d]
        mu = jnp.mean(hres, axis=-1, keepdims=True)
        dev = hres - mu
        var = jnp.mean(dev * dev, axis=-1, keepdims=True)
        y_ref[0, r0:r1, :] = dev * lax.rsqrt(var + EPS) * lng + lnb

    @pl.when(t == n_t - 1)
    def _():
        sg_ref[0] = s_scr[...]
        sc_ref[0] = p_scr[halo - (ck - 1):halo, :]


def _prompt_vmem_bytes(lay, tile):
    weights = 2 * (lay.d * lay.n_pad + lay.rank_pad * lay.qk + lay.vt * lay.d
                   + lay.cw * lay.d + lay.d * lay.d)
    io = 2 * 2 * tile * lay.d * 4 + 2 * (lay.h * lay.dk * lay.dv * 4)
    scratch = (tile * lay.d * 2 + tile * lay.n_pad * 4 + tile * lay.qk * 4
               + (tile + SUBLANES) * lay.cw * 4 + lay.h * lay.dk * lay.dv * 4
               + tile * lay.vt * 2 + tile * lay.cw * 2 + tile * lay.d * 2)
    temporaries = tile * lay.n_pad * 4
    return min(weights + io + scratch + temporaries, VMEM_PHYSICAL_BYTES * 15 // 16)


def _prompt_layer(x, mod_p, params, prev_states, layer, lay, alpha):
    (w_hd, w_lr, w_tl, w_a2, b_a, gng, conv_w, w_pa, w_pb, w_o, ln_g, ln_b) = params
    bsz, seq, d = x.shape
    depth = w_hd.shape[0]
    tile = PROMPT_TILE
    assert seq % tile == 0 and tile % GLA_CHUNK == 0
    ck = conv_w.shape[1]
    n_t = seq // tile

    def const(arr, single_buffer=True):
        return _layer_spec(arr.shape[1:], layer, single_buffer)

    in_specs = [
        pl.BlockSpec((1, tile, d), lambda b, t: (b, t, 0)),
        const(mod_p, False),
    ] + [const(a) for a in params]
    args = [x, mod_p, *params]
    aliases = {}
    if prev_states is not None:
        for j, prev in enumerate(prev_states):
            in_specs.append(pl.BlockSpec(memory_space=pl.ANY))
            aliases[len(args)] = 1 + j
            args.append(prev)

    kern = functools.partial(_prompt_kernel, lay=lay, tile=tile, alpha=alpha)
    return pl.pallas_call(
        kern,
        grid=(bsz, n_t),
        in_specs=in_specs,
        out_specs=[
            pl.BlockSpec((1, tile, d), lambda b, t: (b, t, 0)),
            pl.BlockSpec((None, 1, lay.h, lay.dk, lay.dv), lambda b, t: (layer, b, 0, 0, 0)),
            pl.BlockSpec((None, 1, ck - 1, lay.cw), lambda b, t: (layer, b, 0, 0)),
        ],
        out_shape=[
            jax.ShapeDtypeStruct((bsz, seq, d), F32),
            jax.ShapeDtypeStruct((depth, bsz, lay.h, lay.dk, lay.dv), F32),
            jax.ShapeDtypeStruct((depth, bsz, ck - 1, lay.cw), F32),
        ],
        scratch_shapes=[
            pltpu.VMEM((tile, d), BF16),
            pltpu.VMEM((tile, lay.n_pad), F32),
            pltpu.VMEM((tile, lay.qk), F32),
            pltpu.VMEM((tile + SUBLANES, lay.cw), F32),
            pltpu.VMEM((lay.h, lay.dk, lay.dv), F32),
            pltpu.VMEM((tile, lay.vt), BF16),
            pltpu.VMEM((tile, lay.cw), BF16),
            pltpu.VMEM((tile, d), BF16),
        ],
        input_output_aliases=aliases,
        compiler_params=pltpu.CompilerParams(
            dimension_semantics=("arbitrary", "arbitrary"),
            vmem_limit_bytes=_prompt_vmem_bytes(lay, tile)),
        name="prompt_layer",
    )(*args)


def _sample_proj_kernel(x_ref, mod_ref, w_hd_ref, w_lr_ref, w_tl_ref, w_a2_ref, b_a_ref,
                        z_ref, a_ref, *, lay):
    d = lay.d
    shift = mod_ref[:, 0:d]
    scale = mod_ref[:, d:2 * d]
    u = (x_ref[...] * (1.0 + scale) + shift).astype(BF16)
    w_in_refs = (w_hd_ref, w_lr_ref, w_tl_ref)
    for piece, s0, s1, d0 in lay.groups:
        z_ref[:, d0:d0 + s1 - s0] = jnp.dot(u, w_in_refs[piece][:, s0:s1],
                                            preferred_element_type=F32)
    a_lr = z_ref[:, lay.off_a:lay.off_a + lay.rank_pad].astype(BF16)
    la = jnp.dot(a_lr, w_a2_ref[...], preferred_element_type=F32) + b_a_ref[...]
    a_ref[...] = jnp.exp(_log_sigmoid(la) * (1.0 / GLA_TAU))


def _sample_proj(x3, mod_s, w_hd, w_lr, w_tl, w_a2, b_a, layer, lay):
    n, _, d = x3.shape
    kern = functools.partial(_sample_proj_kernel, lay=lay)
    weights = (w_hd, w_lr, w_tl, w_a2, b_a)
    vmem = 2 * d * lay.n_pad + 2 * 4 * n * lay.n_pad + 16 * MIB
    return pl.pallas_call(
        kern,
        grid=(1,),
        in_specs=[pl.BlockSpec((n, None, d), lambda i: (0, 0, 0)),
                  _layer_spec(mod_s.shape[1:], layer)]
        + [_layer_spec(w.shape[1:], layer, True) for w in weights],
        out_specs=[pl.BlockSpec((n, lay.n_pad), lambda i: (0, 0)),
                   pl.BlockSpec((n, lay.qk), lambda i: (0, 0))],
        out_shape=[jax.ShapeDtypeStruct((n, lay.n_pad), F32),
                   jax.ShapeDtypeStruct((n, lay.qk), F32)],
        compiler_params=pltpu.CompilerParams(dimension_semantics=("arbitrary",),
                                             vmem_limit_bytes=vmem),
        name="sample_proj",
    )(x3, mod_s, *weights)


def _sample_state_kernel(a_ref, qkv_ref, s_ref, *rest, lay, bb):
    s_out_ref, o_ref = rest[-2:]
    h_n, dk, dv = lay.h, lay.dk, lay.dv
    q_scale = dk ** -0.5
    for i in range(bb):
        rows = [a_ref[i:i + 1, h * dk:(h + 1) * dk] for h in range(h_n)]
        rows += [qkv_ref[i:i + 1, lay.off_k + h * dk:lay.off_k + (h + 1) * dk]
                 for h in range(h_n)]
        rows += [qkv_ref[i:i + 1, lay.off_q + h * dk:lay.off_q + (h + 1) * dk]
                 for h in range(h_n)]
        rows.append(jnp.zeros((dk - len(rows), dk), F32))
        mt = jnp.transpose(jnp.concatenate(rows, axis=0))
        for h in range(h_n):
            a_col = mt[:, h:h + 1]
            k_col = mt[:, h_n + h:h_n + h + 1]
            q_col = mt[:, 2 * h_n + h:2 * h_n + h + 1]
            v = qkv_ref[i:i + 1, lay.off_v + h * dv:lay.off_v + (h + 1) * dv]
            s_new = a_col * s_ref[i, h] + k_col * v
            s_out_ref[i, h] = s_new
            o_ref[i:i + 1, h * dv:(h + 1) * dv] = q_scale * jnp.sum(q_col * s_new, axis=0,
                                                                   keepdims=True)


def _sample_state(a, z, state_all, prev_out, layer, lay):
    depth, n = state_all.shape[:2]
    bb = SAMPLE_STATE_ROWS
    assert n % bb == 0
    kern = functools.partial(_sample_state_kernel, lay=lay, bb=bb)
    state_spec = pl.BlockSpec((None, bb, lay.h, lay.dk, lay.dv), lambda i: (layer, i, 0, 0, 0))
    in_specs = [
        pl.BlockSpec((bb, lay.qk), lambda i: (i, 0)),
        pl.BlockSpec((bb, lay.off_gg), lambda i: (i, 0)),
        state_spec,
    ]
    args = [a, z, state_all]
    aliases = {}
    if prev_out is not None:
        in_specs.append(pl.BlockSpec(memory_space=pl.ANY))
        aliases = {len(args): 0}
        args.append(prev_out)
    block_bytes = bb * lay.h * lay.dk * lay.dv * 4
    return pl.pallas_call(
        kern,
        grid=(n // bb,),
        in_specs=in_specs,
        out_specs=[state_spec, pl.BlockSpec((bb, lay.vt), lambda i: (i, 0))],
        out_shape=[jax.ShapeDtypeStruct(state_all.shape, F32),
                   jax.ShapeDtypeStruct((n, lay.vt), F32)],
        input_output_aliases=aliases,
        compiler_params=pltpu.CompilerParams(dimension_semantics=("arbitrary",),
                                             vmem_limit_bytes=4 * block_bytes + 16 * MIB),
        name="sample_state",
    )(*args)


def _sample_out_kernel(o_ref, z_ref, x_ref, mod_ref, buf_ref, gng_ref, cw_ref,
                       w_pa_ref, w_pb_ref, w_o_ref, lng_ref, lnb_ref,
                       y_ref, buf_out_ref, og_scr, *, lay, alpha):
    d, h_n, dv, cw = lay.d, lay.h, lay.dv, lay.cw
    ck = cw_ref.shape[0]
    gate = mod_ref[:, 2 * d:3 * d]
    gng = gng_ref[...]
    for h in range(h_n):
        o = o_ref[:, h * dv:(h + 1) * dv]
        ms = jnp.mean(o * o, axis=-1, keepdims=True)
        on = o * lax.rsqrt(ms + EPS) * gng
        gg = z_ref[:, lay.off_gg + h * dv:lay.off_gg + (h + 1) * dv]
        og_scr[:, h * dv:(h + 1) * dv] = (on * _silu(gg)).astype(BF16)
    pa = jnp.dot(og_scr[...], w_pa_ref[...], preferred_element_type=F32)

    p = z_ref[:, lay.off_cc:lay.off_cc + cw] * z_ref[:, lay.off_ch:lay.off_ch + cw]
    yc = cw_ref[ck - 1:ck, :] * p
    for i in range(ck - 1):
        yc = yc + cw_ref[i:i + 1, :] * buf_ref[:, i * cw:(i + 1) * cw]
    yc = z_ref[:, lay.off_cb:lay.off_cb + cw] * yc * _silu(z_ref[:, lay.off_gc:lay.off_gc + cw])
    pb = jnp.dot(yc.astype(BF16), w_pb_ref[...], preferred_element_type=F32)
    for i in range(ck - 2):
        buf_out_ref[:, i * cw:(i + 1) * cw] = buf_ref[:, (i + 1) * cw:(i + 2) * cw]
    buf_out_ref[:, (ck - 2) * cw:(ck - 1) * cw] = p

    merged = (jax.nn.sigmoid(z_ref[:, lay.off_mg:lay.off_mg + d]) * pa
              + jax.nn.sigmoid(z_ref[:, lay.off_mc:lay.off_mc + d]) * pb)
    out = jnp.dot(merged.astype(BF16), w_o_ref[...], preferred_element_type=F32)
    hres = alpha * x_ref[...] + gate * out
    mu = jnp.mean(hres, axis=-1, keepdims=True)
    dev = hres - mu
    var = jnp.mean(dev * dev, axis=-1, keepdims=True)
    y_ref[...] = dev * lax.rsqrt(var + EPS) * lng_ref[...] + lnb_ref[...]


def _sample_out(o, z, x3, mod_s, buf, gng, conv_w, w_pa, w_pb, w_o, ln_g, ln_b, layer, lay,
                alpha):
    n, _, d = x3.shape
    kern = functools.partial(_sample_out_kernel, lay=lay, alpha=alpha)
    stacked = (gng, conv_w, w_pa, w_pb, w_o, ln_g, ln_b)

    def whole(arr):
        return pl.BlockSpec(arr.shape, lambda i: (0,) * arr.ndim)

    return pl.pallas_call(
        kern,
        grid=(1,),
        in_specs=[whole(o), whole(z), pl.BlockSpec((n, None, d), lambda i: (0, 0, 0)),
                  _layer_spec(mod_s.shape[1:], layer), whole(buf)]
        + [_layer_spec(w.shape[1:], layer) for w in stacked],
        out_specs=[pl.BlockSpec((n, None, d), lambda i: (0, 0, 0)),
                   pl.BlockSpec(buf.shape, lambda i: (0, 0))],
        out_shape=[jax.ShapeDtypeStruct((n, 1, d), F32),
                   jax.ShapeDtypeStruct(buf.shape, F32)],
        scratch_shapes=[pltpu.VMEM((n, lay.vt), BF16)],
        compiler_params=pltpu.CompilerParams(dimension_semantics=("arbitrary",),
                                             vmem_limit_bytes=48 * MIB),
        name="sample_out",
    )(o, z, x3, mod_s, buf, *stacked)


def kernel(x_prompt, x_sample, c_prompt, c_sample, state_gla, state_conv, w_ada, b_ada, w_in,
           w_a2, b_a, gla_norm_g, conv_w, w_pa, w_pb, w_o, ln_g, ln_b):
    depth, n_s, heads, dk, dv = state_gla.shape
    d = x_prompt.shape[2]
    rank = w_a2.shape[1]
    ck, cw = conv_w.shape[1:]
    lay = _Layout(d, heads, dk, dv, rank, cw)
    alpha = (2 * depth) ** 0.25

    mod_p, mod_s = _ada_mod(c_prompt, c_sample, w_ada, b_ada)

    w_hd, w_lr, w_tl = _split_w_in(w_in, lay)
    w_a2_p = jnp.concatenate(
        [w_a2, jnp.zeros((depth, lay.rank_pad - rank, lay.qk), w_a2.dtype)], axis=1).astype(BF16)
    w_pa_b, w_pb_b, w_o_b = _cast_bf16(w_pa), _cast_bf16(w_pb), _cast_bf16(w_o)
    b_a3 = b_a.reshape(depth, 1, -1)
    gng3 = gla_norm_g.reshape(depth, 1, -1)
    lng3 = ln_g.reshape(depth, 1, -1)
    lnb3 = ln_b.reshape(depth, 1, -1)
    params = (w_hd, w_lr, w_tl, w_a2_p, b_a3, gng3, conv_w, w_pa_b, w_pb_b, w_o_b, lng3, lnb3)

    hp, hs = x_prompt, x_sample
    prompt_states = None
    gla_s = None
    conv_s = []
    for l in range(depth):
        hp, *prompt_states = _prompt_layer(hp, mod_p, params, prompt_states, l, lay, alpha)
        z, a = _sample_proj(hs, mod_s, w_hd, w_lr, w_tl, w_a2_p, b_a3, l, lay)
        gla_s, o = _sample_state(a, z, state_gla, gla_s, l, lay)
        hs, buf_new = _sample_out(o, z, hs, mod_s, state_conv[l].reshape(n_s, (ck - 1) * cw),
                                  gng3, conv_w, w_pa_b, w_pb_b, w_o_b, lng3, lnb3, l, lay, alpha)
        conv_s.append(buf_new.reshape(n_s, ck - 1, cw))
    return (hp, hs, prompt_states[0], prompt_states[1], gla_s, jnp.stack(conv_s))
```

```python
import functools

import jax
import jax.numpy as jnp
from jax import lax
from jax.experimental import pallas as pl
from jax.experimental.pallas import tpu as pltpu

F32 = jnp.float32
BF16 = jnp.bfloat16

GLA_TAU = 16.0
GLA_CHUNK = 64
EPS = 1e-5

LANES = 128
SUBLANES = 8
VMEM_PHYSICAL_BYTES = 64 * 1024 * 1024
MIB = 1024 * 1024

PROMPT_TILE = 256
SAMPLE_STATE_ROWS = 8
PREP_COLS = 512


def _log_sigmoid(x):
    return jnp.minimum(x, 0.0) - jnp.log1p(jnp.exp(-jnp.abs(x)))


def _silu(x):
    return x * jax.nn.sigmoid(x)


class _Layout:
    def __init__(self, d_model, heads, dk, dv, rank, conv_w):
        self.d = d_model
        self.h = heads
        self.dk = dk
        self.dv = dv
        self.rank = rank
        self.cw = conv_w
        self.qk = heads * dk
        self.vt = heads * dv
        self.rank_pad = -(-rank // LANES) * LANES
        self.off_q = 0
        self.off_k = self.off_q + self.qk
        self.off_v = self.off_k + self.qk
        self.off_gg = self.off_v + self.vt
        self.off_a = self.off_gg + self.vt
        self.off_cb = self.off_a + self.rank_pad
        self.off_cc = self.off_cb + conv_w
        self.off_ch = self.off_cc + conv_w
        self.off_gc = self.off_ch + conv_w
        self.off_mg = self.off_gc + conv_w
        self.off_mc = self.off_mg + d_model
        self.n_pad = self.off_mc + d_model
        self.n_head = self.off_a
        self.n_tail = self.n_pad - self.off_cb
        self.groups = (
            (0, self.off_q, self.off_v, self.off_q),
            (0, self.off_v, self.off_gg, self.off_v),
            (0, self.off_gg, self.off_a, self.off_gg),
            (1, 0, self.rank_pad, self.off_a),
            (2, 0, 2 * conv_w, self.off_cb),
            (2, 2 * conv_w, 4 * conv_w, self.off_ch),
            (2, 4 * conv_w, 4 * conv_w + d_model, self.off_mg),
            (2, 4 * conv_w + d_model, self.n_tail, self.off_mc))


def _layer_spec(shape, layer, single_buffer=False):
    zeros = (0,) * len(shape)
    kwargs = {"pipeline_mode": pl.Buffered(1)} if single_buffer else {}
    return pl.BlockSpec((None,) + tuple(shape), lambda *_: (layer,) + zeros, **kwargs)


def _transpose_cast_kernel(w_ref, o_ref, *, valid):
    wt = jnp.transpose(w_ref[0])
    if valid < wt.shape[1]:
        lane = lax.broadcasted_iota(jnp.int32, wt.shape, 1)
        wt = jnp.where(lane < valid, wt, 0.0)
    o_ref[...] = wt.astype(BF16)


def _transpose_cast(w_t, col0, n_valid, n_out, cols):
    depth, n_in, d = w_t.shape
    assert n_out % cols == 0 and col0 % SUBLANES == 0 and col0 + n_out <= n_in
    kern = functools.partial(_transpose_cast_kernel, valid=min(n_valid, cols))
    return pl.pallas_call(
        kern,
        grid=(depth, n_out // cols),
        in_specs=[pl.BlockSpec((pl.Element(1), pl.Element(cols), pl.Element(d)),
                               lambda l, j: (l, pl.multiple_of(col0 + j * cols, SUBLANES), 0))],
        out_specs=pl.BlockSpec((None, d, cols), lambda l, j: (l, 0, j)),
        out_shape=jax.ShapeDtypeStruct((depth, d, n_out), BF16),
        compiler_params=pltpu.CompilerParams(
            dimension_semantics=("arbitrary", "arbitrary")),
        name="transpose_cast",
    )(w_t)


def _split_w_in(w_in, lay):
    n_in = w_in.shape[2]
    a_end = lay.off_a + lay.rank
    assert n_in == a_end + lay.n_tail and lay.rank <= lay.rank_pad == LANES
    w_t = jnp.swapaxes(w_in, 1, 2)
    head = _transpose_cast(w_t, 0, lay.n_head, lay.n_head, PREP_COLS)
    low = _transpose_cast(w_t, lay.off_a, lay.rank, lay.rank_pad, lay.rank_pad)
    tail = _transpose_cast(w_t, a_end, lay.n_tail, lay.n_tail, PREP_COLS)
    return head, low, tail


def _cast_kernel(w_ref, o_ref):
    o_ref[...] = w_ref[...].astype(BF16)


def _cast_bf16(w):
    depth, rows, cols = w.shape
    return pl.pallas_call(
        _cast_kernel,
        grid=(depth,),
        in_specs=[pl.BlockSpec((None, rows, cols), lambda l: (l, 0, 0))],
        out_specs=pl.BlockSpec((None, rows, cols), lambda l: (l, 0, 0)),
        out_shape=jax.ShapeDtypeStruct(w.shape, BF16),
        compiler_params=pltpu.CompilerParams(
            dimension_semantics=("arbitrary",),
            vmem_limit_bytes=2 * rows * cols * 6 + 16 * MIB),
        name="cast_bf16",
    )(w)


def _ada_kernel(cp_ref, cs_ref, w_ref, b_ref, op_ref, os_ref):
    w = w_ref[...].astype(BF16)
    b = b_ref[...]
    op_ref[...] = jnp.dot(cp_ref[...].astype(BF16), w, preferred_element_type=F32) + b
    os_ref[...] = jnp.dot(cs_ref[...].astype(BF16), w, preferred_element_type=F32) + b


def _ada_mod(c_prompt, c_sample, w_ada, b_ada):
    depth, d, n3 = w_ada.shape
    bp, ns = c_prompt.shape[0], c_sample.shape[0]
    tn = d
    return pl.pallas_call(
        _ada_kernel,
        grid=(depth, n3 // tn),
        in_specs=[
            pl.BlockSpec((bp, d), lambda l, j: (0, 0)),
            pl.BlockSpec((ns, d), lambda l, j: (0, 0)),
            pl.BlockSpec((None, d, tn), lambda l, j: (l, 0, j)),
            pl.BlockSpec((None, 1, tn), lambda l, j: (l, 0, j)),
        ],
        out_specs=[pl.BlockSpec((None, bp, tn), lambda l, j: (l, 0, j)),
                   pl.BlockSpec((None, ns, tn), lambda l, j: (l, 0, j))],
        out_shape=[jax.ShapeDtypeStruct((depth, bp, n3), F32),
                   jax.ShapeDtypeStruct((depth, ns, n3), F32)],
        compiler_params=pltpu.CompilerParams(
            dimension_semantics=("arbitrary", "arbitrary")),
        name="ada_mod",
    )(c_prompt, c_sample, w_ada, b_ada.reshape(depth, 1, n3))


def _prompt_kernel(x_ref, mod_ref, w_hd_ref, w_lr_ref, w_tl_ref, w_a2_ref, b_a_ref, gng_ref,
                   cw_ref, w_pa_ref, w_pb_ref, w_o_ref, lng_ref, lnb_ref, *rest,
                   lay, tile, alpha):
    (y_ref, sg_ref, sc_ref,
     u_scr, z_scr, la_scr, p_scr, s_scr, og_scr, yc_scr, mg_scr) = rest[-11:]
    bi = pl.program_id(0)
    t = pl.program_id(1)
    n_t = pl.num_programs(1)
    d, h_n, dk, dv, cw = lay.d, lay.h, lay.dk, lay.dv, lay.cw
    ck = cw_ref.shape[0]
    halo = SUBLANES

    @pl.when(t == 0)
    def _():
        s_scr[...] = jnp.zeros_like(s_scr)
        p_scr[0:halo, :] = jnp.zeros((halo, cw), F32)

    shift = mod_ref[pl.ds(bi, 1), 0:d]
    scale = mod_ref[pl.ds(bi, 1), d:2 * d]
    gate = mod_ref[pl.ds(bi, 1), 2 * d:3 * d]

    u_scr[...] = (x_ref[0] * (1.0 + scale) + shift).astype(BF16)
    w_in_refs = (w_hd_ref, w_lr_ref, w_tl_ref)
    for piece, s0, s1, d0 in lay.groups:
        z_scr[:, d0:d0 + s1 - s0] = jnp.dot(u_scr[...], w_in_refs[piece][:, s0:s1],
                                            preferred_element_type=F32)

    a_lr = z_scr[:, lay.off_a:lay.off_a + lay.rank_pad].astype(BF16)
    la = jnp.dot(a_lr, w_a2_ref[...], preferred_element_type=F32) + b_a_ref[...]
    la_scr[...] = _log_sigmoid(la) * (1.0 / GLA_TAU)

    row = lax.broadcasted_iota(jnp.int32, (GLA_CHUNK, lay.qk), 0)
    causal = (lax.broadcasted_iota(jnp.int32, (GLA_CHUNK, GLA_CHUNK), 0)
              >= lax.broadcasted_iota(jnp.int32, (GLA_CHUNK, GLA_CHUNK), 1))
    q_scale = dk ** -0.5
    gng = gng_ref[...]

    for c in range(tile // GLA_CHUNK):
        r0 = c * GLA_CHUNK
        r1 = r0 + GLA_CHUNK
        b = la_scr[r0:r1, :]
        s = 1
        while s < GLA_CHUNK:
            b = b + jnp.where(row >= s, pltpu.roll(b, s, 0), 0.0)
            s *= 2
        bl = b[GLA_CHUNK - 1:GLA_CHUNK, :]
        q_in = z_scr[r0:r1, lay.off_q:lay.off_q + lay.qk] * q_scale * jnp.exp(b)
        k = z_scr[r0:r1, lay.off_k:lay.off_k + lay.qk]
        k_in = k * jnp.exp(-b)
        k_st = k * jnp.exp(bl - b)
        decay = jnp.exp(bl)
        for h in range(h_n):
            ks = slice(h * dk, (h + 1) * dk)
            qh = q_in[:, ks].astype(BF16)
            v = z_scr[r0:r1, lay.off_v + h * dv:lay.off_v + (h + 1) * dv].astype(BF16)
            att = lax.dot_general(qh, k_in[:, ks].astype(BF16),
                                  (((1,), (1,)), ((), ())), preferred_element_type=F32)
            att = jnp.where(causal, att, 0.0).astype(BF16)
            s_prev = s_scr[h]
            o = (jnp.dot(att, v, preferred_element_type=F32)
                 + jnp.dot(qh, s_prev.astype(BF16), preferred_element_type=F32))
            d_state = lax.dot_general(k_st[:, ks].astype(BF16), v,
                                      (((0,), (0,)), ((), ())), preferred_element_type=F32)
            dec = jnp.transpose(jnp.broadcast_to(decay[:, ks], (dk, dk)))
            dec = jnp.concatenate([dec] * (dv // dk), axis=1)
            s_scr[h] = dec * s_prev + d_state
            ms = jnp.mean(o * o, axis=-1, keepdims=True)
            on = o * lax.rsqrt(ms + EPS) * gng
            gg = z_scr[r0:r1, lay.off_gg + h * dv:lay.off_gg + (h + 1) * dv]
            og_scr[r0:r1, h * dv:(h + 1) * dv] = (on * _silu(gg)).astype(BF16)
        p = (z_scr[r0:r1, lay.off_cc:lay.off_cc + cw]
             * z_scr[r0:r1, lay.off_ch:lay.off_ch + cw])
        p_scr[halo + r0:halo + r1, :] = p
        yc = cw_ref[ck - 1:ck, :] * p
        for i in range(ck - 1):
            lag = ck - 1 - i
            yc = yc + cw_ref[i:i + 1, :] * p_scr[halo + r0 - lag:halo + r1 - lag, :]
        yc = (z_scr[r0:r1, lay.off_cb:lay.off_cb + cw] * yc
              * _silu(z_scr[r0:r1, lay.off_gc:lay.off_gc + cw]))
        yc_scr[r0:r1, :] = yc.astype(BF16)

    p_scr[0:halo, :] = p_scr[tile:tile + halo, :]

    z_scr[:, 0:d] = jnp.dot(og_scr[...], w_pa_ref[...], preferred_element_type=F32)
    z_scr[:, d:2 * d] = jnp.dot(yc_scr[...], w_pb_ref[...], preferred_element_type=F32)
    for c in range(tile // GLA_CHUNK):
        r0 = c * GLA_CHUNK
        r1 = r0 + GLA_CHUNK
        merged = (jax.nn.sigmoid(z_scr[r0:r1, lay.off_mg:lay.off_mg + d]) * z_scr[r0:r1, 0:d]
                  + jax.nn.sigmoid(z_scr[r0:r1, lay.off_mc:lay.off_mc + d]) * z_scr[r0:r1, d:2 * d])
        mg_scr[r0:r1, :] = merged.astype(BF16)
    z_scr[:, 2 * d:3 * d] = jnp.dot(mg_scr[...], w_o_ref[...], preferred_element_type=F32)

    lng = lng_ref[...]
    lnb = lnb_ref[...]
    for c in range(tile // GLA_CHUNK):
        r0 = c * GLA_CHUNK
        r1 = r0 + GLA_CHUNK
        hres = alpha * x_ref[0, r0:r1, :] + gate * z_scr[r0:r1, 2 * d:3 * d]
        mu = jnp.mean(hres, axis=-1, keepdims=True)
        dev = hres - mu
        var = jnp.mean(dev * dev, axis=-1, keepdims=True)
        y_ref[0, r0:r1, :] = dev * lax.rsqrt(var + EPS) * lng + lnb

    @pl.when(t == n_t - 1)
    def _():
        sg_ref[0] = s_scr[...]
        sc_ref[0] = p_scr[halo - (ck - 1):halo, :]


def _prompt_vmem_bytes(lay, tile):
    weights = 2 * (lay.d * lay.n_pad + lay.rank_pad * lay.qk + lay.vt * lay.d
                   + lay.cw * lay.d + lay.d * lay.d)
    io = 2 * 2 * tile * lay.d * 4 + 2 * (lay.h * lay.dk * lay.dv * 4)
    scratch = (tile * lay.d * 2 + tile * lay.n_pad * 4 + tile * lay.qk * 4
               + (tile + SUBLANES) * lay.cw * 4 + lay.h * lay.dk * lay.dv * 4
               + tile * lay.vt * 2 + tile * lay.cw * 2 + tile * lay.d * 2)
    temporaries = tile * lay.n_pad * 4
    return min(weights + io + scratch + temporaries, VMEM_PHYSICAL_BYTES * 15 // 16)


def _prompt_layer(x, mod_p, params, prev_states, layer, lay, alpha):
    (w_hd, w_lr, w_tl, w_a2, b_a, gng, conv_w, w_pa, w_pb, w_o, ln_g, ln_b) = params
    bsz, seq, d = x.shape
    depth = w_hd.shape[0]
    tile = PROMPT_TILE
    assert seq % tile == 0 and tile % GLA_CHUNK == 0
    ck = conv_w.shape[1]
    n_t = seq // tile

    def const(arr, single_buffer=True):
        return _layer_spec(arr.shape[1:], layer, single_buffer)

    in_specs = [
        pl.BlockSpec((1, tile, d), lambda b, t: (b, t, 0)),
        const(mod_p, False),
    ] + [const(a) for a in params]
    args = [x, mod_p, *params]
    aliases = {}
    if prev_states is not None:
        for j, prev in enumerate(prev_states):
            in_specs.append(pl.BlockSpec(memory_space=pl.ANY))
            aliases[len(args)] = 1 + j
            args.append(prev)

    kern = functools.partial(_prompt_kernel, lay=lay, tile=tile, alpha=alpha)
    return pl.pallas_call(
        kern,
        grid=(bsz, n_t),
        in_specs=in_specs,
        out_specs=[
            pl.BlockSpec((1, tile, d), lambda b, t: (b, t, 0)),
            pl.BlockSpec((None, 1, lay.h, lay.dk, lay.dv), lambda b, t: (layer, b, 0, 0, 0)),
            pl.BlockSpec((None, 1, ck - 1, lay.cw), lambda b, t: (layer, b, 0, 0)),
        ],
        out_shape=[
            jax.ShapeDtypeStruct((bsz, seq, d), F32),
            jax.ShapeDtypeStruct((depth, bsz, lay.h, lay.dk, lay.dv), F32),
            jax.ShapeDtypeStruct((depth, bsz, ck - 1, lay.cw), F32),
        ],
        scratch_shapes=[
            pltpu.VMEM((tile, d), BF16),
            pltpu.VMEM((tile, lay.n_pad), F32),
            pltpu.VMEM((tile, lay.qk), F32),
            pltpu.VMEM((tile + SUBLANES, lay.cw), F32),
            pltpu.VMEM((lay.h, lay.dk, lay.dv), F32),
            pltpu.VMEM((tile, lay.vt), BF16),
            pltpu.VMEM((tile, lay.cw), BF16),
            pltpu.VMEM((tile, d), BF16),
        ],
        input_output_aliases=aliases,
        compiler_params=pltpu.CompilerParams(
            dimension_semantics=("arbitrary", "arbitrary"),
            vmem_limit_bytes=_prompt_vmem_bytes(lay, tile)),
        name="prompt_layer",
    )(*args)


def _sample_proj_kernel(x_ref, mod_ref, w_hd_ref, w_lr_ref, w_tl_ref, w_a2_ref, b_a_ref,
                        z_ref, a_ref, *, lay):
    d = lay.d
    shift = mod_ref[:, 0:d]
    scale = mod_ref[:, d:2 * d]
    u = (x_ref[...] * (1.0 + scale) + shift).astype(BF16)
    w_in_refs = (w_hd_ref, w_lr_ref, w_tl_ref)
    for piece, s0, s1, d0 in lay.groups:
        z_ref[:, d0:d0 + s1 - s0] = jnp.dot(u, w_in_refs[piece][:, s0:s1],
                                            preferred_element_type=F32)
    a_lr = z_ref[:, lay.off_a:lay.off_a + lay.rank_pad].astype(BF16)
    la = jnp.dot(a_lr, w_a2_ref[...], preferred_element_type=F32) + b_a_ref[...]
    a_ref[...] = jnp.exp(_log_sigmoid(la) * (1.0 / GLA_TAU))


def _sample_proj(x3, mod_s, w_hd, w_lr, w_tl, w_a2, b_a, layer, lay):
    n, _, d = x3.shape
    kern = functools.partial(_sample_proj_kernel, lay=lay)
    weights = (w_hd, w_lr, w_tl, w_a2, b_a)
    vmem = 2 * d * lay.n_pad + 2 * 4 * n * lay.n_pad + 16 * MIB
    return pl.pallas_call(
        kern,
        grid=(1,),
        in_specs=[pl.BlockSpec((n, None, d), lambda i: (0, 0, 0)),
                  _layer_spec(mod_s.shape[1:], layer)]
        + [_layer_spec(w.shape[1:], layer, True) for w in weights],
        out_specs=[pl.BlockSpec((n, lay.n_pad), lambda i: (0, 0)),
                   pl.BlockSpec((n, lay.qk), lambda i: (0, 0))],
        out_shape=[jax.ShapeDtypeStruct((n, lay.n_pad), F32),
                   jax.ShapeDtypeStruct((n, lay.qk), F32)],
        compiler_params=pltpu.CompilerParams(dimension_semantics=("arbitrary",),
                                             vmem_limit_bytes=vmem),
        name="sample_proj",
    )(x3, mod_s, *weights)


def _sample_state_kernel(a_ref, qkv_ref, s_ref, *rest, lay, bb):
    s_out_ref, o_ref = rest[-2:]
    h_n, dk, dv = lay.h, lay.dk, lay.dv
    q_scale = dk ** -0.5
    for i in range(bb):
        rows = [a_ref[i:i + 1, h * dk:(h + 1) * dk] for h in range(h_n)]
        rows += [qkv_ref[i:i + 1, lay.off_k + h * dk:lay.off_k + (h + 1) * dk]
                 for h in range(h_n)]
        rows += [qkv_ref[i:i + 1, lay.off_q + h * dk:lay.off_q + (h + 1) * dk]
                 for h in range(h_n)]
        rows.append(jnp.zeros((dk - len(rows), dk), F32))
        mt = jnp.transpose(jnp.concatenate(rows, axis=0))
        for h in range(h_n):
            a_col = mt[:, h:h + 1]
            k_col = mt[:, h_n + h:h_n + h + 1]
            q_col = mt[:, 2 * h_n + h:2 * h_n + h + 1]
            v = qkv_ref[i:i + 1, lay.off_v + h * dv:lay.off_v + (h + 1) * dv]
            s_new = a_col * s_ref[i, h] + k_col * v
            s_out_ref[i, h] = s_new
            o_ref[i:i + 1, h * dv:(h + 1) * dv] = q_scale * jnp.sum(q_col * s_new, axis=0,
                                                                   keepdims=True)


def _sample_state(a, z, state_all, prev_out, layer, lay):
    depth, n = state_all.shape[:2]
    bb = SAMPLE_STATE_ROWS
    assert n % bb == 0
    kern = functools.partial(_sample_state_kernel, lay=lay, bb=bb)
    state_spec = pl.BlockSpec((None, bb, lay.h, lay.dk, lay.dv), lambda i: (layer, i, 0, 0, 0))
    in_specs = [
        pl.BlockSpec((bb, lay.qk), lambda i: (i, 0)),
        pl.BlockSpec((bb, lay.off_gg), lambda i: (i, 0)),
        state_spec,
    ]
    args = [a, z, state_all]
    aliases = {}
    if prev_out is not None:
        in_specs.append(pl.BlockSpec(memory_space=pl.ANY))
        aliases = {len(args): 0}
        args.append(prev_out)
    block_bytes = bb * lay.h * lay.dk * lay.dv * 4
    return pl.pallas_call(
        kern,
        grid=(n // bb,),
        in_specs=in_specs,
        out_specs=[state_spec, pl.BlockSpec((bb, lay.vt), lambda i: (i, 0))],
        out_shape=[jax.ShapeDtypeStruct(state_all.shape, F32),
                   jax.ShapeDtypeStruct((n, lay.vt), F32)],
        input_output_aliases=aliases,
        compiler_params=pltpu.CompilerParams(dimension_semantics=("arbitrary",),
                                             vmem_limit_bytes=4 * block_bytes + 16 * MIB),
        name="sample_state",
    )(*args)


def _sample_out_kernel(o_ref, z_ref, x_ref, mod_ref, buf_ref, gng_ref, cw_ref,
                       w_pa_ref, w_pb_ref, w_o_ref, lng_ref, lnb_ref,
                       y_ref, buf_out_ref, og_scr, *, lay, alpha):
    d, h_n, dv, cw = lay.d, lay.h, lay.dv, lay.cw
    ck = cw_ref.shape[0]
    gate = mod_ref[:, 2 * d:3 * d]
    gng = gng_ref[...]
    for h in range(h_n):
        o = o_ref[:, h * dv:(h + 1) * dv]
        ms = jnp.mean(o * o, axis=-1, keepdims=True)
        on = o * lax.rsqrt(ms + EPS) * gng
        gg = z_ref[:, lay.off_gg + h * dv:lay.off_gg + (h + 1) * dv]
        og_scr[:, h * dv:(h + 1) * dv] = (on * _silu(gg)).astype(BF16)
    pa = jnp.dot(og_scr[...], w_pa_ref[...], preferred_element_type=F32)

    p = z_ref[:, lay.off_cc:lay.off_cc + cw] * z_ref[:, lay.off_ch:lay.off_ch + cw]
    yc = cw_ref[ck - 1:ck, :] * p
    for i in range(ck - 1):
        yc = yc + cw_ref[i:i + 1, :] * buf_ref[:, i * cw:(i + 1) * cw]
    yc = z_ref[:, lay.off_cb:lay.off_cb + cw] * yc * _silu(z_ref[:, lay.off_gc:lay.off_gc + cw])
    pb = jnp.dot(yc.astype(BF16), w_pb_ref[...], preferred_element_type=F32)
    for i in range(ck - 2):
        buf_out_ref[:, i * cw:(i + 1) * cw] = buf_ref[:, (i + 1) * cw:(i + 2) * cw]
    buf_out_ref[:, (ck - 2) * cw:(ck - 1) * cw] = p

    merged = (jax.nn.sigmoid(z_ref[:, lay.off_mg:lay.off_mg + d]) * pa
              + jax.nn.sigmoid(z_ref[:, lay.off_mc:lay.off_mc + d]) * pb)
    out = jnp.dot(merged.astype(BF16), w_o_ref[...], preferred_element_type=F32)
    hres = alpha * x_ref[...] + gate * out
    mu = jnp.mean(hres, axis=-1, keepdims=True)
    dev = hres - mu
    var = jnp.mean(dev * dev, axis=-1, keepdims=True)
    y_ref[...] = dev * lax.rsqrt(var + EPS) * lng_ref[...] + lnb_ref[...]


def _sample_out(o, z, x3, mod_s, buf, gng, conv_w, w_pa, w_pb, w_o, ln_g, ln_b, layer, lay,
                alpha):
    n, _, d = x3.shape
    kern = functools.partial(_sample_out_kernel, lay=lay, alpha=alpha)
    stacked = (gng, conv_w, w_pa, w_pb, w_o, ln_g, ln_b)

    def whole(arr):
        return pl.BlockSpec(arr.shape, lambda i: (0,) * arr.ndim)

    return pl.pallas_call(
        kern,
        grid=(1,),
        in_specs=[whole(o), whole(z), pl.BlockSpec((n, None, d), lambda i: (0, 0, 0)),
                  _layer_spec(mod_s.shape[1:], layer), whole(buf)]
        + [_layer_spec(w.shape[1:], layer) for w in stacked],
        out_specs=[pl.BlockSpec((n, None, d), lambda i: (0, 0, 0)),
                   pl.BlockSpec(buf.shape, lambda i: (0, 0))],
        out_shape=[jax.ShapeDtypeStruct((n, 1, d), F32),
                   jax.ShapeDtypeStruct(buf.shape, F32)],
        scratch_shapes=[pltpu.VMEM((n, lay.vt), BF16)],
        compiler_params=pltpu.CompilerParams(dimension_semantics=("arbitrary",),
                                             vmem_limit_bytes=48 * MIB),
        name="sample_out",
    )(o, z, x3, mod_s, buf, *stacked)


def kernel(x_prompt, x_sample, c_prompt, c_sample, state_gla, state_conv, w_ada, b_ada, w_in,
           w_a2, b_a, gla_norm_g, conv_w, w_pa, w_pb, w_o, ln_g, ln_b):
    depth, n_s, heads, dk, dv = state_gla.shape
    d = x_prompt.shape[2]
    rank = w_a2.shape[1]
    ck, cw = conv_w.shape[1:]
    lay = _Layout(d, heads, dk, dv, rank, cw)
    alpha = (2 * depth) ** 0.25

    mod_p, mod_s = _ada_mod(c_prompt, c_sample, w_ada, b_ada)

    w_hd, w_lr, w_tl = _split_w_in(w_in, lay)
    w_a2_p = jnp.concatenate(
        [w_a2, jnp.zeros((depth, lay.rank_pad - rank, lay.qk), w_a2.dtype)], axis=1).astype(BF16)
    w_pa_b, w_pb_b, w_o_b = _cast_bf16(w_pa), _cast_bf16(w_pb), _cast_bf16(w_o)
    b_a3 = b_a.reshape(depth, 1, -1)
    gng3 = gla_norm_g.reshape(depth, 1, -1)
    lng3 = ln_g.reshape(depth, 1, -1)
    lnb3 = ln_b.reshape(depth, 1, -1)
    params = (w_hd, w_lr, w_tl, w_a2_p, b_a3, gng3, conv_w, w_pa_b, w_pb_b, w_o_b, lng3, lnb3)

    hp, hs = x_prompt, x_sample
    prompt_states = None
    gla_s = None
    conv_s = []
    for l in range(depth):
        hp, *prompt_states = _prompt_layer(hp, mod_p, params, prompt_states, l, lay, alpha)
        z, a = _sample_proj(hs, mod_s, w_hd, w_lr, w_tl, w_a2_p, b_a3, l, lay)
        gla_s, o = _sample_state(a, z, state_gla, gla_s, l, lay)
        hs, buf_new = _sample_out(o, z, hs, mod_s, state_conv[l].reshape(n_s, (ck - 1) * cw),
                                  gng3, conv_w, w_pa_b, w_pb_b, w_o_b, lng3, lnb3, l, lay, alpha)
        conv_s.append(buf_new.reshape(n_s, ck - 1, cw))
    return (hp, hs, prompt_states[0], prompt_states[1], gla_s, jnp.stack(conv_s))
```

```python
import functools

import jax
import jax.numpy as jnp
from jax import lax
from jax.experimental import pallas as pl
from jax.experimental.pallas import tpu as pltpu

F32 = jnp.float32
BF16 = jnp.bfloat16

GLA_TAU = 16.0
GLA_CHUNK = 64
EPS = 1e-5

LANES = 128
SUBLANES = 8
VMEM_PHYSICAL_BYTES = 64 * 1024 * 1024
MIB = 1024 * 1024

PROMPT_TILE = 256
PROJ_COLS = 512
SAMPLE_STATE_ROWS = 8
PREP_COLS = 512


def _log_sigmoid(x):
    return jnp.minimum(x, 0.0) - jnp.log1p(jnp.exp(-jnp.abs(x)))


def _silu(x):
    return x * jax.nn.sigmoid(x)


class _Layout:
    def __init__(self, d_model, heads, dk, dv, rank, conv_w):
        self.d = d_model
        self.h = heads
        self.dk = dk
        self.dv = dv
        self.rank = rank
        self.cw = conv_w
        self.qk = heads * dk
        self.vt = heads * dv
        self.rank_pad = -(-rank // LANES) * LANES
        self.off_q = 0
        self.off_k = self.off_q + self.qk
        self.off_v = self.off_k + self.qk
        self.off_gg = self.off_v + self.vt
        self.off_a = self.off_gg + self.vt
        self.off_cb = self.off_a + self.rank_pad
        self.off_cc = self.off_cb + conv_w
        self.off_ch = self.off_cc + conv_w
        self.off_gc = self.off_ch + conv_w
        self.off_mg = self.off_gc + conv_w
        self.off_mc = self.off_mg + d_model
        self.n_pad = self.off_mc + d_model
        self.n_head = self.off_a
        self.n_tail = self.n_pad - self.off_cb
        self.groups = (
            (0, self.off_q, self.off_v, self.off_q),
            (0, self.off_v, self.off_gg, self.off_v),
            (0, self.off_gg, self.off_a, self.off_gg),
            (1, 0, self.rank_pad, self.off_a),
            (2, 0, 2 * conv_w, self.off_cb),
            (2, 2 * conv_w, 4 * conv_w, self.off_ch),
            (2, 4 * conv_w, 4 * conv_w + d_model, self.off_mg),
            (2, 4 * conv_w + d_model, self.n_tail, self.off_mc))


def _layer_spec(shape, layer, single_buffer=False):
    zeros = (0,) * len(shape)
    kwargs = {"pipeline_mode": pl.Buffered(1)} if single_buffer else {}
    return pl.BlockSpec((None,) + tuple(shape), lambda *_: (layer,) + zeros, **kwargs)


def _transpose_cast_kernel(w_ref, o_ref, *, valid):
    wt = jnp.transpose(w_ref[0])
    if valid < wt.shape[1]:
        lane = lax.broadcasted_iota(jnp.int32, wt.shape, 1)
        wt = jnp.where(lane < valid, wt, 0.0)
    o_ref[...] = wt.astype(BF16)


def _transpose_cast(w_t, col0, n_valid, n_out, cols):
    depth, n_in, d = w_t.shape
    assert n_out % cols == 0 and col0 % SUBLANES == 0 and col0 + n_out <= n_in
    kern = functools.partial(_transpose_cast_kernel, valid=min(n_valid, cols))
    return pl.pallas_call(
        kern,
        grid=(depth, n_out // cols),
        in_specs=[pl.BlockSpec((pl.Element(1), pl.Element(cols), pl.Element(d)),
                               lambda l, j: (l, pl.multiple_of(col0 + j * cols, SUBLANES), 0))],
        out_specs=pl.BlockSpec((None, d, cols), lambda l, j: (l, 0, j)),
        out_shape=jax.ShapeDtypeStruct((depth, d, n_out), BF16),
        compiler_params=pltpu.CompilerParams(
            dimension_semantics=("arbitrary", "arbitrary")),
        name="transpose_cast",
    )(w_t)


def _split_w_in(w_in, lay):
    n_in = w_in.shape[2]
    a_end = lay.off_a + lay.rank
    assert n_in == a_end + lay.n_tail and lay.rank <= lay.rank_pad == LANES
    w_t = jnp.swapaxes(w_in, 1, 2)
    head = _transpose_cast(w_t, 0, lay.n_head, lay.n_head, PREP_COLS)
    low = _transpose_cast(w_t, lay.off_a, lay.rank, lay.rank_pad, lay.rank_pad)
    tail = _transpose_cast(w_t, a_end, lay.n_tail, lay.n_tail, PREP_COLS)
    return head, low, tail


def _cast_kernel(w_ref, o_ref):
    o_ref[...] = w_ref[...].astype(BF16)


def _cast_bf16(w):
    depth, rows, cols = w.shape
    return pl.pallas_call(
        _cast_kernel,
        grid=(depth,),
        in_specs=[pl.BlockSpec((None, rows, cols), lambda l: (l, 0, 0))],
        out_specs=pl.BlockSpec((None, rows, cols), lambda l: (l, 0, 0)),
        out_shape=jax.ShapeDtypeStruct(w.shape, BF16),
        compiler_params=pltpu.CompilerParams(
            dimension_semantics=("arbitrary",),
            vmem_limit_bytes=2 * rows * cols * 6 + 16 * MIB),
        name="cast_bf16",
    )(w)


def _ada_kernel(cp_ref, cs_ref, w_ref, b_ref, op_ref, os_ref):
    w = w_ref[...].astype(BF16)
    b = b_ref[...]
    op_ref[...] = jnp.dot(cp_ref[...].astype(BF16), w, preferred_element_type=F32) + b
    os_ref[...] = jnp.dot(cs_ref[...].astype(BF16), w, preferred_element_type=F32) + b


def _ada_mod(c_prompt, c_sample, w_ada, b_ada):
    depth, d, n3 = w_ada.shape
    bp, ns = c_prompt.shape[0], c_sample.shape[0]
    tn = d
    return pl.pallas_call(
        _ada_kernel,
        grid=(depth, n3 // tn),
        in_specs=[
            pl.BlockSpec((bp, d), lambda l, j: (0, 0)),
            pl.BlockSpec((ns, d), lambda l, j: (0, 0)),
            pl.BlockSpec((None, d, tn), lambda l, j: (l, 0, j)),
            pl.BlockSpec((None, 1, tn), lambda l, j: (l, 0, j)),
        ],
        out_specs=[pl.BlockSpec((None, bp, tn), lambda l, j: (l, 0, j)),
                   pl.BlockSpec((None, ns, tn), lambda l, j: (l, 0, j))],
        out_shape=[jax.ShapeDtypeStruct((depth, bp, n3), F32),
                   jax.ShapeDtypeStruct((depth, ns, n3), F32)],
        compiler_params=pltpu.CompilerParams(
            dimension_semantics=("arbitrary", "arbitrary")),
        name="ada_mod",
    )(c_prompt, c_sample, w_ada, b_ada.reshape(depth, 1, n3))


def _project_plan(lay):
    plan = []
    for piece, s0, s1, d0 in lay.groups:
        for c0 in range(s0, s1, PROJ_COLS):
            plan.append((piece, c0, min(c0 + PROJ_COLS, s1), d0 + c0 - s0))
    return plan


def _project_stages(x_rows, mod_ref, brow, w_in_refs, u_scr, z_dst, lay):
    d = lay.d
    shift = mod_ref[pl.ds(brow, 1), 0:d]
    scale = mod_ref[pl.ds(brow, 1), d:2 * d]
    u_scr[...] = (x_rows() * (1.0 + scale) + shift).astype(BF16)
    yield
    for piece, c0, c1, d0 in _project_plan(lay):
        z_dst[:, d0:d0 + c1 - c0] = jnp.dot(u_scr[...], w_in_refs[piece][:, c0:c1],
                                            preferred_element_type=F32)
        yield


def _mix_stages(z_scr, x_ref, y_ref, row0, gate, w_a2_ref, b_a_ref, gng_ref, cw_ref,
                w_pa_ref, w_pb_ref, w_o_ref, lng_ref, lnb_ref,
                la_scr, p_scr, s_scr, og_scr, yc_scr, mg_scr,
                qin_scr, kin_scr, kst_scr, vb_scr, dec_scr, att_scr, ds_scr, sbf_scr,
                lay, tile, alpha):
    d, h_n, dk, dv, cw = lay.d, lay.h, lay.dk, lay.dv, lay.cw
    ck = cw_ref.shape[0]
    halo = SUBLANES
    chunks = [(c, c * GLA_CHUNK, (c + 1) * GLA_CHUNK) for c in range(tile // GLA_CHUNK)]
    heads = [(h, slice(h * dk, (h + 1) * dk), slice(h * dv, (h + 1) * dv)) for h in range(h_n)]

    a_lr = z_scr[:, lay.off_a:lay.off_a + lay.rank_pad].astype(BF16)
    la = jnp.dot(a_lr, w_a2_ref[...], preferred_element_type=F32) + b_a_ref[...]
    la_scr[...] = _log_sigmoid(la) * (1.0 / GLA_TAU)
    yield 4

    row = lax.broadcasted_iota(jnp.int32, (GLA_CHUNK, lay.qk), 0)
    causal = (lax.broadcasted_iota(jnp.int32, (GLA_CHUNK, GLA_CHUNK), 0)
              >= lax.broadcasted_iota(jnp.int32, (GLA_CHUNK, GLA_CHUNK), 1))
    q_scale = dk ** -0.5
    gng = gng_ref[...]

    for c, r0, r1 in chunks:
        b = la_scr[r0:r1, :]
        s = 1
        while s < GLA_CHUNK:
            b = b + jnp.where(row >= s, pltpu.roll(b, s, 0), 0.0)
            s *= 2
        bl = b[GLA_CHUNK - 1:GLA_CHUNK, :]
        q_in = z_scr[r0:r1, lay.off_q:lay.off_q + lay.qk] * q_scale * jnp.exp(b)
        k = z_scr[r0:r1, lay.off_k:lay.off_k + lay.qk]
        qin_scr[r0:r1, :] = q_in.astype(BF16)
        kin_scr[r0:r1, :] = (k * jnp.exp(-b)).astype(BF16)
        kst_scr[r0:r1, :] = (k * jnp.exp(bl - b)).astype(BF16)
        dec_scr[c:c + 1, :] = jnp.exp(bl)
        vb_scr[r0:r1, :] = z_scr[r0:r1, lay.off_v:lay.off_v + lay.vt].astype(BF16)
        yield 4

    for c, r0, r1 in chunks:
        for h, ks, vs in heads:
            att = lax.dot_general(qin_scr[r0:r1, ks], kin_scr[r0:r1, ks],
                                  (((1,), (1,)), ((), ())), preferred_element_type=F32)
            att_scr[h, r0:r1, :] = jnp.where(causal, att, 0.0).astype(BF16)
        yield 1

    for h, ks, vs in heads:
        for c, r0, r1 in chunks:
            ds_scr[c * h_n + h] = lax.dot_general(kst_scr[r0:r1, ks], vb_scr[r0:r1, vs],
                                                  (((0,), (0,)), ((), ())),
                                                  preferred_element_type=F32)
        yield 1
    for h, ks, vs in heads:
        state = s_scr[h]
        for c, r0, r1 in chunks:
            sbf_scr[c * h_n + h] = state.astype(BF16)
            dec = jnp.transpose(jnp.broadcast_to(dec_scr[c:c + 1, ks], (dk, dk)))
            dec = jnp.concatenate([dec] * (dv // dk), axis=1)
            state = dec * state + ds_scr[c * h_n + h]
        s_scr[h] = state
        yield 2

    for c, r0, r1 in chunks:
        for h, ks, vs in heads:
            lhs = jnp.concatenate([qin_scr[r0:r1, ks], att_scr[h, r0:r1, :]], axis=1)
            rhs = jnp.concatenate([sbf_scr[c * h_n + h], vb_scr[r0:r1, vs]], axis=0)
            z_scr[r0:r1, vs] = jnp.dot(lhs, rhs, preferred_element_type=F32)
        yield 1

    for c, r0, r1 in chunks:
        for h, ks, vs in heads:
            o = z_scr[r0:r1, vs]
            ms = jnp.mean(o * o, axis=-1, keepdims=True)
            on = o * lax.rsqrt(ms + EPS) * gng
            gg = z_scr[r0:r1, lay.off_gg + h * dv:lay.off_gg + (h + 1) * dv]
            og_scr[r0:r1, vs] = (on * _silu(gg)).astype(BF16)
        yield 3

    for c, r0, r1 in chunks:
        p = (z_scr[r0:r1, lay.off_cc:lay.off_cc + cw]
             * z_scr[r0:r1, lay.off_ch:lay.off_ch + cw])
        p_scr[halo + r0:halo + r1, :] = p
        yc = cw_ref[ck - 1:ck, :] * p
        for i in range(ck - 1):
            lag = ck - 1 - i
            yc = yc + cw_ref[i:i + 1, :] * p_scr[halo + r0 - lag:halo + r1 - lag, :]
        yc = (z_scr[r0:r1, lay.off_cb:lay.off_cb + cw] * yc
              * _silu(z_scr[r0:r1, lay.off_gc:lay.off_gc + cw]))
        yc_scr[r0:r1, :] = yc.astype(BF16)
        yield 2

    p_scr[0:halo, :] = p_scr[tile:tile + halo, :]

    z_scr[:, 0:d] = jnp.dot(og_scr[...], w_pa_ref[...], preferred_element_type=F32)
    yield 0
    z_scr[:, d:2 * d] = jnp.dot(yc_scr[...], w_pb_ref[...], preferred_element_type=F32)
    yield 0
    for c, r0, r1 in chunks:
        merged = (jax.nn.sigmoid(z_scr[r0:r1, lay.off_mg:lay.off_mg + d]) * z_scr[r0:r1, 0:d]
                  + jax.nn.sigmoid(z_scr[r0:r1, lay.off_mc:lay.off_mc + d]) * z_scr[r0:r1, d:2 * d])
        mg_scr[r0:r1, :] = merged.astype(BF16)
        yield 2
    z_scr[:, 2 * d:3 * d] = jnp.dot(mg_scr[...], w_o_ref[...], preferred_element_type=F32)
    yield 0

    lng = lng_ref[...]
    lnb = lnb_ref[...]
    for c, r0, r1 in chunks:
        hres = alpha * x_ref[0, row0 + r0:row0 + r1, :] + gate * z_scr[r0:r1, 2 * d:3 * d]
        mu = jnp.mean(hres, axis=-1, keepdims=True)
        dev = hres - mu
        var = jnp.mean(dev * dev, axis=-1, keepdims=True)
        y_ref[0, row0 + r0:row0 + r1, :] = dev * lax.rsqrt(var + EPS) * lng + lnb
        yield 3


def _mix_weight(lay, tile):
    chunks = tile // GLA_CHUNK
    return 4 + chunks * (4 + 1 + 1 + 3 + 2 + 2 + 3) + lay.h * (1 + 2)


def _emit_overlapped(mix, mix_weight, proj, n_proj):
    done_w = done_p = 0
    for weight in mix:
        done_w += weight
        while done_p * mix_weight < done_w * n_proj:
            next(proj)
            done_p += 1
    assert done_w == mix_weight and done_p == n_proj


def _prompt_kernel(xc_ref, xn_ref, mod_ref, w_hd_ref, w_lr_ref, w_tl_ref, w_a2_ref, b_a_ref,
                   gng_ref, cw_ref, w_pa_ref, w_pb_ref, w_o_ref, lng_ref, lnb_ref, *rest,
                   lay, tile, alpha, n_t, n_tiles):
    (y_ref, sg_ref, sc_ref, ua_scr, ub_scr, za_scr, zb_scr, *mix_scr) = rest[-21:]
    p_scr, s_scr = mix_scr[1:3]
    j = pl.program_id(0)
    steps_per_seq = n_t // 2
    jj = lax.rem(j, steps_per_seq)
    b_cur = lax.div(j, steps_per_seq)
    b_next = lax.div(jnp.minimum(2 * j + 2, n_tiles - 1), n_t)
    d = lay.d
    ck = cw_ref.shape[0]
    halo = SUBLANES
    w_in_refs = (w_hd_ref, w_lr_ref, w_tl_ref)

    @pl.when(jj == 0)
    def _():
        s_scr[...] = jnp.zeros_like(s_scr)
        p_scr[0:halo, :] = jnp.zeros((halo, lay.cw), F32)

    def project(x_rows, brow, u_scr, z_dst):
        return _project_stages(x_rows, mod_ref, brow, w_in_refs, u_scr, z_dst, lay)

    @pl.when(j == 0)
    def _():
        for _ in project(lambda: xc_ref[0, 0:tile, :], b_cur, ua_scr, za_scr):
            pass

    gate = mod_ref[pl.ds(b_cur, 1), 2 * d:3 * d]
    mix_args = (w_a2_ref, b_a_ref, gng_ref, cw_ref, w_pa_ref, w_pb_ref, w_o_ref, lng_ref,
                lnb_ref, *mix_scr, lay, tile, alpha)
    mix_weight = _mix_weight(lay, tile)
    n_proj = 1 + len(_project_plan(lay))
    _emit_overlapped(_mix_stages(za_scr, xc_ref, y_ref, 0, gate, *mix_args), mix_weight,
                     project(lambda: xc_ref[0, tile:2 * tile, :], b_cur, ub_scr, zb_scr), n_proj)
    _emit_overlapped(_mix_stages(zb_scr, xc_ref, y_ref, tile, gate, *mix_args), mix_weight,
                     project(lambda: xn_ref[0], b_next, ua_scr, za_scr), n_proj)

    @pl.when(jj == steps_per_seq - 1)
    def _():
        sg_ref[0] = s_scr[...]
        sc_ref[0] = p_scr[halo - (ck - 1):halo, :]


def _prompt_vmem_bytes(lay, tile):
    weights = 2 * (lay.d * lay.n_pad + lay.rank_pad * lay.qk + lay.vt * lay.d
                   + lay.cw * lay.d + lay.d * lay.d)
    io = 2 * (2 * tile + tile + 2 * tile) * lay.d * 4 + 2 * (lay.h * lay.dk * lay.dv * 4)
    chunk_heads = (tile // GLA_CHUNK) * lay.h
    scratch = (2 * tile * lay.d * 2 + 2 * tile * lay.n_pad * 4 + tile * lay.qk * 4
               + (tile + SUBLANES) * lay.cw * 4 + lay.h * lay.dk * lay.dv * 4
               + tile * lay.vt * 2 + tile * lay.cw * 2 + tile * lay.d * 2
               + 3 * tile * lay.qk * 2 + tile * lay.vt * 2 + SUBLANES * lay.qk * 4
               + lay.h * tile * LANES * 2 + chunk_heads * lay.dk * lay.dv * 6)
    temporaries = tile * lay.n_pad * 2
    return min(weights + io + scratch + temporaries, VMEM_PHYSICAL_BYTES * 15 // 16)


def _prompt_layer(x, mod_p, params, prev_states, layer, lay, alpha):
    (w_hd, w_lr, w_tl, w_a2, b_a, gng, conv_w, w_pa, w_pb, w_o, ln_g, ln_b) = params
    bsz, seq, d = x.shape
    depth = w_hd.shape[0]
    tile = PROMPT_TILE
    assert seq % (2 * tile) == 0 and tile % GLA_CHUNK == 0
    ck = conv_w.shape[1]
    n_t = seq // tile
    n_tiles = bsz * n_t
    steps_per_seq = n_t // 2
    n_chunk_heads = (tile // GLA_CHUNK) * lay.h
    assert tile // GLA_CHUNK <= SUBLANES

    def const(arr, single_buffer=True):
        return _layer_spec(arr.shape[1:], layer, single_buffer)

    def next_tile(j):
        g = jnp.minimum(2 * j + 2, n_tiles - 1)
        return (g // n_t, g % n_t, 0)

    in_specs = [
        pl.BlockSpec((1, 2 * tile, d), lambda j: (j // steps_per_seq, j % steps_per_seq, 0)),
        pl.BlockSpec((1, tile, d), next_tile),
        const(mod_p, False),
    ] + [const(a) for a in params]
    args = [x, x, mod_p, *params]
    aliases = {}
    if prev_states is not None:
        for i, prev in enumerate(prev_states):
            in_specs.append(pl.BlockSpec(memory_space=pl.ANY))
            aliases[len(args)] = 1 + i
            args.append(prev)

    kern = functools.partial(_prompt_kernel, lay=lay, tile=tile, alpha=alpha, n_t=n_t,
                             n_tiles=n_tiles)
    return pl.pallas_call(
        kern,
        grid=(n_tiles // 2,),
        in_specs=in_specs,
        out_specs=[
            pl.BlockSpec((1, 2 * tile, d), lambda j: (j // steps_per_seq, j % steps_per_seq, 0)),
            pl.BlockSpec((None, 1, lay.h, lay.dk, lay.dv),
                         lambda j: (layer, j // steps_per_seq, 0, 0, 0)),
            pl.BlockSpec((None, 1, ck - 1, lay.cw), lambda j: (layer, j // steps_per_seq, 0, 0)),
        ],
        out_shape=[
            jax.ShapeDtypeStruct((bsz, seq, d), F32),
            jax.ShapeDtypeStruct((depth, bsz, lay.h, lay.dk, lay.dv), F32),
            jax.ShapeDtypeStruct((depth, bsz, ck - 1, lay.cw), F32),
        ],
        scratch_shapes=[
            pltpu.VMEM((tile, d), BF16),
            pltpu.VMEM((tile, d), BF16),
            pltpu.VMEM((tile, lay.n_pad), F32),
            pltpu.VMEM((tile, lay.n_pad), F32),
            pltpu.VMEM((tile, lay.qk), F32),
            pltpu.VMEM((tile + SUBLANES, lay.cw), F32),
            pltpu.VMEM((lay.h, lay.dk, lay.dv), F32),
            pltpu.VMEM((tile, lay.vt), BF16),
            pltpu.VMEM((tile, lay.cw), BF16),
            pltpu.VMEM((tile, d), BF16),
            pltpu.VMEM((tile, lay.qk), BF16),
            pltpu.VMEM((tile, lay.qk), BF16),
            pltpu.VMEM((tile, lay.qk), BF16),
            pltpu.VMEM((tile, lay.vt), BF16),
            pltpu.VMEM((SUBLANES, lay.qk), F32),
            pltpu.VMEM((lay.h, tile, GLA_CHUNK), BF16),
            pltpu.VMEM((n_chunk_heads, lay.dk, lay.dv), F32),
            pltpu.VMEM((n_chunk_heads, lay.dk, lay.dv), BF16),
        ],
        input_output_aliases=aliases,
        compiler_params=pltpu.CompilerParams(
            dimension_semantics=("arbitrary",),
            vmem_limit_bytes=_prompt_vmem_bytes(lay, tile)),
        name="prompt_layer",
    )(*args)


def _sample_proj_kernel(x_ref, mod_ref, w_hd_ref, w_lr_ref, w_tl_ref, w_a2_ref, b_a_ref,
                        z_ref, a_ref, *, lay):
    d = lay.d
    shift = mod_ref[:, 0:d]
    scale = mod_ref[:, d:2 * d]
    u = (x_ref[...] * (1.0 + scale) + shift).astype(BF16)
    w_in_refs = (w_hd_ref, w_lr_ref, w_tl_ref)
    for piece, s0, s1, d0 in lay.groups:
        z_ref[:, d0:d0 + s1 - s0] = jnp.dot(u, w_in_refs[piece][:, s0:s1],
                                            preferred_element_type=F32)
    a_lr = z_ref[:, lay.off_a:lay.off_a + lay.rank_pad].astype(BF16)
    la = jnp.dot(a_lr, w_a2_ref[...], preferred_element_type=F32) + b_a_ref[...]
    a_ref[...] = jnp.exp(_log_sigmoid(la) * (1.0 / GLA_TAU))


def _sample_proj(x3, mod_s, w_hd, w_lr, w_tl, w_a2, b_a, layer, lay):
    n, _, d = x3.shape
    kern = functools.partial(_sample_proj_kernel, lay=lay)
    weights = (w_hd, w_lr, w_tl, w_a2, b_a)
    vmem = 2 * d * lay.n_pad + 2 * 4 * n * lay.n_pad + 16 * MIB
    return pl.pallas_call(
        kern,
        grid=(1,),
        in_specs=[pl.BlockSpec((n, None, d), lambda i: (0, 0, 0)),
                  _layer_spec(mod_s.shape[1:], layer)]
        + [_layer_spec(w.shape[1:], layer, True) for w in weights],
        out_specs=[pl.BlockSpec((n, lay.n_pad), lambda i: (0, 0)),
                   pl.BlockSpec((n, lay.qk), lambda i: (0, 0))],
        out_shape=[jax.ShapeDtypeStruct((n, lay.n_pad), F32),
                   jax.ShapeDtypeStruct((n, lay.qk), F32)],
        compiler_params=pltpu.CompilerParams(dimension_semantics=("arbitrary",),
                                             vmem_limit_bytes=vmem),
        name="sample_proj",
    )(x3, mod_s, *weights)


def _sample_state_kernel(a_ref, qkv_ref, s_ref, *rest, lay, bb):
    s_out_ref, o_ref = rest[-2:]
    h_n, dk, dv = lay.h, lay.dk, lay.dv
    q_scale = dk ** -0.5
    for i in range(bb):
        rows = [a_ref[i:i + 1, h * dk:(h + 1) * dk] for h in range(h_n)]
        rows += [qkv_ref[i:i + 1, lay.off_k + h * dk:lay.off_k + (h + 1) * dk]
                 for h in range(h_n)]
        rows += [qkv_ref[i:i + 1, lay.off_q + h * dk:lay.off_q + (h + 1) * dk]
                 for h in range(h_n)]
        rows.append(jnp.zeros((dk - len(rows), dk), F32))
        mt = jnp.transpose(jnp.concatenate(rows, axis=0))
        for h in range(h_n):
            a_col = mt[:, h:h + 1]
            k_col = mt[:, h_n + h:h_n + h + 1]
            q_col = mt[:, 2 * h_n + h:2 * h_n + h + 1]
            v = qkv_ref[i:i + 1, lay.off_v + h * dv:lay.off_v + (h + 1) * dv]
            s_new = a_col * s_ref[i, h] + k_col * v
            s_out_ref[i, h] = s_new
            o_ref[i:i + 1, h * dv:(h + 1) * dv] = q_scale * jnp.sum(q_col * s_new, axis=0,
                                                                   keepdims=True)


def _sample_state(a, z, state_all, prev_out, layer, lay):
    depth, n = state_all.shape[:2]
    bb = SAMPLE_STATE_ROWS
    assert n % bb == 0
    kern = functools.partial(_sample_state_kernel, lay=lay, bb=bb)
    state_spec = pl.BlockSpec((None, bb, lay.h, lay.dk, lay.dv), lambda i: (layer, i, 0, 0, 0))
    in_specs = [
        pl.BlockSpec((bb, lay.qk), lambda i: (i, 0)),
        pl.BlockSpec((bb, lay.off_gg), lambda i: (i, 0)),
        state_spec,
    ]
    args = [a, z, state_all]
    aliases = {}
    if prev_out is not None:
        in_specs.append(pl.BlockSpec(memory_space=pl.ANY))
        aliases = {len(args): 0}
        args.append(prev_out)
    block_bytes = bb * lay.h * lay.dk * lay.dv * 4
    return pl.pallas_call(
        kern,
        grid=(n // bb,),
        in_specs=in_specs,
        out_specs=[state_spec, pl.BlockSpec((bb, lay.vt), lambda i: (i, 0))],
        out_shape=[jax.ShapeDtypeStruct(state_all.shape, F32),
                   jax.ShapeDtypeStruct((n, lay.vt), F32)],
        input_output_aliases=aliases,
        compiler_params=pltpu.CompilerParams(dimension_semantics=("arbitrary",),
                                             vmem_limit_bytes=4 * block_bytes + 16 * MIB),
        name="sample_state",
    )(*args)


def _sample_out_kernel(o_ref, z_ref, x_ref, mod_ref, buf_ref, gng_ref, cw_ref,
                       w_pa_ref, w_pb_ref, w_o_ref, lng_ref, lnb_ref,
                       y_ref, buf_out_ref, og_scr, *, lay, alpha):
    d, h_n, dv, cw = lay.d, lay.h, lay.dv, lay.cw
    ck = cw_ref.shape[0]
    gate = mod_ref[:, 2 * d:3 * d]
    gng = gng_ref[...]
    for h in range(h_n):
        o = o_ref[:, h * dv:(h + 1) * dv]
        ms = jnp.mean(o * o, axis=-1, keepdims=True)
        on = o * lax.rsqrt(ms + EPS) * gng
        gg = z_ref[:, lay.off_gg + h * dv:lay.off_gg + (h + 1) * dv]
        og_scr[:, h * dv:(h + 1) * dv] = (on * _silu(gg)).astype(BF16)
    pa = jnp.dot(og_scr[...], w_pa_ref[...], preferred_element_type=F32)

    p = z_ref[:, lay.off_cc:lay.off_cc + cw] * z_ref[:, lay.off_ch:lay.off_ch + cw]
    yc = cw_ref[ck - 1:ck, :] * p
    for i in range(ck - 1):
        yc = yc + cw_ref[i:i + 1, :] * buf_ref[:, i * cw:(i + 1) * cw]
    yc = z_ref[:, lay.off_cb:lay.off_cb + cw] * yc * _silu(z_ref[:, lay.off_gc:lay.off_gc + cw])
    pb = jnp.dot(yc.astype(BF16), w_pb_ref[...], preferred_element_type=F32)
    for i in range(ck - 2):
        buf_out_ref[:, i * cw:(i + 1) * cw] = buf_ref[:, (i + 1) * cw:(i + 2) * cw]
    buf_out_ref[:, (ck - 2) * cw:(ck - 1) * cw] = p

    merged = (jax.nn.sigmoid(z_ref[:, lay.off_mg:lay.off_mg + d]) * pa
              + jax.nn.sigmoid(z_ref[:, lay.off_mc:lay.off_mc + d]) * pb)
    out = jnp.dot(merged.astype(BF16), w_o_ref[...], preferred_element_type=F32)
    hres = alpha * x_ref[...] + gate * out
    mu = jnp.mean(hres, axis=-1, keepdims=True)
    dev = hres - mu
    var = jnp.mean(dev * dev, axis=-1, keepdims=True)
    y_ref[...] = dev * lax.rsqrt(var + EPS) * lng_ref[...] + lnb_ref[...]


def _sample_out(o, z, x3, mod_s, buf, gng, conv_w, w_pa, w_pb, w_o, ln_g, ln_b, layer, lay,
                alpha):
    n, _, d = x3.shape
    kern = functools.partial(_sample_out_kernel, lay=lay, alpha=alpha)
    stacked = (gng, conv_w, w_pa, w_pb, w_o, ln_g, ln_b)

    def whole(arr):
        return pl.BlockSpec(arr.shape, lambda i: (0,) * arr.ndim)

    return pl.pallas_call(
        kern,
        grid=(1,),
        in_specs=[whole(o), whole(z), pl.BlockSpec((n, None, d), lambda i: (0, 0, 0)),
                  _layer_spec(mod_s.shape[1:], layer), whole(buf)]
        + [_layer_spec(w.shape[1:], layer) for w in stacked],
        out_specs=[pl.BlockSpec((n, None, d), lambda i: (0, 0, 0)),
                   pl.BlockSpec(buf.shape, lambda i: (0, 0))],
        out_shape=[jax.ShapeDtypeStruct((n, 1, d), F32),
                   jax.ShapeDtypeStruct(buf.shape, F32)],
        scratch_shapes=[pltpu.VMEM((n, lay.vt), BF16)],
        compiler_params=pltpu.CompilerParams(dimension_semantics=("arbitrary",),
                                             vmem_limit_bytes=48 * MIB),
        name="sample_out",
    )(o, z, x3, mod_s, buf, *stacked)


def kernel(x_prompt, x_sample, c_prompt, c_sample, state_gla, state_conv, w_ada, b_ada, w_in,
           w_a2, b_a, gla_norm_g, conv_w, w_pa, w_pb, w_o, ln_g, ln_b):
    depth, n_s, heads, dk, dv = state_gla.shape
    d = x_prompt.shape[2]
    rank = w_a2.shape[1]
    ck, cw = conv_w.shape[1:]
    lay = _Layout(d, heads, dk, dv, rank, cw)
    alpha = (2 * depth) ** 0.25

    mod_p, mod_s = _ada_mod(c_prompt, c_sample, w_ada, b_ada)

    w_hd, w_lr, w_tl = _split_w_in(w_in, lay)
    w_a2_p = jnp.concatenate(
        [w_a2, jnp.zeros((depth, lay.rank_pad - rank, lay.qk), w_a2.dtype)], axis=1).astype(BF16)
    w_pa_b, w_pb_b, w_o_b = _cast_bf16(w_pa), _cast_bf16(w_pb), _cast_bf16(w_o)
    b_a3 = b_a.reshape(depth, 1, -1)
    gng3 = gla_norm_g.reshape(depth, 1, -1)
    lng3 = ln_g.reshape(depth, 1, -1)
    lnb3 = ln_b.reshape(depth, 1, -1)
    params = (w_hd, w_lr, w_tl, w_a2_p, b_a3, gng3, conv_w, w_pa_b, w_pb_b, w_o_b, lng3, lnb3)

    hp, hs = x_prompt, x_sample
    prompt_states = None
    gla_s = None
    conv_s = []
    for l in range(depth):
        hp, *prompt_states = _prompt_layer(hp, mod_p, params, prompt_states, l, lay, alpha)
        z, a = _sample_proj(hs, mod_s, w_hd, w_lr, w_tl, w_a2_p, b_a3, l, lay)
        gla_s, o = _sample_state(a, z, state_gla, gla_s, l, lay)
        hs, buf_new = _sample_out(o, z, hs, mod_s, state_conv[l].reshape(n_s, (ck - 1) * cw),
                                  gng3, conv_w, w_pa_b, w_pb_b, w_o_b, lng3, lnb3, l, lay, alpha)
        conv_s.append(buf_new.reshape(n_s, ck - 1, cw))
    return (hp, hs, prompt_states[0], prompt_states[1], gla_s, jnp.stack(conv_s))
```

```python
import functools

import jax
import jax.numpy as jnp
from jax import lax
from jax.experimental import pallas as pl
from jax.experimental.pallas import tpu as pltpu

F32 = jnp.float32
BF16 = jnp.bfloat16

GLA_TAU = 16.0
GLA_CHUNK = 64
EPS = 1e-5

LANES = 128
SUBLANES = 8
VMEM_PHYSICAL_BYTES = 64 * 1024 * 1024
MIB = 1024 * 1024

PROMPT_TILE = 256
PROJ_COLS = 512
SAMPLE_STATE_ROWS = 8
PREP_COLS = 512


LOG2_E = 1.4426950408889634


def _log_sigmoid(x):
    return jnp.minimum(x, 0.0) - jnp.log(1.0 + jnp.exp(-jnp.abs(x)))


def _cumsum_rows(x):
    row = lax.broadcasted_iota(jnp.int32, (SUBLANES, x.shape[1]), 0)
    groups = []
    carry = None
    for g in range(x.shape[0] // SUBLANES):
        t = x[g * SUBLANES:(g + 1) * SUBLANES, :]
        s = 1
        while s < SUBLANES:
            t = t + jnp.where(row >= s, pltpu.roll(t, s, 0), 0.0)
            s *= 2
        if carry is not None:
            t = t + carry
        carry = jnp.broadcast_to(t[SUBLANES - 1:SUBLANES, :], t.shape)
        groups.append(t)
    return jnp.concatenate(groups, axis=0)


def _silu(x):
    return x * jax.nn.sigmoid(x)


PACKED = jnp.uint32
PACK = 2


def _pack_rows(w_bf16):
    return pltpu.bitcast(w_bf16, PACKED)


def _unpack_rows(w_packed):
    return pltpu.bitcast(w_packed, BF16)


class _Layout:
    def __init__(self, d_model, heads, dk, dv, rank, conv_w):
        self.d = d_model
        self.h = heads
        self.dk = dk
        self.dv = dv
        self.rank = rank
        self.cw = conv_w
        self.qk = heads * dk
        self.vt = heads * dv
        self.rank_pad = -(-rank // LANES) * LANES
        self.off_q = 0
        self.off_k = self.off_q + self.qk
        self.off_v = self.off_k + self.qk
        self.off_gg = self.off_v + self.vt
        self.off_a = self.off_gg + self.vt
        self.off_cb = self.off_a + self.rank_pad
        self.off_cc = self.off_cb + conv_w
        self.off_ch = self.off_cc + conv_w
        self.off_gc = self.off_ch + conv_w
        self.off_mg = self.off_gc + conv_w
        self.off_mc = self.off_mg + d_model
        self.n_pad = self.off_mc + d_model
        self.n_head = self.off_a
        self.n_tail = self.n_pad - self.off_cb
        self.groups = (
            (0, self.off_q, self.off_v, self.off_q),
            (0, self.off_v, self.off_gg, self.off_v),
            (0, self.off_gg, self.off_a, self.off_gg),
            (1, 0, self.rank_pad, self.off_a),
            (2, 0, 2 * conv_w, self.off_cb),
            (2, 2 * conv_w, 4 * conv_w, self.off_ch),
            (2, 4 * conv_w, 4 * conv_w + d_model, self.off_mg),
            (2, 4 * conv_w + d_model, self.n_tail, self.off_mc))


def _layer_spec(shape, layer, single_buffer=False):
    zeros = (0,) * len(shape)
    kwargs = {"pipeline_mode": pl.Buffered(1)} if single_buffer else {}
    return pl.BlockSpec((None,) + tuple(shape), lambda *_: (layer,) + zeros, **kwargs)


def _transpose_cast_kernel(w_ref, o_ref, *, valid):
    wt = jnp.transpose(w_ref[0])
    if valid < wt.shape[1]:
        lane = lax.broadcasted_iota(jnp.int32, wt.shape, 1)
        wt = jnp.where(lane < valid, wt, 0.0)
    o_ref[...] = _pack_rows(wt.astype(BF16))


def _transpose_cast(w_t, col0, n_valid, n_out, cols):
    depth, n_in, d = w_t.shape
    assert n_out % cols == 0 and col0 % SUBLANES == 0 and col0 + n_out <= n_in
    kern = functools.partial(_transpose_cast_kernel, valid=min(n_valid, cols))
    return pl.pallas_call(
        kern,
        grid=(depth, n_out // cols),
        in_specs=[pl.BlockSpec((pl.Element(1), pl.Element(cols), pl.Element(d)),
                               lambda l, j: (l, pl.multiple_of(col0 + j * cols, SUBLANES), 0))],
        out_specs=pl.BlockSpec((None, d // PACK, cols), lambda l, j: (l, 0, j)),
        out_shape=jax.ShapeDtypeStruct((depth, d // PACK, n_out), PACKED),
        compiler_params=pltpu.CompilerParams(
            dimension_semantics=("arbitrary", "arbitrary")),
        name="transpose_cast",
    )(w_t)


def _split_w_in(w_in, lay):
    n_in = w_in.shape[2]
    a_end = lay.off_a + lay.rank
    assert n_in == a_end + lay.n_tail and lay.rank <= lay.rank_pad == LANES
    w_t = jnp.swapaxes(w_in, 1, 2)
    head = _transpose_cast(w_t, 0, lay.n_head, lay.n_head, PREP_COLS)
    low = _transpose_cast(w_t, lay.off_a, lay.rank, lay.rank_pad, lay.rank_pad)
    tail = _transpose_cast(w_t, a_end, lay.n_tail, lay.n_tail, PREP_COLS)
    return head, low, tail


def _cast_kernel(w_ref, o_ref):
    o_ref[...] = _pack_rows(w_ref[...].astype(BF16))


def _cast_bf16(w):
    depth, rows, cols = w.shape
    return pl.pallas_call(
        _cast_kernel,
        grid=(depth,),
        in_specs=[pl.BlockSpec((None, rows, cols), lambda l: (l, 0, 0))],
        out_specs=pl.BlockSpec((None, rows // PACK, cols), lambda l: (l, 0, 0)),
        out_shape=jax.ShapeDtypeStruct((depth, rows // PACK, cols), PACKED),
        compiler_params=pltpu.CompilerParams(
            dimension_semantics=("arbitrary",),
            vmem_limit_bytes=2 * rows * cols * 6 + 16 * MIB),
        name="cast_bf16",
    )(w)


def _ada_kernel(cp_ref, cs_ref, w_ref, b_ref, op_ref, os_ref):
    w = w_ref[...].astype(BF16)
    b = b_ref[...]
    op_ref[...] = jnp.dot(cp_ref[...].astype(BF16), w, preferred_element_type=F32) + b
    os_ref[...] = jnp.dot(cs_ref[...].astype(BF16), w, preferred_element_type=F32) + b


def _ada_mod(c_prompt, c_sample, w_ada, b_ada):
    depth, d, n3 = w_ada.shape
    bp, ns = c_prompt.shape[0], c_sample.shape[0]
    tn = d
    return pl.pallas_call(
        _ada_kernel,
        grid=(depth, n3 // tn),
        in_specs=[
            pl.BlockSpec((bp, d), lambda l, j: (0, 0)),
            pl.BlockSpec((ns, d), lambda l, j: (0, 0)),
            pl.BlockSpec((None, d, tn), lambda l, j: (l, 0, j)),
            pl.BlockSpec((None, 1, tn), lambda l, j: (l, 0, j)),
        ],
        out_specs=[pl.BlockSpec((None, bp, tn), lambda l, j: (l, 0, j)),
                   pl.BlockSpec((None, ns, tn), lambda l, j: (l, 0, j))],
        out_shape=[jax.ShapeDtypeStruct((depth, bp, n3), F32),
                   jax.ShapeDtypeStruct((depth, ns, n3), F32)],
        compiler_params=pltpu.CompilerParams(
            dimension_semantics=("arbitrary", "arbitrary")),
        name="ada_mod",
    )(c_prompt, c_sample, w_ada, b_ada.reshape(depth, 1, n3))


def _project_plan(lay):
    plan = []
    for piece, s0, s1, d0 in lay.groups:
        for c0 in range(s0, s1, PROJ_COLS):
            plan.append((piece, c0, min(c0 + PROJ_COLS, s1), d0 + c0 - s0))
    return plan


def _project_stages(x_rows, mod_ref, brow, w_in_refs, u_scr, z_dst, lay):
    d = lay.d
    shift = mod_ref[pl.ds(brow, 1), 0:d]
    scale = mod_ref[pl.ds(brow, 1), d:2 * d]
    u_scr[...] = (x_rows() * (1.0 + scale) + shift).astype(BF16)
    yield
    for piece, c0, c1, d0 in _project_plan(lay):
        z_dst[:, d0:d0 + c1 - c0] = jnp.dot(u_scr[...],
                                            _unpack_rows(w_in_refs[piece][:, c0:c1]),
                                            preferred_element_type=F32)
        yield


def _mix_stages(z_scr, x_ref, y_ref, row0, gate, w_a2_ref, b_a_ref, gng_ref, cw_ref,
                w_pa_ref, w_pb_ref, w_o_ref, lng_ref, lnb_ref,
                la_scr, p_scr, s_scr, og_scr, yc_scr, mg_scr,
                qin_scr, kin_scr, kst_scr, vb_scr, dec_scr, att_scr, ds_scr, sbf_scr,
                lay, tile, alpha):
    d, h_n, dk, dv, cw = lay.d, lay.h, lay.dk, lay.dv, lay.cw
    ck = cw_ref.shape[0]
    halo = SUBLANES
    chunks = [(c, c * GLA_CHUNK, (c + 1) * GLA_CHUNK) for c in range(tile // GLA_CHUNK)]
    heads = [(h, slice(h * dk, (h + 1) * dk), slice(h * dv, (h + 1) * dv)) for h in range(h_n)]

    a_lr = z_scr[:, lay.off_a:lay.off_a + lay.rank_pad].astype(BF16)
    la = jnp.dot(a_lr, _unpack_rows(w_a2_ref[...]), preferred_element_type=F32) + b_a_ref[...]
    la_scr[...] = _log_sigmoid(la) * (1.0 / GLA_TAU)
    yield 4

    causal = (lax.broadcasted_iota(jnp.int32, (GLA_CHUNK, GLA_CHUNK), 0)
              >= lax.broadcasted_iota(jnp.int32, (GLA_CHUNK, GLA_CHUNK), 1))
    q_scale = dk ** -0.5
    gng = gng_ref[...]

    for c, r0, r1 in chunks:
        for h, ks, vs in heads:
            b2 = _cumsum_rows(la_scr[r0:r1, ks]) * LOG2_E
            bl2 = b2[GLA_CHUNK - 1:GLA_CHUNK, :]
            q_in = z_scr[r0:r1, lay.off_q + h * dk:lay.off_q + (h + 1) * dk] * q_scale
            k = z_scr[r0:r1, lay.off_k + h * dk:lay.off_k + (h + 1) * dk]
            qin_scr[r0:r1, ks] = (q_in * jnp.exp2(b2)).astype(BF16)
            kin_scr[r0:r1, ks] = (k * jnp.exp2(-b2)).astype(BF16)
            kst_scr[r0:r1, ks] = (k * jnp.exp2(bl2 - b2)).astype(BF16)
            dec_scr[c:c + 1, ks] = jnp.exp2(bl2)
            vb_scr[r0:r1, vs] = z_scr[r0:r1, lay.off_v + h * dv:lay.off_v + (h + 1) * dv] \
                .astype(BF16)
            yield 1

    for c, r0, r1 in chunks:
        for h, ks, vs in heads:
            att = lax.dot_general(qin_scr[r0:r1, ks], kin_scr[r0:r1, ks],
                                  (((1,), (1,)), ((), ())), preferred_element_type=F32)
            att_scr[h, r0:r1, :] = jnp.where(causal, att, 0.0).astype(BF16)
        yield 1

    for h, ks, vs in heads:
        for c, r0, r1 in chunks:
            ds_scr[c * h_n + h] = lax.dot_general(kst_scr[r0:r1, ks], vb_scr[r0:r1, vs],
                                                  (((0,), (0,)), ((), ())),
                                                  preferred_element_type=F32)
        yield 1
    for h, ks, vs in heads:
        state = s_scr[h]
        for c, r0, r1 in chunks:
            sbf_scr[c * h_n + h] = state.astype(BF16)
            dec = jnp.transpose(jnp.broadcast_to(dec_scr[c:c + 1, ks], (dk, dk)))
            dec = jnp.concatenate([dec] * (dv // dk), axis=1)
            state = dec * state + ds_scr[c * h_n + h]
        s_scr[h] = state
        yield 2

    for c, r0, r1 in chunks:
        for h, ks, vs in heads:
            lhs = jnp.concatenate([qin_scr[r0:r1, ks], att_scr[h, r0:r1, :]], axis=1)
            rhs = jnp.concatenate([sbf_scr[c * h_n + h], vb_scr[r0:r1, vs]], axis=0)
            z_scr[r0:r1, vs] = jnp.dot(lhs, rhs, preferred_element_type=F32)
        yield 1

    for c, r0, r1 in chunks:
        for h, ks, vs in heads:
            o = z_scr[r0:r1, vs]
            ms = jnp.mean(o * o, axis=-1, keepdims=True)
            on = o * lax.rsqrt(ms + EPS) * gng
            gg = z_scr[r0:r1, lay.off_gg + h * dv:lay.off_gg + (h + 1) * dv]
            og_scr[r0:r1, vs] = (on * _silu(gg)).astype(BF16)
        yield 3

    for c, r0, r1 in chunks:
        p = (z_scr[r0:r1, lay.off_cc:lay.off_cc + cw]
             * z_scr[r0:r1, lay.off_ch:lay.off_ch + cw])
        p_scr[halo + r0:halo + r1, :] = p
        yc = cw_ref[ck - 1:ck, :] * p
        for i in range(ck - 1):
            lag = ck - 1 - i
            yc = yc + cw_ref[i:i + 1, :] * p_scr[halo + r0 - lag:halo + r1 - lag, :]
        yc = (z_scr[r0:r1, lay.off_cb:lay.off_cb + cw] * yc
              * _silu(z_scr[r0:r1, lay.off_gc:lay.off_gc + cw]))
        yc_scr[r0:r1, :] = yc.astype(BF16)
        yield 2

    p_scr[0:halo, :] = p_scr[tile:tile + halo, :]

    z_scr[:, 0:d] = jnp.dot(og_scr[...], _unpack_rows(w_pa_ref[...]), preferred_element_type=F32)
    yield 0
    z_scr[:, d:2 * d] = jnp.dot(yc_scr[...], _unpack_rows(w_pb_ref[...]),
                                preferred_element_type=F32)
    yield 0
    for c, r0, r1 in chunks:
        merged = (jax.nn.sigmoid(z_scr[r0:r1, lay.off_mg:lay.off_mg + d]) * z_scr[r0:r1, 0:d]
                  + jax.nn.sigmoid(z_scr[r0:r1, lay.off_mc:lay.off_mc + d]) * z_scr[r0:r1, d:2 * d])
        mg_scr[r0:r1, :] = merged.astype(BF16)
        yield 2
    z_scr[:, 2 * d:3 * d] = jnp.dot(mg_scr[...], _unpack_rows(w_o_ref[...]),
                                    preferred_element_type=F32)
    yield 0

    lng = lng_ref[...]
    lnb = lnb_ref[...]
    for c, r0, r1 in chunks:
        hres = alpha * x_ref[0, row0 + r0:row0 + r1, :] + gate * z_scr[r0:r1, 2 * d:3 * d]
        mu = jnp.mean(hres, axis=-1, keepdims=True)
        dev = hres - mu
        var = jnp.mean(dev * dev, axis=-1, keepdims=True)
        y_ref[0, row0 + r0:row0 + r1, :] = dev * lax.rsqrt(var + EPS) * lng + lnb
        yield 3


def _mix_weights(lay, tile):
    chunks = tile // GLA_CHUNK
    return ([4] + [1] * (chunks * lay.h) + [1] * chunks + [1] * lay.h + [2] * lay.h + [1] * chunks
            + [3] * chunks + [2] * chunks + [0, 0] + [2] * chunks + [0] + [3] * chunks)


def _emit_overlapped(mix, mix_weights, proj, n_proj):
    total = sum(mix_weights)
    done_w = done_p = 0
    for weight in mix_weights:
        done_w += weight
        while done_p * total < done_w * n_proj:
            next(proj)
            done_p += 1
        assert next(mix) == weight
    assert done_p == n_proj and next(mix, None) is None


def _prompt_kernel(xc_ref, xn_ref, mod_ref, w_hd_ref, w_lr_ref, w_tl_ref, w_a2_ref, b_a_ref,
                   gng_ref, cw_ref, w_pa_ref, w_pb_ref, w_o_ref, lng_ref, lnb_ref, *rest,
                   lay, tile, alpha, n_t, n_tiles):
    (y_ref, sg_ref, sc_ref, ua_scr, ub_scr, za_scr, zb_scr, *mix_scr) = rest[-21:]
    p_scr, s_scr = mix_scr[1:3]
    j = pl.program_id(0)
    steps_per_seq = n_t // 2
    jj = lax.rem(j, steps_per_seq)
    b_cur = lax.div(j, steps_per_seq)
    b_next = lax.div(jnp.minimum(2 * j + 2, n_tiles - 1), n_t)
    d = lay.d
    ck = cw_ref.shape[0]
    halo = SUBLANES
    w_in_refs = (w_hd_ref, w_lr_ref, w_tl_ref)

    @pl.when(jj == 0)
    def _():
        s_scr[...] = jnp.zeros_like(s_scr)
        p_scr[0:halo, :] = jnp.zeros((halo, lay.cw), F32)

    def project(x_rows, brow, u_scr, z_dst):
        return _project_stages(x_rows, mod_ref, brow, w_in_refs, u_scr, z_dst, lay)

    @pl.when(j == 0)
    def _():
        for _ in project(lambda: xc_ref[0, 0:tile, :], b_cur, ua_scr, za_scr):
            pass

    gate = mod_ref[pl.ds(b_cur, 1), 2 * d:3 * d]
    mix_args = (w_a2_ref, b_a_ref, gng_ref, cw_ref, w_pa_ref, w_pb_ref, w_o_ref, lng_ref,
                lnb_ref, *mix_scr, lay, tile, alpha)
    mix_weight = _mix_weights(lay, tile)
    n_proj = 1 + len(_project_plan(lay))
    _emit_overlapped(_mix_stages(za_scr, xc_ref, y_ref, 0, gate, *mix_args), mix_weight,
                     project(lambda: xc_ref[0, tile:2 * tile, :], b_cur, ub_scr, zb_scr), n_proj)
    _emit_overlapped(_mix_stages(zb_scr, xc_ref, y_ref, tile, gate, *mix_args), mix_weight,
                     project(lambda: xn_ref[0], b_next, ua_scr, za_scr), n_proj)

    @pl.when(jj == steps_per_seq - 1)
    def _():
        sg_ref[0] = s_scr[...]
        sc_ref[0] = p_scr[halo - (ck - 1):halo, :]


def _prompt_vmem_bytes(lay, tile):
    weights = 2 * (lay.d * lay.n_pad + lay.rank_pad * lay.qk + lay.vt * lay.d
                   + lay.cw * lay.d + lay.d * lay.d)
    io = 2 * (2 * tile + tile + 2 * tile) * lay.d * 4 + 2 * (lay.h * lay.dk * lay.dv * 4)
    chunk_heads = (tile // GLA_CHUNK) * lay.h
    scratch = (2 * tile * lay.d * 2 + 2 * tile * lay.n_pad * 4 + tile * lay.qk * 4
               + (tile + SUBLANES) * lay.cw * 4 + lay.h * lay.dk * lay.dv * 4
               + tile * lay.vt * 2 + tile * lay.cw * 2 + tile * lay.d * 2
               + 3 * tile * lay.qk * 2 + tile * lay.vt * 2 + SUBLANES * lay.qk * 4
               + lay.h * tile * LANES * 2 + chunk_heads * lay.dk * lay.dv * 6)
    temporaries = tile * lay.n_pad * 2
    return min(weights + io + scratch + temporaries, VMEM_PHYSICAL_BYTES * 15 // 16)


def _prompt_layer(x, mod_p, params, prev_states, layer, lay, alpha):
    (w_hd, w_lr, w_tl, w_a2, b_a, gng, conv_w, w_pa, w_pb, w_o, ln_g, ln_b) = params
    bsz, seq, d = x.shape
    depth = w_hd.shape[0]
    tile = PROMPT_TILE
    assert seq % (2 * tile) == 0 and tile % GLA_CHUNK == 0
    ck = conv_w.shape[1]
    n_t = seq // tile
    n_tiles = bsz * n_t
    steps_per_seq = n_t // 2
    n_chunk_heads = (tile // GLA_CHUNK) * lay.h
    assert tile // GLA_CHUNK <= SUBLANES

    def const(arr, single_buffer=True):
        return _layer_spec(arr.shape[1:], layer, single_buffer)

    def next_tile(j):
        g = jnp.minimum(2 * j + 2, n_tiles - 1)
        return (g // n_t, g % n_t, 0)

    in_specs = [
        pl.BlockSpec((1, 2 * tile, d), lambda j: (j // steps_per_seq, j % steps_per_seq, 0)),
        pl.BlockSpec((1, tile, d), next_tile),
        const(mod_p, False),
    ] + [const(a) for a in params]
    args = [x, x, mod_p, *params]
    aliases = {}
    if prev_states is not None:
        for i, prev in enumerate(prev_states):
            in_specs.append(pl.BlockSpec(memory_space=pl.ANY))
            aliases[len(args)] = 1 + i
            args.append(prev)

    kern = functools.partial(_prompt_kernel, lay=lay, tile=tile, alpha=alpha, n_t=n_t,
                             n_tiles=n_tiles)
    return pl.pallas_call(
        kern,
        grid=(n_tiles // 2,),
        in_specs=in_specs,
        out_specs=[
            pl.BlockSpec((1, 2 * tile, d), lambda j: (j // steps_per_seq, j % steps_per_seq, 0)),
            pl.BlockSpec((None, 1, lay.h, lay.dk, lay.dv),
                         lambda j: (layer, j // steps_per_seq, 0, 0, 0)),
            pl.BlockSpec((None, 1, ck - 1, lay.cw), lambda j: (layer, j // steps_per_seq, 0, 0)),
        ],
        out_shape=[
            jax.ShapeDtypeStruct((bsz, seq, d), F32),
            jax.ShapeDtypeStruct((depth, bsz, lay.h, lay.dk, lay.dv), F32),
            jax.ShapeDtypeStruct((depth, bsz, ck - 1, lay.cw), F32),
        ],
        scratch_shapes=[
            pltpu.VMEM((tile, d), BF16),
            pltpu.VMEM((tile, d), BF16),
            pltpu.VMEM((tile, lay.n_pad), F32),
            pltpu.VMEM((tile, lay.n_pad), F32),
            pltpu.VMEM((tile, lay.qk), F32),
            pltpu.VMEM((tile + SUBLANES, lay.cw), F32),
            pltpu.VMEM((lay.h, lay.dk, lay.dv), F32),
            pltpu.VMEM((tile, lay.vt), BF16),
            pltpu.VMEM((tile, lay.cw), BF16),
            pltpu.VMEM((tile, d), BF16),
            pltpu.VMEM((tile, lay.qk), BF16),
            pltpu.VMEM((tile, lay.qk), BF16),
            pltpu.VMEM((tile, lay.qk), BF16),
            pltpu.VMEM((tile, lay.vt), BF16),
            pltpu.VMEM((SUBLANES, lay.qk), F32),
            pltpu.VMEM((lay.h, tile, GLA_CHUNK), BF16),
            pltpu.VMEM((n_chunk_heads, lay.dk, lay.dv), F32),
            pltpu.VMEM((n_chunk_heads, lay.dk, lay.dv), BF16),
        ],
        input_output_aliases=aliases,
        compiler_params=pltpu.CompilerParams(
            dimension_semantics=("arbitrary",),
            vmem_limit_bytes=_prompt_vmem_bytes(lay, tile)),
        name="prompt_layer",
    )(*args)


def _sample_proj_kernel(x_ref, mod_ref, w_hd_ref, w_lr_ref, w_tl_ref, w_a2_ref, b_a_ref,
                        z_ref, a_ref, *, lay):
    d = lay.d
    shift = mod_ref[:, 0:d]
    scale = mod_ref[:, d:2 * d]
    u = (x_ref[...] * (1.0 + scale) + shift).astype(BF16)
    w_in_refs = (w_hd_ref, w_lr_ref, w_tl_ref)
    for piece, s0, s1, d0 in lay.groups:
        z_ref[:, d0:d0 + s1 - s0] = jnp.dot(u, _unpack_rows(w_in_refs[piece][:, s0:s1]),
                                            preferred_element_type=F32)
    a_lr = z_ref[:, lay.off_a:lay.off_a + lay.rank_pad].astype(BF16)
    la = jnp.dot(a_lr, _unpack_rows(w_a2_ref[...]), preferred_element_type=F32) + b_a_ref[...]
    a_ref[...] = jnp.exp(_log_sigmoid(la) * (1.0 / GLA_TAU))


def _sample_proj(x3, mod_s, w_hd, w_lr, w_tl, w_a2, b_a, layer, lay):
    n, _, d = x3.shape
    kern = functools.partial(_sample_proj_kernel, lay=lay)
    weights = (w_hd, w_lr, w_tl, w_a2, b_a)
    vmem = 2 * d * lay.n_pad + 2 * 4 * n * lay.n_pad + 16 * MIB
    return pl.pallas_call(
        kern,
        grid=(1,),
        in_specs=[pl.BlockSpec((n, None, d), lambda i: (0, 0, 0)),
                  _layer_spec(mod_s.shape[1:], layer)]
        + [_layer_spec(w.shape[1:], layer, True) for w in weights],
        out_specs=[pl.BlockSpec((n, lay.n_pad), lambda i: (0, 0)),
                   pl.BlockSpec((n, lay.qk), lambda i: (0, 0))],
        out_shape=[jax.ShapeDtypeStruct((n, lay.n_pad), F32),
                   jax.ShapeDtypeStruct((n, lay.qk), F32)],
        compiler_params=pltpu.CompilerParams(dimension_semantics=("arbitrary",),
                                             vmem_limit_bytes=vmem),
        name="sample_proj",
    )(x3, mod_s, *weights)


def _sample_state_kernel(a_ref, qkv_ref, s_ref, *rest, lay, bb):
    s_out_ref, o_ref = rest[-2:]
    h_n, dk, dv = lay.h, lay.dk, lay.dv
    q_scale = dk ** -0.5
    for i in range(bb):
        rows = [a_ref[i:i + 1, h * dk:(h + 1) * dk] for h in range(h_n)]
        rows += [qkv_ref[i:i + 1, lay.off_k + h * dk:lay.off_k + (h + 1) * dk]
                 for h in range(h_n)]
        rows += [qkv_ref[i:i + 1, lay.off_q + h * dk:lay.off_q + (h + 1) * dk]
                 for h in range(h_n)]
        rows.append(jnp.zeros((dk - len(rows), dk), F32))
        mt = jnp.transpose(jnp.concatenate(rows, axis=0))
        for h in range(h_n):
            a_col = mt[:, h:h + 1]
            k_col = mt[:, h_n + h:h_n + h + 1]
            q_col = mt[:, 2 * h_n + h:2 * h_n + h + 1]
            v = qkv_ref[i:i + 1, lay.off_v + h * dv:lay.off_v + (h + 1) * dv]
            s_new = a_col * s_ref[i, h] + k_col * v
            s_out_ref[i, h] = s_new
            o_ref[i:i + 1, h * dv:(h + 1) * dv] = q_scale * jnp.sum(q_col * s_new, axis=0,
                                                                   keepdims=True)


def _sample_state(a, z, state_all, prev_out, layer, lay):
    depth, n = state_all.shape[:2]
    bb = SAMPLE_STATE_ROWS
    assert n % bb == 0
    kern = functools.partial(_sample_state_kernel, lay=lay, bb=bb)
    state_spec = pl.BlockSpec((None, bb, lay.h, lay.dk, lay.dv), lambda i: (layer, i, 0, 0, 0))
    in_specs = [
        pl.BlockSpec((bb, lay.qk), lambda i: (i, 0)),
        pl.BlockSpec((bb, lay.off_gg), lambda i: (i, 0)),
        state_spec,
    ]
    args = [a, z, state_all]
    aliases = {}
    if prev_out is not None:
        in_specs.append(pl.BlockSpec(memory_space=pl.ANY))
        aliases = {len(args): 0}
        args.append(prev_out)
    block_bytes = bb * lay.h * lay.dk * lay.dv * 4
    return pl.pallas_call(
        kern,
        grid=(n // bb,),
        in_specs=in_specs,
        out_specs=[state_spec, pl.BlockSpec((bb, lay.vt), lambda i: (i, 0))],
        out_shape=[jax.ShapeDtypeStruct(state_all.shape, F32),
                   jax.ShapeDtypeStruct((n, lay.vt), F32)],
        input_output_aliases=aliases,
        compiler_params=pltpu.CompilerParams(dimension_semantics=("arbitrary",),
                                             vmem_limit_bytes=4 * block_bytes + 16 * MIB),
        name="sample_state",
    )(*args)


def _sample_out_kernel(o_ref, z_ref, x_ref, mod_ref, buf_ref, gng_ref, cw_ref,
                       w_pa_ref, w_pb_ref, w_o_ref, lng_ref, lnb_ref,
                       y_ref, buf_out_ref, og_scr, *, lay, alpha):
    d, h_n, dv, cw = lay.d, lay.h, lay.dv, lay.cw
    ck = cw_ref.shape[0]
    gate = mod_ref[:, 2 * d:3 * d]
    gng = gng_ref[...]
    for h in range(h_n):
        o = o_ref[:, h * dv:(h + 1) * dv]
        ms = jnp.mean(o * o, axis=-1, keepdims=True)
        on = o * lax.rsqrt(ms + EPS) * gng
        gg = z_ref[:, lay.off_gg + h * dv:lay.off_gg + (h + 1) * dv]
        og_scr[:, h * dv:(h + 1) * dv] = (on * _silu(gg)).astype(BF16)
    pa = jnp.dot(og_scr[...], _unpack_rows(w_pa_ref[...]), preferred_element_type=F32)

    p = z_ref[:, lay.off_cc:lay.off_cc + cw] * z_ref[:, lay.off_ch:lay.off_ch + cw]
    yc = cw_ref[ck - 1:ck, :] * p
    for i in range(ck - 1):
        yc = yc + cw_ref[i:i + 1, :] * buf_ref[:, i * cw:(i + 1) * cw]
    yc = z_ref[:, lay.off_cb:lay.off_cb + cw] * yc * _silu(z_ref[:, lay.off_gc:lay.off_gc + cw])
    pb = jnp.dot(yc.astype(BF16), _unpack_rows(w_pb_ref[...]), preferred_element_type=F32)
    for i in range(ck - 2):
        buf_out_ref[:, i * cw:(i + 1) * cw] = buf_ref[:, (i + 1) * cw:(i + 2) * cw]
    buf_out_ref[:, (ck - 2) * cw:(ck - 1) * cw] = p

    merged = (jax.nn.sigmoid(z_ref[:, lay.off_mg:lay.off_mg + d]) * pa
              + jax.nn.sigmoid(z_ref[:, lay.off_mc:lay.off_mc + d]) * pb)
    out = jnp.dot(merged.astype(BF16), _unpack_rows(w_o_ref[...]),
                  preferred_element_type=F32)
    hres = alpha * x_ref[...] + gate * out
    mu = jnp.mean(hres, axis=-1, keepdims=True)
    dev = hres - mu
    var = jnp.mean(dev * dev, axis=-1, keepdims=True)
    y_ref[...] = dev * lax.rsqrt(var + EPS) * lng_ref[...] + lnb_ref[...]


def _sample_out(o, z, x3, mod_s, buf, gng, conv_w, w_pa, w_pb, w_o, ln_g, ln_b, layer, lay,
                alpha):
    n, _, d = x3.shape
    kern = functools.partial(_sample_out_kernel, lay=lay, alpha=alpha)
    stacked = (gng, conv_w, w_pa, w_pb, w_o, ln_g, ln_b)

    def whole(arr):
        return pl.BlockSpec(arr.shape, lambda i: (0,) * arr.ndim)

    return pl.pallas_call(
        kern,
        grid=(1,),
        in_specs=[whole(o), whole(z), pl.BlockSpec((n, None, d), lambda i: (0, 0, 0)),
                  _layer_spec(mod_s.shape[1:], layer), whole(buf)]
        + [_layer_spec(w.shape[1:], layer) for w in stacked],
        out_specs=[pl.BlockSpec((n, None, d), lambda i: (0, 0, 0)),
                   pl.BlockSpec(buf.shape, lambda i: (0, 0))],
        out_shape=[jax.ShapeDtypeStruct((n, 1, d), F32),
                   jax.ShapeDtypeStruct(buf.shape, F32)],
        scratch_shapes=[pltpu.VMEM((n, lay.vt), BF16)],
        compiler_params=pltpu.CompilerParams(dimension_semantics=("arbitrary",),
                                             vmem_limit_bytes=48 * MIB),
        name="sample_out",
    )(o, z, x3, mod_s, buf, *stacked)


def kernel(x_prompt, x_sample, c_prompt, c_sample, state_gla, state_conv, w_ada, b_ada, w_in,
           w_a2, b_a, gla_norm_g, conv_w, w_pa, w_pb, w_o, ln_g, ln_b):
    depth, n_s, heads, dk, dv = state_gla.shape
    d = x_prompt.shape[2]
    rank = w_a2.shape[1]
    ck, cw = conv_w.shape[1:]
    lay = _Layout(d, heads, dk, dv, rank, cw)
    alpha = (2 * depth) ** 0.25

    mod_p, mod_s = _ada_mod(c_prompt, c_sample, w_ada, b_ada)

    w_hd, w_lr, w_tl = _split_w_in(w_in, lay)
    w_a2_p = _cast_bf16(jnp.concatenate(
        [w_a2, jnp.zeros((depth, lay.rank_pad - rank, lay.qk), w_a2.dtype)], axis=1))
    w_pa_b, w_pb_b, w_o_b = _cast_bf16(w_pa), _cast_bf16(w_pb), _cast_bf16(w_o)
    b_a3 = b_a.reshape(depth, 1, -1)
    gng3 = gla_norm_g.reshape(depth, 1, -1)
    lng3 = ln_g.reshape(depth, 1, -1)
    lnb3 = ln_b.reshape(depth, 1, -1)
    params = (w_hd, w_lr, w_tl, w_a2_p, b_a3, gng3, conv_w, w_pa_b, w_pb_b, w_o_b, lng3, lnb3)

    hp, hs = x_prompt, x_sample
    prompt_states = None
    gla_s = None
    conv_s = []
    for l in range(depth):
        hp, *prompt_states = _prompt_layer(hp, mod_p, params, prompt_states, l, lay, alpha)
        z, a = _sample_proj(hs, mod_s, w_hd, w_lr, w_tl, w_a2_p, b_a3, l, lay)
        gla_s, o = _sample_state(a, z, state_gla, gla_s, l, lay)
        hs, buf_new = _sample_out(o, z, hs, mod_s, state_conv[l].reshape(n_s, (ck - 1) * cw),
                                  gng3, conv_w, w_pa_b, w_pb_b, w_o_b, lng3, lnb3, l, lay, alpha)
        conv_s.append(buf_new.reshape(n_s, ck - 1, cw))
    return (hp, hs, prompt_states[0], prompt_states[1], gla_s, jnp.stack(conv_s))
```

```python
import functools

import jax
import jax.numpy as jnp
from jax import lax
from jax.experimental import pallas as pl
from jax.experimental.pallas import tpu as pltpu

F32 = jnp.float32
BF16 = jnp.bfloat16

GLA_TAU = 16.0
GLA_CHUNK = 64
EPS = 1e-5

LANES = 128
SUBLANES = 8
VMEM_PHYSICAL_BYTES = 64 * 1024 * 1024
MIB = 1024 * 1024

PROMPT_TILE = 256
PROJ_COLS = 256
PREP_COLS = 512


LOG2_E = 1.4426950408889634


def _log_sigmoid(x):
    return jnp.minimum(x, 0.0) - jnp.log(1.0 + jnp.exp(-jnp.abs(x)))


def _cumsum_rows(x):
    row = lax.broadcasted_iota(jnp.int32, (SUBLANES, x.shape[1]), 0)
    groups = []
    carry = None
    for g in range(x.shape[0] // SUBLANES):
        t = x[g * SUBLANES:(g + 1) * SUBLANES, :]
        s = 1
        while s < SUBLANES:
            t = t + jnp.where(row >= s, pltpu.roll(t, s, 0), 0.0)
            s *= 2
        if carry is not None:
            t = t + carry
        carry = jnp.broadcast_to(t[SUBLANES - 1:SUBLANES, :], t.shape)
        groups.append(t)
    return jnp.concatenate(groups, axis=0)


def _silu(x):
    return x * jax.nn.sigmoid(x)


PACKED = jnp.uint32
PACK = 2


def _pack_rows(w_bf16):
    return pltpu.bitcast(w_bf16, PACKED)


def _unpack_rows(w_packed):
    return pltpu.bitcast(w_packed, BF16)


class _Layout:
    def __init__(self, d_model, heads, dk, dv, rank, conv_w):
        self.d = d_model
        self.h = heads
        self.dk = dk
        self.dv = dv
        self.rank = rank
        self.cw = conv_w
        self.qk = heads * dk
        self.vt = heads * dv
        self.rank_pad = -(-rank // LANES) * LANES
        self.off_q = 0
        self.off_k = self.off_q + self.qk
        self.off_v = self.off_k + self.qk
        self.off_gg = self.off_v + self.vt
        self.off_a = self.off_gg + self.vt
        self.off_cb = self.off_a + self.rank_pad
        self.off_cc = self.off_cb + conv_w
        self.off_ch = self.off_cc + conv_w
        self.off_gc = self.off_ch + conv_w
        self.off_mg = self.off_gc + conv_w
        self.off_mc = self.off_mg + d_model
        self.n_pad = self.off_mc + d_model
        self.n_head = self.off_a
        self.n_tail = self.n_pad - self.off_cb
        self.groups = (
            (0, self.off_q, self.off_v, self.off_q),
            (0, self.off_v, self.off_gg, self.off_v),
            (0, self.off_gg, self.off_a, self.off_gg),
            (1, 0, self.rank_pad, self.off_a),
            (2, 0, 2 * conv_w, self.off_cb),
            (2, 2 * conv_w, 4 * conv_w, self.off_ch),
            (2, 4 * conv_w, 4 * conv_w + d_model, self.off_mg),
            (2, 4 * conv_w + d_model, self.n_tail, self.off_mc))


def _layer_spec(shape, layer, single_buffer=False):
    zeros = (0,) * len(shape)
    kwargs = {"pipeline_mode": pl.Buffered(1)} if single_buffer else {}
    return pl.BlockSpec((None,) + tuple(shape), lambda *_: (layer,) + zeros, **kwargs)


def _transpose_cast_kernel(w_ref, o_ref, *, valid):
    wt = jnp.transpose(w_ref[0])
    if valid < wt.shape[1]:
        lane = lax.broadcasted_iota(jnp.int32, wt.shape, 1)
        wt = jnp.where(lane < valid, wt, 0.0)
    o_ref[...] = _pack_rows(wt.astype(BF16))


def _transpose_cast(w_t, col0, n_valid, n_out, cols):
    depth, n_in, d = w_t.shape
    assert n_out % cols == 0 and col0 % SUBLANES == 0 and col0 + n_out <= n_in
    kern = functools.partial(_transpose_cast_kernel, valid=min(n_valid, cols))
    return pl.pallas_call(
        kern,
        grid=(depth, n_out // cols),
        in_specs=[pl.BlockSpec((pl.Element(1), pl.Element(cols), pl.Element(d)),
                               lambda l, j: (l, pl.multiple_of(col0 + j * cols, SUBLANES), 0))],
        out_specs=pl.BlockSpec((None, d // PACK, cols), lambda l, j: (l, 0, j)),
        out_shape=jax.ShapeDtypeStruct((depth, d // PACK, n_out), PACKED),
        compiler_params=pltpu.CompilerParams(
            dimension_semantics=("arbitrary", "arbitrary")),
        name="transpose_cast",
    )(w_t)


def _split_w_in(w_in, lay):
    n_in = w_in.shape[2]
    a_end = lay.off_a + lay.rank
    assert n_in == a_end + lay.n_tail and lay.rank <= lay.rank_pad == LANES
    w_t = jnp.swapaxes(w_in, 1, 2)
    head = _transpose_cast(w_t, 0, lay.n_head, lay.n_head, PREP_COLS)
    low = _transpose_cast(w_t, lay.off_a, lay.rank, lay.rank_pad, lay.rank_pad)
    tail = _transpose_cast(w_t, a_end, lay.n_tail, lay.n_tail, PREP_COLS)
    return head, low, tail


def _cast_kernel(w_ref, o_ref):
    o_ref[...] = _pack_rows(w_ref[...].astype(BF16))


def _cast_bf16(w):
    depth, rows, cols = w.shape
    return pl.pallas_call(
        _cast_kernel,
        grid=(depth,),
        in_specs=[pl.BlockSpec((None, rows, cols), lambda l: (l, 0, 0))],
        out_specs=pl.BlockSpec((None, rows // PACK, cols), lambda l: (l, 0, 0)),
        out_shape=jax.ShapeDtypeStruct((depth, rows // PACK, cols), PACKED),
        compiler_params=pltpu.CompilerParams(
            dimension_semantics=("arbitrary",),
            vmem_limit_bytes=2 * rows * cols * 6 + 16 * MIB),
        name="cast_bf16",
    )(w)


def _ada_kernel(cp_ref, cs_ref, w_ref, b_ref, op_ref, os_ref):
    w = w_ref[...].astype(BF16)
    b = b_ref[...]
    op_ref[...] = jnp.dot(cp_ref[...].astype(BF16), w, preferred_element_type=F32) + b
    os_ref[...] = jnp.dot(cs_ref[...].astype(BF16), w, preferred_element_type=F32) + b


def _ada_mod(c_prompt, c_sample, w_ada, b_ada):
    depth, d, n3 = w_ada.shape
    bp, ns = c_prompt.shape[0], c_sample.shape[0]
    tn = d
    return pl.pallas_call(
        _ada_kernel,
        grid=(depth, n3 // tn),
        in_specs=[
            pl.BlockSpec((bp, d), lambda l, j: (0, 0)),
            pl.BlockSpec((ns, d), lambda l, j: (0, 0)),
            pl.BlockSpec((None, d, tn), lambda l, j: (l, 0, j)),
            pl.BlockSpec((None, 1, tn), lambda l, j: (l, 0, j)),
        ],
        out_specs=[pl.BlockSpec((None, bp, tn), lambda l, j: (l, 0, j)),
                   pl.BlockSpec((None, ns, tn), lambda l, j: (l, 0, j))],
        out_shape=[jax.ShapeDtypeStruct((depth, bp, n3), F32),
                   jax.ShapeDtypeStruct((depth, ns, n3), F32)],
        compiler_params=pltpu.CompilerParams(
            dimension_semantics=("arbitrary", "arbitrary")),
        name="ada_mod",
    )(c_prompt, c_sample, w_ada, b_ada.reshape(depth, 1, n3))


def _project_plan(lay):
    plan = []
    for piece, s0, s1, d0 in lay.groups:
        for c0 in range(s0, s1, PROJ_COLS):
            plan.append((piece, c0, min(c0 + PROJ_COLS, s1), d0 + c0 - s0))
    return plan


def _project_stages(x_rows, mod_ref, brow, w_in_refs, u_scr, z_dst, lay):
    d = lay.d
    shift = mod_ref[pl.ds(brow, 1), 0:d]
    scale = mod_ref[pl.ds(brow, 1), d:2 * d]
    u_scr[...] = (x_rows() * (1.0 + scale) + shift).astype(BF16)
    yield
    for piece, c0, c1, d0 in _project_plan(lay):
        z_dst[:, d0:d0 + c1 - c0] = jnp.dot(u_scr[...],
                                            _unpack_rows(w_in_refs[piece][:, c0:c1]),
                                            preferred_element_type=F32)
        yield


def _mix_stages(z_scr, x_ref, y_ref, row0, gate, w_a2_ref, b_a_ref, gng_ref, cw_ref,
                w_pa_ref, w_pb_ref, w_o_ref, lng_ref, lnb_ref,
                la_scr, p_scr, s_scr, og_scr, yc_scr, mg_scr,
                qin_scr, kin_scr, kst_scr, vb_scr, dec_scr, att_scr, ds_scr, sbf_scr,
                lay, tile, alpha):
    d, h_n, dk, dv, cw = lay.d, lay.h, lay.dk, lay.dv, lay.cw
    ck = cw_ref.shape[0]
    halo = SUBLANES
    chunks = [(c, c * GLA_CHUNK, (c + 1) * GLA_CHUNK) for c in range(tile // GLA_CHUNK)]
    heads = [(h, slice(h * dk, (h + 1) * dk), slice(h * dv, (h + 1) * dv)) for h in range(h_n)]

    a_lr = z_scr[:, lay.off_a:lay.off_a + lay.rank_pad].astype(BF16)
    la = jnp.dot(a_lr, _unpack_rows(w_a2_ref[...]), preferred_element_type=F32) + b_a_ref[...]
    la_scr[...] = _log_sigmoid(la) * (1.0 / GLA_TAU)
    yield 4

    causal = (lax.broadcasted_iota(jnp.int32, (GLA_CHUNK, GLA_CHUNK), 0)
              >= lax.broadcasted_iota(jnp.int32, (GLA_CHUNK, GLA_CHUNK), 1))
    q_scale = dk ** -0.5
    gng = gng_ref[...]

    for c, r0, r1 in chunks:
        for h, ks, vs in heads:
            b2 = _cumsum_rows(la_scr[r0:r1, ks]) * LOG2_E
            bl2 = b2[GLA_CHUNK - 1:GLA_CHUNK, :]
            q_in = z_scr[r0:r1, lay.off_q + h * dk:lay.off_q + (h + 1) * dk] * q_scale
            k = z_scr[r0:r1, lay.off_k + h * dk:lay.off_k + (h + 1) * dk]
            qin_scr[r0:r1, ks] = (q_in * jnp.exp2(b2)).astype(BF16)
            kin_scr[r0:r1, ks] = (k * jnp.exp2(-b2)).astype(BF16)
            kst_scr[r0:r1, ks] = (k * jnp.exp2(bl2 - b2)).astype(BF16)
            dec_scr[c:c + 1, ks] = jnp.exp2(bl2)
            vb_scr[r0:r1, vs] = z_scr[r0:r1, lay.off_v + h * dv:lay.off_v + (h + 1) * dv] \
                .astype(BF16)
            yield 1

    for c, r0, r1 in chunks:
        for h, ks, vs in heads:
            att = lax.dot_general(qin_scr[r0:r1, ks], kin_scr[r0:r1, ks],
                                  (((1,), (1,)), ((), ())), preferred_element_type=F32)
            att_scr[h, r0:r1, :] = jnp.where(causal, att, 0.0).astype(BF16)
        yield 1

    for h, ks, vs in heads:
        for c, r0, r1 in chunks:
            ds_scr[c * h_n + h] = lax.dot_general(kst_scr[r0:r1, ks], vb_scr[r0:r1, vs],
                                                  (((0,), (0,)), ((), ())),
                                                  preferred_element_type=F32)
        yield 1
    for h, ks, vs in heads:
        state = s_scr[h]
        for c, r0, r1 in chunks:
            sbf_scr[c * h_n + h] = state.astype(BF16)
            dec = jnp.transpose(jnp.broadcast_to(dec_scr[c:c + 1, ks], (dk, dk)))
            dec = jnp.concatenate([dec] * (dv // dk), axis=1)
            state = dec * state + ds_scr[c * h_n + h]
        s_scr[h] = state
        yield 2

    for c, r0, r1 in chunks:
        for h, ks, vs in heads:
            lhs = jnp.concatenate([qin_scr[r0:r1, ks], att_scr[h, r0:r1, :]], axis=1)
            rhs = jnp.concatenate([sbf_scr[c * h_n + h], vb_scr[r0:r1, vs]], axis=0)
            z_scr[r0:r1, vs] = jnp.dot(lhs, rhs, preferred_element_type=F32)
        yield 1

    for c, r0, r1 in chunks:
        for h, ks, vs in heads:
            o = z_scr[r0:r1, vs]
            ms = jnp.mean(o * o, axis=-1, keepdims=True)
            on = o * lax.rsqrt(ms + EPS) * gng
            gg = z_scr[r0:r1, lay.off_gg + h * dv:lay.off_gg + (h + 1) * dv]
            og_scr[r0:r1, vs] = (on * _silu(gg)).astype(BF16)
        yield 3

    for c, r0, r1 in chunks:
        p = (z_scr[r0:r1, lay.off_cc:lay.off_cc + cw]
             * z_scr[r0:r1, lay.off_ch:lay.off_ch + cw])
        p_scr[halo + r0:halo + r1, :] = p
        yc = cw_ref[ck - 1:ck, :] * p
        for i in range(ck - 1):
            lag = ck - 1 - i
            yc = yc + cw_ref[i:i + 1, :] * p_scr[halo + r0 - lag:halo + r1 - lag, :]
        yc = (z_scr[r0:r1, lay.off_cb:lay.off_cb + cw] * yc
              * _silu(z_scr[r0:r1, lay.off_gc:lay.off_gc + cw]))
        yc_scr[r0:r1, :] = yc.astype(BF16)
        yield 2

    p_scr[0:halo, :] = p_scr[tile:tile + halo, :]

    z_scr[:, 0:d] = jnp.dot(og_scr[...], _unpack_rows(w_pa_ref[...]), preferred_element_type=F32)
    yield 0
    z_scr[:, d:2 * d] = jnp.dot(yc_scr[...], _unpack_rows(w_pb_ref[...]),
                                preferred_element_type=F32)
    yield 0
    for c, r0, r1 in chunks:
        merged = (jax.nn.sigmoid(z_scr[r0:r1, lay.off_mg:lay.off_mg + d]) * z_scr[r0:r1, 0:d]
                  + jax.nn.sigmoid(z_scr[r0:r1, lay.off_mc:lay.off_mc + d]) * z_scr[r0:r1, d:2 * d])
        mg_scr[r0:r1, :] = merged.astype(BF16)
        yield 2
    z_scr[:, 2 * d:3 * d] = jnp.dot(mg_scr[...], _unpack_rows(w_o_ref[...]),
                                    preferred_element_type=F32)
    yield 0

    lng = lng_ref[...]
    lnb = lnb_ref[...]
    for c, r0, r1 in chunks:
        hres = alpha * x_ref[0, row0 + r0:row0 + r1, :] + gate * z_scr[r0:r1, 2 * d:3 * d]
        mu = jnp.mean(hres, axis=-1, keepdims=True)
        dev = hres - mu
        var = jnp.mean(dev * dev, axis=-1, keepdims=True)
        y_ref[0, row0 + r0:row0 + r1, :] = dev * lax.rsqrt(var + EPS) * lng + lnb
        yield 3


def _mix_weights(lay, tile):
    chunks = tile // GLA_CHUNK
    return ([4] + [1] * (chunks * lay.h) + [1] * chunks + [1] * lay.h + [2] * lay.h + [1] * chunks
            + [3] * chunks + [2] * chunks + [0, 0] + [2] * chunks + [0] + [3] * chunks)


def _emit_overlapped(mix, mix_weights, proj, n_proj, fillers):
    total = sum(mix_weights)
    matmul_slots = mix_weights.count(0)
    done_w = done_p = done_f = slot = 0
    for weight in mix_weights:
        done_w += weight
        while done_p * total < done_w * n_proj:
            next(proj)
            done_p += 1
        if weight == 0:
            slot += 1
            while done_f * matmul_slots < slot * len(fillers):
                fillers[done_f]()
                done_f += 1
        assert next(mix) == weight
    assert done_p == n_proj and done_f == len(fillers) and next(mix, None) is None


def _sample_state_step(i, a_ref, qkv_ref, s_ref, s_out_ref, o_ref, lay):
    h_n, dk, dv = lay.h, lay.dk, lay.dv
    q_scale = dk ** -0.5
    r = slice(i, i + 1)
    rows = [a_ref[r, h * dk:(h + 1) * dk] for h in range(h_n)]
    rows += [qkv_ref[r, lay.off_k + h * dk:lay.off_k + (h + 1) * dk] for h in range(h_n)]
    rows += [qkv_ref[r, lay.off_q + h * dk:lay.off_q + (h + 1) * dk] for h in range(h_n)]
    rows.append(jnp.zeros((dk - len(rows), dk), F32))
    mt = jnp.transpose(jnp.concatenate(rows, axis=0))
    for h in range(h_n):
        a_col = mt[:, h:h + 1]
        k_col = mt[:, h_n + h:h_n + h + 1]
        q_col = mt[:, 2 * h_n + h:2 * h_n + h + 1]
        v = qkv_ref[r, lay.off_v + h * dv:lay.off_v + (h + 1) * dv]
        s_new = a_col * s_ref[i, h] + k_col * v
        s_out_ref[i, h] = s_new
        o_ref[i:i + 1, h * dv:(h + 1) * dv] = q_scale * jnp.sum(q_col * s_new, axis=0,
                                                               keepdims=True)


def _prompt_kernel(xc_ref, xn_ref, mod_ref, sa_ref, sqkv_ref, sst_ref,
                   w_hd_ref, w_lr_ref, w_tl_ref, w_a2_ref, b_a_ref,
                   gng_ref, cw_ref, w_pa_ref, w_pb_ref, w_o_ref, lng_ref, lnb_ref, *rest,
                   lay, tile, alpha, n_t, n_tiles, sample_rows):
    (y_ref, sg_ref, sc_ref, sso_ref, so_ref,
     ua_scr, ub_scr, za_scr, zb_scr, *mix_scr) = rest[-23:]
    p_scr, s_scr = mix_scr[1:3]
    j = pl.program_id(0)
    steps_per_seq = n_t // 2
    jj = lax.rem(j, steps_per_seq)
    b_cur = lax.div(j, steps_per_seq)
    b_next = lax.div(jnp.minimum(2 * j + 2, n_tiles - 1), n_t)
    d = lay.d
    ck = cw_ref.shape[0]
    halo = SUBLANES
    w_in_refs = (w_hd_ref, w_lr_ref, w_tl_ref)

    @pl.when(jj == 0)
    def _():
        s_scr[...] = jnp.zeros_like(s_scr)
        p_scr[0:halo, :] = jnp.zeros((halo, lay.cw), F32)

    def project(x_rows, brow, u_scr, z_dst):
        return _project_stages(x_rows, mod_ref, brow, w_in_refs, u_scr, z_dst, lay)

    @pl.when(j == 0)
    def _():
        for _ in project(lambda: xc_ref[0, 0:tile, :], b_cur, ua_scr, za_scr):
            pass

    gate = mod_ref[pl.ds(b_cur, 1), 2 * d:3 * d]
    mix_args = (w_a2_ref, b_a_ref, gng_ref, cw_ref, w_pa_ref, w_pb_ref, w_o_ref, lng_ref,
                lnb_ref, *mix_scr, lay, tile, alpha)
    mix_weight = _mix_weights(lay, tile)
    n_proj = 1 + len(_project_plan(lay))

    fillers = [functools.partial(_sample_state_step, i, sa_ref, sqkv_ref, sst_ref,
                                 sso_ref, so_ref, lay) for i in range(sample_rows)]
    half = sample_rows // 2

    _emit_overlapped(_mix_stages(za_scr, xc_ref, y_ref, 0, gate, *mix_args), mix_weight,
                     project(lambda: xc_ref[0, tile:2 * tile, :], b_cur, ub_scr, zb_scr), n_proj,
                     fillers[:half])
    _emit_overlapped(_mix_stages(zb_scr, xc_ref, y_ref, tile, gate, *mix_args), mix_weight,
                     project(lambda: xn_ref[0], b_next, ua_scr, za_scr), n_proj,
                     fillers[half:])

    @pl.when(jj == steps_per_seq - 1)
    def _():
        sg_ref[0] = s_scr[...]
        sc_ref[0] = p_scr[halo - (ck - 1):halo, :]


def _prompt_vmem_bytes(lay, tile, sample_rows):
    weights = 2 * (lay.d * lay.n_pad + lay.rank_pad * lay.qk + lay.vt * lay.d
                   + lay.cw * lay.d + lay.d * lay.d)
    state = lay.h * lay.dk * lay.dv * 4
    io = (2 * (2 * tile + tile + 2 * tile) * lay.d * 4 + 2 * state
          + 2 * 2 * sample_rows * state + 2 * SUBLANES * (lay.qk + lay.off_gg + lay.vt) * 4)
    chunk_heads = (tile // GLA_CHUNK) * lay.h
    scratch = (2 * tile * lay.d * 2 + 2 * tile * lay.n_pad * 4 + tile * lay.qk * 4
               + (tile + SUBLANES) * lay.cw * 4 + lay.h * lay.dk * lay.dv * 4
               + tile * lay.vt * 2 + tile * lay.cw * 2 + tile * lay.d * 2
               + 3 * tile * lay.qk * 2 + tile * lay.vt * 2 + SUBLANES * lay.qk * 4
               + lay.h * tile * LANES * 2 + chunk_heads * lay.dk * lay.dv * 6)
    temporaries = tile * lay.n_pad * 2
    return min(weights + io + scratch + temporaries, VMEM_PHYSICAL_BYTES * 15 // 16)


def _prompt_layer(x, mod_p, sample_a, sample_z, sample_state, params, prev_states, layer, lay,
                  alpha):
    (w_hd, w_lr, w_tl, w_a2, b_a, gng, conv_w, w_pa, w_pb, w_o, ln_g, ln_b) = params
    bsz, seq, d = x.shape
    depth = w_hd.shape[0]
    tile = PROMPT_TILE
    assert seq % (2 * tile) == 0 and tile % GLA_CHUNK == 0
    ck = conv_w.shape[1]
    n_t = seq // tile
    n_tiles = bsz * n_t
    n_steps = n_tiles // 2
    steps_per_seq = n_t // 2
    n_chunk_heads = (tile // GLA_CHUNK) * lay.h
    assert tile // GLA_CHUNK <= SUBLANES
    n_s = sample_state.shape[1]
    sample_rows = n_s // n_steps
    assert sample_rows * n_steps == n_s and sample_rows % 2 == 0
    sample_a = sample_a.reshape(n_steps, sample_rows, lay.qk)
    sample_qkv = sample_z[:, :lay.off_gg].reshape(n_steps, sample_rows, lay.off_gg)

    def const(arr, single_buffer=True):
        return _layer_spec(arr.shape[1:], layer, single_buffer)

    def next_tile(j):
        g = jnp.minimum(2 * j + 2, n_tiles - 1)
        return (g // n_t, g % n_t, 0)

    sample_state_spec = pl.BlockSpec((None, sample_rows, lay.h, lay.dk, lay.dv),
                                     lambda j: (layer, j, 0, 0, 0))
    in_specs = [
        pl.BlockSpec((1, 2 * tile, d), lambda j: (j // steps_per_seq, j % steps_per_seq, 0)),
        pl.BlockSpec((1, tile, d), next_tile),
        const(mod_p, False),
        pl.BlockSpec((None, sample_rows, lay.qk), lambda j: (j, 0, 0)),
        pl.BlockSpec((None, sample_rows, lay.off_gg), lambda j: (j, 0, 0)),
        sample_state_spec,
    ] + [const(a) for a in params]
    args = [x, x, mod_p, sample_a, sample_qkv, sample_state, *params]
    aliases = {}
    if prev_states is not None:
        for i, prev in enumerate(prev_states):
            in_specs.append(pl.BlockSpec(memory_space=pl.ANY))
            aliases[len(args)] = 1 + i
            args.append(prev)

    kern = functools.partial(_prompt_kernel, lay=lay, tile=tile, alpha=alpha, n_t=n_t,
                             n_tiles=n_tiles, sample_rows=sample_rows)
    return pl.pallas_call(
        kern,
        grid=(n_steps,),
        in_specs=in_specs,
        out_specs=[
            pl.BlockSpec((1, 2 * tile, d), lambda j: (j // steps_per_seq, j % steps_per_seq, 0)),
            pl.BlockSpec((None, 1, lay.h, lay.dk, lay.dv),
                         lambda j: (layer, j // steps_per_seq, 0, 0, 0)),
            pl.BlockSpec((None, 1, ck - 1, lay.cw), lambda j: (layer, j // steps_per_seq, 0, 0)),
            sample_state_spec,
            pl.BlockSpec((None, sample_rows, lay.vt), lambda j: (j, 0, 0)),
        ],
        out_shape=[
            jax.ShapeDtypeStruct((bsz, seq, d), F32),
            jax.ShapeDtypeStruct((depth, bsz, lay.h, lay.dk, lay.dv), F32),
            jax.ShapeDtypeStruct((depth, bsz, ck - 1, lay.cw), F32),
            jax.ShapeDtypeStruct(sample_state.shape, F32),
            jax.ShapeDtypeStruct((n_steps, sample_rows, lay.vt), F32),
        ],
        scratch_shapes=[
            pltpu.VMEM((tile, d), BF16),
            pltpu.VMEM((tile, d), BF16),
            pltpu.VMEM((tile, lay.n_pad), F32),
            pltpu.VMEM((tile, lay.n_pad), F32),
            pltpu.VMEM((tile, lay.qk), F32),
            pltpu.VMEM((tile + SUBLANES, lay.cw), F32),
            pltpu.VMEM((lay.h, lay.dk, lay.dv), F32),
            pltpu.VMEM((tile, lay.vt), BF16),
            pltpu.VMEM((tile, lay.cw), BF16),
            pltpu.VMEM((tile, d), BF16),
            pltpu.VMEM((tile, lay.qk), BF16),
            pltpu.VMEM((tile, lay.qk), BF16),
            pltpu.VMEM((tile, lay.qk), BF16),
            pltpu.VMEM((tile, lay.vt), BF16),
            pltpu.VMEM((SUBLANES, lay.qk), F32),
            pltpu.VMEM((lay.h, tile, GLA_CHUNK), BF16),
            pltpu.VMEM((n_chunk_heads, lay.dk, lay.dv), F32),
            pltpu.VMEM((n_chunk_heads, lay.dk, lay.dv), BF16),
        ],
        input_output_aliases=aliases,
        compiler_params=pltpu.CompilerParams(
            dimension_semantics=("arbitrary",),
            vmem_limit_bytes=_prompt_vmem_bytes(lay, tile, sample_rows)),
        name="prompt_layer",
    )(*args)


def _sample_proj_kernel(x_ref, mod_ref, w_hd_ref, w_lr_ref, w_tl_ref, w_a2_ref, b_a_ref,
                        z_ref, a_ref, *, lay):
    d = lay.d
    shift = mod_ref[:, 0:d]
    scale = mod_ref[:, d:2 * d]
    u = (x_ref[...] * (1.0 + scale) + shift).astype(BF16)
    w_in_refs = (w_hd_ref, w_lr_ref, w_tl_ref)
    for piece, s0, s1, d0 in lay.groups:
        z_ref[:, d0:d0 + s1 - s0] = jnp.dot(u, _unpack_rows(w_in_refs[piece][:, s0:s1]),
                                            preferred_element_type=F32)
    a_lr = z_ref[:, lay.off_a:lay.off_a + lay.rank_pad].astype(BF16)
    la = jnp.dot(a_lr, _unpack_rows(w_a2_ref[...]), preferred_element_type=F32) + b_a_ref[...]
    a_ref[...] = jnp.exp(_log_sigmoid(la) * (1.0 / GLA_TAU))


def _sample_proj(x3, mod_s, w_hd, w_lr, w_tl, w_a2, b_a, layer, lay):
    n, _, d = x3.shape
    kern = functools.partial(_sample_proj_kernel, lay=lay)
    weights = (w_hd, w_lr, w_tl, w_a2, b_a)
    vmem = 2 * d * lay.n_pad + 2 * 4 * n * lay.n_pad + 16 * MIB
    return pl.pallas_call(
        kern,
        grid=(1,),
        in_specs=[pl.BlockSpec((n, None, d), lambda i: (0, 0, 0)),
                  _layer_spec(mod_s.shape[1:], layer)]
        + [_layer_spec(w.shape[1:], layer, True) for w in weights],
        out_specs=[pl.BlockSpec((n, lay.n_pad), lambda i: (0, 0)),
                   pl.BlockSpec((n, lay.qk), lambda i: (0, 0))],
        out_shape=[jax.ShapeDtypeStruct((n, lay.n_pad), F32),
                   jax.ShapeDtypeStruct((n, lay.qk), F32)],
        compiler_params=pltpu.CompilerParams(dimension_semantics=("arbitrary",),
                                             vmem_limit_bytes=vmem),
        name="sample_proj",
    )(x3, mod_s, *weights)


def _sample_out_kernel(o_ref, z_ref, x_ref, mod_ref, buf_ref, gng_ref, cw_ref,
                       w_pa_ref, w_pb_ref, w_o_ref, lng_ref, lnb_ref,
                       y_ref, buf_out_ref, og_scr, *, lay, alpha):
    d, h_n, dv, cw = lay.d, lay.h, lay.dv, lay.cw
    ck = cw_ref.shape[0]
    gate = mod_ref[:, 2 * d:3 * d]
    gng = gng_ref[...]
    for h in range(h_n):
        o = o_ref[:, h * dv:(h + 1) * dv]
        ms = jnp.mean(o * o, axis=-1, keepdims=True)
        on = o * lax.rsqrt(ms + EPS) * gng
        gg = z_ref[:, lay.off_gg + h * dv:lay.off_gg + (h + 1) * dv]
        og_scr[:, h * dv:(h + 1) * dv] = (on * _silu(gg)).astype(BF16)
    pa = jnp.dot(og_scr[...], _unpack_rows(w_pa_ref[...]), preferred_element_type=F32)

    p = z_ref[:, lay.off_cc:lay.off_cc + cw] * z_ref[:, lay.off_ch:lay.off_ch + cw]
    yc = cw_ref[ck - 1:ck, :] * p
    for i in range(ck - 1):
        yc = yc + cw_ref[i:i + 1, :] * buf_ref[:, i * cw:(i + 1) * cw]
    yc = z_ref[:, lay.off_cb:lay.off_cb + cw] * yc * _silu(z_ref[:, lay.off_gc:lay.off_gc + cw])
    pb = jnp.dot(yc.astype(BF16), _unpack_rows(w_pb_ref[...]), preferred_element_type=F32)
    for i in range(ck - 2):
        buf_out_ref[:, i * cw:(i + 1) * cw] = buf_ref[:, (i + 1) * cw:(i + 2) * cw]
    buf_out_ref[:, (ck - 2) * cw:(ck - 1) * cw] = p

    merged = (jax.nn.sigmoid(z_ref[:, lay.off_mg:lay.off_mg + d]) * pa
              + jax.nn.sigmoid(z_ref[:, lay.off_mc:lay.off_mc + d]) * pb)
    out = jnp.dot(merged.astype(BF16), _unpack_rows(w_o_ref[...]),
                  preferred_element_type=F32)
    hres = alpha * x_ref[...] + gate * out
    mu = jnp.mean(hres, axis=-1, keepdims=True)
    dev = hres - mu
    var = jnp.mean(dev * dev, axis=-1, keepdims=True)
    y_ref[...] = dev * lax.rsqrt(var + EPS) * lng_ref[...] + lnb_ref[...]


def _sample_out(o, z, x3, mod_s, buf, gng, conv_w, w_pa, w_pb, w_o, ln_g, ln_b, layer, lay,
                alpha):
    n, _, d = x3.shape
    kern = functools.partial(_sample_out_kernel, lay=lay, alpha=alpha)
    stacked = (gng, conv_w, w_pa, w_pb, w_o, ln_g, ln_b)

    def whole(arr):
        return pl.BlockSpec(arr.shape, lambda i: (0,) * arr.ndim)

    return pl.pallas_call(
        kern,
        grid=(1,),
        in_specs=[whole(o), whole(z), pl.BlockSpec((n, None, d), lambda i: (0, 0, 0)),
                  _layer_spec(mod_s.shape[1:], layer), whole(buf)]
        + [_layer_spec(w.shape[1:], layer) for w in stacked],
        out_specs=[pl.BlockSpec((n, None, d), lambda i: (0, 0, 0)),
                   pl.BlockSpec(buf.shape, lambda i: (0, 0))],
        out_shape=[jax.ShapeDtypeStruct((n, 1, d), F32),
                   jax.ShapeDtypeStruct(buf.shape, F32)],
        scratch_shapes=[pltpu.VMEM((n, lay.vt), BF16)],
        compiler_params=pltpu.CompilerParams(dimension_semantics=("arbitrary",),
                                             vmem_limit_bytes=48 * MIB),
        name="sample_out",
    )(o, z, x3, mod_s, buf, *stacked)


def kernel(x_prompt, x_sample, c_prompt, c_sample, state_gla, state_conv, w_ada, b_ada, w_in,
           w_a2, b_a, gla_norm_g, conv_w, w_pa, w_pb, w_o, ln_g, ln_b):
    depth, n_s, heads, dk, dv = state_gla.shape
    d = x_prompt.shape[2]
    rank = w_a2.shape[1]
    ck, cw = conv_w.shape[1:]
    lay = _Layout(d, heads, dk, dv, rank, cw)
    alpha = (2 * depth) ** 0.25

    mod_p, mod_s = _ada_mod(c_prompt, c_sample, w_ada, b_ada)

    w_hd, w_lr, w_tl = _split_w_in(w_in, lay)
    w_a2_p = _cast_bf16(jnp.concatenate(
        [w_a2, jnp.zeros((depth, lay.rank_pad - rank, lay.qk), w_a2.dtype)], axis=1))
    w_pa_b, w_pb_b, w_o_b = _cast_bf16(w_pa), _cast_bf16(w_pb), _cast_bf16(w_o)
    b_a3 = b_a.reshape(depth, 1, -1)
    gng3 = gla_norm_g.reshape(depth, 1, -1)
    lng3 = ln_g.reshape(depth, 1, -1)
    lnb3 = ln_b.reshape(depth, 1, -1)
    params = (w_hd, w_lr, w_tl, w_a2_p, b_a3, gng3, conv_w, w_pa_b, w_pb_b, w_o_b, lng3, lnb3)

    hp, hs = x_prompt, x_sample
    states = None
    conv_s = []
    for l in range(depth):
        z, a = _sample_proj(hs, mod_s, w_hd, w_lr, w_tl, w_a2_p, b_a3, l, lay)
        hp, *states, o = _prompt_layer(hp, mod_p, a, z, state_gla, params, states, l, lay, alpha)
        hs, buf_new = _sample_out(o.reshape(n_s, lay.vt), z, hs, mod_s,
                                  state_conv[l].reshape(n_s, (ck - 1) * cw),
                                  gng3, conv_w, w_pa_b, w_pb_b, w_o_b, lng3, lnb3, l, lay, alpha)
        conv_s.append(buf_new.reshape(n_s, ck - 1, cw))
    return (hp, hs, states[0], states[1], states[2], jnp.stack(conv_s))
```

```python
import functools

import jax
import jax.numpy as jnp
from jax import lax
from jax.experimental import pallas as pl
from jax.experimental.pallas import tpu as pltpu

F32 = jnp.float32
BF16 = jnp.bfloat16

GLA_TAU = 16.0
GLA_CHUNK = 64
EPS = 1e-5

LANES = 128
SUBLANES = 8
VMEM_PHYSICAL_BYTES = 64 * 1024 * 1024
MIB = 1024 * 1024

PROMPT_TILE = 256
PROJ_COLS = 256
PREP_COLS = 512


LOG2_E = 1.4426950408889634


def _log_sigmoid(x):
    return jnp.minimum(x, 0.0) - jnp.log(1.0 + jnp.exp(-jnp.abs(x)))


def _cumsum_rows(x):
    row = lax.broadcasted_iota(jnp.int32, (SUBLANES, x.shape[1]), 0)
    groups = []
    carry = None
    for g in range(x.shape[0] // SUBLANES):
        t = x[g * SUBLANES:(g + 1) * SUBLANES, :]
        s = 1
        while s < SUBLANES:
            t = t + jnp.where(row >= s, pltpu.roll(t, s, 0), 0.0)
            s *= 2
        if carry is not None:
            t = t + carry
        carry = jnp.broadcast_to(t[SUBLANES - 1:SUBLANES, :], t.shape)
        groups.append(t)
    return jnp.concatenate(groups, axis=0)


def _silu(x):
    return x * jax.nn.sigmoid(x)


PACKED = jnp.uint32
PACK = 2


def _pack_rows(w_bf16):
    return pltpu.bitcast(w_bf16, PACKED)


def _unpack_rows(w_packed):
    return pltpu.bitcast(w_packed, BF16)


class _Layout:
    def __init__(self, d_model, heads, dk, dv, rank, conv_w):
        self.d = d_model
        self.h = heads
        self.dk = dk
        self.dv = dv
        self.rank = rank
        self.cw = conv_w
        self.qk = heads * dk
        self.vt = heads * dv
        self.rank_pad = -(-rank // LANES) * LANES
        self.off_q = 0
        self.off_k = self.off_q + self.qk
        self.off_v = self.off_k + self.qk
        self.off_gg = self.off_v + self.vt
        self.off_a = self.off_gg + self.vt
        self.off_cb = self.off_a + self.rank_pad
        self.off_cc = self.off_cb + conv_w
        self.off_ch = self.off_cc + conv_w
        self.off_gc = self.off_ch + conv_w
        self.off_mg = self.off_gc + conv_w
        self.off_mc = self.off_mg + d_model
        self.n_pad = self.off_mc + d_model
        self.n_head = self.off_a
        self.n_tail = self.n_pad - self.off_cb
        self.groups = (
            (0, self.off_q, self.off_v, self.off_q),
            (0, self.off_v, self.off_gg, self.off_v),
            (0, self.off_gg, self.off_a, self.off_gg),
            (1, 0, self.rank_pad, self.off_a),
            (2, 0, 2 * conv_w, self.off_cb),
            (2, 2 * conv_w, 4 * conv_w, self.off_ch),
            (2, 4 * conv_w, 4 * conv_w + d_model, self.off_mg),
            (2, 4 * conv_w + d_model, self.n_tail, self.off_mc))


def _layer_spec(shape, layer, single_buffer=False):
    zeros = (0,) * len(shape)
    kwargs = {"pipeline_mode": pl.Buffered(1)} if single_buffer else {}
    return pl.BlockSpec((None,) + tuple(shape), lambda *_: (layer,) + zeros, **kwargs)


def _transpose_cast_kernel(w_ref, o_ref, *, valid):
    wt = jnp.transpose(w_ref[0])
    if valid < wt.shape[1]:
        lane = lax.broadcasted_iota(jnp.int32, wt.shape, 1)
        wt = jnp.where(lane < valid, wt, 0.0)
    o_ref[...] = _pack_rows(wt.astype(BF16))


def _transpose_cast(w_t, col0, n_valid, n_out, cols):
    depth, n_in, d = w_t.shape
    assert n_out % cols == 0 and col0 % SUBLANES == 0 and col0 + n_out <= n_in
    kern = functools.partial(_transpose_cast_kernel, valid=min(n_valid, cols))
    return pl.pallas_call(
        kern,
        grid=(depth, n_out // cols),
        in_specs=[pl.BlockSpec((pl.Element(1), pl.Element(cols), pl.Element(d)),
                               lambda l, j: (l, pl.multiple_of(col0 + j * cols, SUBLANES), 0))],
        out_specs=pl.BlockSpec((None, d // PACK, cols), lambda l, j: (l, 0, j)),
        out_shape=jax.ShapeDtypeStruct((depth, d // PACK, n_out), PACKED),
        compiler_params=pltpu.CompilerParams(
            dimension_semantics=("arbitrary", "arbitrary")),
        name="transpose_cast",
    )(w_t)


def _split_w_in(w_in, lay):
    n_in = w_in.shape[2]
    a_end = lay.off_a + lay.rank
    assert n_in == a_end + lay.n_tail and lay.rank <= lay.rank_pad == LANES
    w_t = jnp.swapaxes(w_in, 1, 2)
    head = _transpose_cast(w_t, 0, lay.n_head, lay.n_head, PREP_COLS)
    low = _transpose_cast(w_t, lay.off_a, lay.rank, lay.rank_pad, lay.rank_pad)
    tail = _transpose_cast(w_t, a_end, lay.n_tail, lay.n_tail, PREP_COLS)
    return head, low, tail


def _cast_kernel(w_ref, o_ref):
    o_ref[...] = _pack_rows(w_ref[...].astype(BF16))


def _cast_bf16(w):
    depth, rows, cols = w.shape
    return pl.pallas_call(
        _cast_kernel,
        grid=(depth,),
        in_specs=[pl.BlockSpec((None, rows, cols), lambda l: (l, 0, 0))],
        out_specs=pl.BlockSpec((None, rows // PACK, cols), lambda l: (l, 0, 0)),
        out_shape=jax.ShapeDtypeStruct((depth, rows // PACK, cols), PACKED),
        compiler_params=pltpu.CompilerParams(
            dimension_semantics=("arbitrary",),
            vmem_limit_bytes=2 * rows * cols * 6 + 16 * MIB),
        name="cast_bf16",
    )(w)


def _ada_kernel(cp_ref, cs_ref, w_ref, b_ref, op_ref, os_ref):
    w = w_ref[...].astype(BF16)
    b = b_ref[...]
    op_ref[...] = jnp.dot(cp_ref[...].astype(BF16), w, preferred_element_type=F32) + b
    os_ref[...] = jnp.dot(cs_ref[...].astype(BF16), w, preferred_element_type=F32) + b


def _ada_mod(c_prompt, c_sample, w_ada, b_ada):
    depth, d, n3 = w_ada.shape
    bp, ns = c_prompt.shape[0], c_sample.shape[0]
    tn = d
    return pl.pallas_call(
        _ada_kernel,
        grid=(depth, n3 // tn),
        in_specs=[
            pl.BlockSpec((bp, d), lambda l, j: (0, 0)),
            pl.BlockSpec((ns, d), lambda l, j: (0, 0)),
            pl.BlockSpec((None, d, tn), lambda l, j: (l, 0, j)),
            pl.BlockSpec((None, 1, tn), lambda l, j: (l, 0, j)),
        ],
        out_specs=[pl.BlockSpec((None, bp, tn), lambda l, j: (l, 0, j)),
                   pl.BlockSpec((None, ns, tn), lambda l, j: (l, 0, j))],
        out_shape=[jax.ShapeDtypeStruct((depth, bp, n3), F32),
                   jax.ShapeDtypeStruct((depth, ns, n3), F32)],
        compiler_params=pltpu.CompilerParams(
            dimension_semantics=("arbitrary", "arbitrary")),
        name="ada_mod",
    )(c_prompt, c_sample, w_ada, b_ada.reshape(depth, 1, n3))


def _project_plan(lay):
    plan = []
    for piece, s0, s1, d0 in lay.groups:
        for c0 in range(s0, s1, PROJ_COLS):
            plan.append((piece, c0, min(c0 + PROJ_COLS, s1), d0 + c0 - s0))
    return plan


def _project_stages(x_rows, mod_ref, brow, w_in_refs, u_scr, z_dst, lay):
    d = lay.d
    shift = mod_ref[pl.ds(brow, 1), 0:d]
    scale = mod_ref[pl.ds(brow, 1), d:2 * d]
    u_scr[...] = (x_rows() * (1.0 + scale) + shift).astype(BF16)
    yield
    for piece, c0, c1, d0 in _project_plan(lay):
        z_dst[:, d0:d0 + c1 - c0] = jnp.dot(u_scr[...],
                                            _unpack_rows(w_in_refs[piece][:, c0:c1]),
                                            preferred_element_type=F32)
        yield


def _tile_chunks(tile):
    return [(c, c * GLA_CHUNK, (c + 1) * GLA_CHUNK) for c in range(tile // GLA_CHUNK)]


def _tile_heads(lay):
    return [(h, slice(h * lay.dk, (h + 1) * lay.dk), slice(h * lay.dv, (h + 1) * lay.dv))
            for h in range(lay.h)]


W_DECAY, W_FACTORS, W_SCORES, W_INCR, W_RECUR, W_OUT, W_NORM, W_CONV, W_MERGE, W_LN = (
    4, 4, 1, 1, 5, 1, 11, 10, 10, 10)


def _premix_weights(lay, tile):
    return [W_DECAY] + [W_FACTORS] * (len(_tile_chunks(tile)) * lay.h)


def _premix_stages(z_scr, w_a2_ref, b_a_ref, la_scr, qin_scr, kin_scr, kst_scr, vb_scr, dec_scr,
                   lay, tile):
    dk, dv = lay.dk, lay.dv
    q_scale = dk ** -0.5

    a_lr = z_scr[:, lay.off_a:lay.off_a + lay.rank_pad].astype(BF16)
    la_scr[...] = (jnp.dot(a_lr, _unpack_rows(w_a2_ref[...]), preferred_element_type=F32)
                   + b_a_ref[...])
    yield W_DECAY

    for c, r0, r1 in _tile_chunks(tile):
        for h, ks, vs in _tile_heads(lay):
            log_a = _log_sigmoid(la_scr[r0:r1, ks]) * (1.0 / GLA_TAU)
            b2 = _cumsum_rows(log_a) * LOG2_E
            bl2 = b2[GLA_CHUNK - 1:GLA_CHUNK, :]
            q_in = z_scr[r0:r1, lay.off_q + h * dk:lay.off_q + (h + 1) * dk] * q_scale
            k = z_scr[r0:r1, lay.off_k + h * dk:lay.off_k + (h + 1) * dk]
            qin_scr[r0:r1, ks] = (q_in * jnp.exp2(b2)).astype(BF16)
            kin_scr[r0:r1, ks] = (k * jnp.exp2(-b2)).astype(BF16)
            kst_scr[r0:r1, ks] = (k * jnp.exp2(bl2 - b2)).astype(BF16)
            dec_scr[c:c + 1, ks] = jnp.exp2(bl2)
            vb_scr[r0:r1, vs] = z_scr[r0:r1, lay.off_v + h * dv:lay.off_v + (h + 1) * dv] \
                .astype(BF16)
            yield W_FACTORS


def _mix_weights(lay, tile):
    chunks = len(_tile_chunks(tile))
    head = ([W_SCORES] * chunks + [W_INCR] * lay.h + [W_RECUR] * lay.h + [W_OUT] * chunks
            + [W_NORM] * chunks + [W_CONV] * chunks)
    tail = [0, 0] + [W_MERGE] * chunks + [0] + [W_LN] * chunks
    return head, tail


def _mix_stages(z_scr, x_ref, y_ref, row0, gate, gng_ref, cw_ref,
                w_pa_ref, w_pb_ref, w_o_ref, lng_ref, lnb_ref,
                p_scr, s_scr, og_scr, yc_scr, mg_scr,
                qin_scr, kin_scr, kst_scr, vb_scr, dec_scr, att_scr, ds_scr, sbf_scr,
                lay, tile, alpha):
    d, h_n, dk, dv, cw = lay.d, lay.h, lay.dk, lay.dv, lay.cw
    ck = cw_ref.shape[0]
    halo = SUBLANES
    chunks = _tile_chunks(tile)
    heads = _tile_heads(lay)
    causal = (lax.broadcasted_iota(jnp.int32, (GLA_CHUNK, GLA_CHUNK), 0)
              >= lax.broadcasted_iota(jnp.int32, (GLA_CHUNK, GLA_CHUNK), 1))
    gng = gng_ref[...]

    for c, r0, r1 in chunks:
        for h, ks, vs in heads:
            att = lax.dot_general(qin_scr[r0:r1, ks], kin_scr[r0:r1, ks],
                                  (((1,), (1,)), ((), ())), preferred_element_type=F32)
            att_scr[h, r0:r1, :] = jnp.where(causal, att, 0.0).astype(BF16)
        yield W_SCORES

    for h, ks, vs in heads:
        for c, r0, r1 in chunks:
            ds_scr[c * h_n + h] = lax.dot_general(kst_scr[r0:r1, ks], vb_scr[r0:r1, vs],
                                                  (((0,), (0,)), ((), ())),
                                                  preferred_element_type=F32)
        yield W_INCR
    for h, ks, vs in heads:
        state = s_scr[h]
        for c, r0, r1 in chunks:
            sbf_scr[c * h_n + h] = state.astype(BF16)
            dec = jnp.transpose(jnp.broadcast_to(dec_scr[c:c + 1, ks], (dk, dk)))
            dec = jnp.concatenate([dec] * (dv // dk), axis=1)
            state = dec * state + ds_scr[c * h_n + h]
        s_scr[h] = state
        yield W_RECUR

    for c, r0, r1 in chunks:
        for h, ks, vs in heads:
            lhs = jnp.concatenate([qin_scr[r0:r1, ks], att_scr[h, r0:r1, :]], axis=1)
            rhs = jnp.concatenate([sbf_scr[c * h_n + h], vb_scr[r0:r1, vs]], axis=0)
            z_scr[r0:r1, vs] = jnp.dot(lhs, rhs, preferred_element_type=F32)
        yield W_OUT

    for c, r0, r1 in chunks:
        for h, ks, vs in heads:
            o = z_scr[r0:r1, vs]
            ms = jnp.mean(o * o, axis=-1, keepdims=True)
            on = o * lax.rsqrt(ms + EPS) * gng
            gg = z_scr[r0:r1, lay.off_gg + h * dv:lay.off_gg + (h + 1) * dv]
            og_scr[r0:r1, vs] = (on * _silu(gg)).astype(BF16)
        yield W_NORM

    for c, r0, r1 in chunks:
        for g0 in range(0, cw, LANES):
            gl = slice(g0, g0 + LANES)
            p = (z_scr[r0:r1, lay.off_cc + g0:lay.off_cc + g0 + LANES]
                 * z_scr[r0:r1, lay.off_ch + g0:lay.off_ch + g0 + LANES])
            p_scr[halo + r0:halo + r1, gl] = p
            ext = jnp.concatenate([p_scr[r0:halo + r0, gl], p], axis=0)
            yc = cw_ref[ck - 1:ck, gl] * p
            for i in range(ck - 1):
                lag = ck - 1 - i
                yc = yc + cw_ref[i:i + 1, gl] * pltpu.roll(ext, lag, 0)[halo:, :]
            yc = (z_scr[r0:r1, lay.off_cb + g0:lay.off_cb + g0 + LANES] * yc
                  * _silu(z_scr[r0:r1, lay.off_gc + g0:lay.off_gc + g0 + LANES]))
            yc_scr[r0:r1, gl] = yc.astype(BF16)
        yield W_CONV

    p_scr[0:halo, :] = p_scr[tile:tile + halo, :]

    z_scr[:, 0:d] = jnp.dot(og_scr[...], _unpack_rows(w_pa_ref[...]), preferred_element_type=F32)
    yield 0
    z_scr[:, d:2 * d] = jnp.dot(yc_scr[...], _unpack_rows(w_pb_ref[...]),
                                preferred_element_type=F32)
    yield 0
    for c, r0, r1 in chunks:
        for g0 in range(0, d, 2 * LANES):
            g1 = g0 + 2 * LANES
            merged = (jax.nn.sigmoid(z_scr[r0:r1, lay.off_mg + g0:lay.off_mg + g1])
                      * z_scr[r0:r1, g0:g1]
                      + jax.nn.sigmoid(z_scr[r0:r1, lay.off_mc + g0:lay.off_mc + g1])
                      * z_scr[r0:r1, d + g0:d + g1])
            mg_scr[r0:r1, g0:g1] = merged.astype(BF16)
        yield W_MERGE
    z_scr[:, 2 * d:3 * d] = jnp.dot(mg_scr[...], _unpack_rows(w_o_ref[...]),
                                    preferred_element_type=F32)
    yield 0

    lng = lng_ref[...]
    lnb = lnb_ref[...]
    for c, r0, r1 in chunks:
        hres = alpha * x_ref[0, row0 + r0:row0 + r1, :] + gate * z_scr[r0:r1, 2 * d:3 * d]
        mu = jnp.mean(hres, axis=-1, keepdims=True)
        dev = hres - mu
        var = jnp.mean(dev * dev, axis=-1, keepdims=True)
        y_ref[0, row0 + r0:row0 + r1, :] = dev * lax.rsqrt(var + EPS) * lng + lnb
        yield W_LN


def _emit_overlapped(mix, mix_weights, premix, premix_weights, proj, n_proj, premix_needs_proj,
                     fillers):
    head, tail = mix_weights
    order = [("mix", w) for w in head]
    done = 0
    for i, w in enumerate(tail):
        while done * len(tail) < i * len(premix_weights):
            order.append(("premix", premix_weights[done]))
            done += 1
        order.append(("mix", w))
    order += [("premix", w) for w in premix_weights[done:]]

    total = sum(w for _, w in order)
    matmul_slots = sum(1 for _, w in order if w == 0)
    done_w = done_p = done_f = slot = 0
    for source, weight in order:
        done_w += weight
        while done_p * total < done_w * n_proj or (source == "premix"
                                                    and done_p < premix_needs_proj):
            next(proj)
            done_p += 1
        if weight == 0:
            slot += 1
            while done_f * matmul_slots < slot * len(fillers):
                fillers[done_f]()
                done_f += 1
        assert next(mix if source == "mix" else premix) == weight
    while done_p < n_proj:
        next(proj)
        done_p += 1
    assert done_f == len(fillers) and next(mix, None) is None and next(premix, None) is None


def _sample_state_step(i, a_ref, qkv_ref, s_ref, s_out_ref, o_ref, lay):
    h_n, dk, dv = lay.h, lay.dk, lay.dv
    q_scale = dk ** -0.5
    r = slice(i, i + 1)
    rows = [a_ref[r, h * dk:(h + 1) * dk] for h in range(h_n)]
    rows += [qkv_ref[r, lay.off_k + h * dk:lay.off_k + (h + 1) * dk] for h in range(h_n)]
    rows += [qkv_ref[r, lay.off_q + h * dk:lay.off_q + (h + 1) * dk] for h in range(h_n)]
    rows.append(jnp.zeros((dk - len(rows), dk), F32))
    mt = jnp.transpose(jnp.concatenate(rows, axis=0))
    for h in range(h_n):
        a_col = mt[:, h:h + 1]
        k_col = mt[:, h_n + h:h_n + h + 1]
        q_col = mt[:, 2 * h_n + h:2 * h_n + h + 1]
        v = qkv_ref[r, lay.off_v + h * dv:lay.off_v + (h + 1) * dv]
        s_new = a_col * s_ref[i, h] + k_col * v
        s_out_ref[i, h] = s_new
        o_ref[i:i + 1, h * dv:(h + 1) * dv] = q_scale * jnp.sum(q_col * s_new, axis=0,
                                                               keepdims=True)


def _prompt_kernel(xc_ref, xn_ref, mod_ref, sa_ref, sqkv_ref, sst_ref,
                   w_hd_ref, w_lr_ref, w_tl_ref, w_a2_ref, b_a_ref,
                   gng_ref, cw_ref, w_pa_ref, w_pb_ref, w_o_ref, lng_ref, lnb_ref, *rest,
                   lay, tile, alpha, n_t, n_tiles, sample_rows):
    (y_ref, sg_ref, sc_ref, sso_ref, so_ref,
     ua_scr, ub_scr, za_scr, zb_scr, la_scr, p_scr, s_scr, og_scr, yc_scr, mg_scr,
     qin_scr, kin_scr, kst_scr, vb_scr, dec_scr, att_scr, ds_scr, sbf_scr) = rest[-23:]
    j = pl.program_id(0)
    steps_per_seq = n_t // 2
    jj = lax.rem(j, steps_per_seq)
    b_cur = lax.div(j, steps_per_seq)
    b_next = lax.div(jnp.minimum(2 * j + 2, n_tiles - 1), n_t)
    d = lay.d
    ck = cw_ref.shape[0]
    halo = SUBLANES
    w_in_refs = (w_hd_ref, w_lr_ref, w_tl_ref)

    @pl.when(jj == 0)
    def _():
        s_scr[...] = jnp.zeros_like(s_scr)
        p_scr[0:halo, :] = jnp.zeros((halo, lay.cw), F32)

    def project(x_rows, brow, u_scr, z_dst):
        return _project_stages(x_rows, mod_ref, brow, w_in_refs, u_scr, z_dst, lay)

    operands = (qin_scr, kin_scr, kst_scr, vb_scr, dec_scr)

    def premix(z_scr):
        return _premix_stages(z_scr, w_a2_ref, b_a_ref, la_scr, *operands, lay, tile)

    def mix(z_scr, row0):
        return _mix_stages(z_scr, xc_ref, y_ref, row0, gate, gng_ref, cw_ref, w_pa_ref, w_pb_ref,
                           w_o_ref, lng_ref, lnb_ref, p_scr, s_scr, og_scr, yc_scr, mg_scr,
                           *operands, att_scr, ds_scr, sbf_scr, lay, tile, alpha)

    @pl.when(j == 0)
    def _():
        for _ in project(lambda: xc_ref[0, 0:tile, :], b_cur, ua_scr, za_scr):
            pass
        for _ in premix(za_scr):
            pass

    gate = mod_ref[pl.ds(b_cur, 1), 2 * d:3 * d]
    mix_weights = _mix_weights(lay, tile)
    premix_weights = _premix_weights(lay, tile)
    plan = _project_plan(lay)
    n_proj = 1 + len(plan)
    premix_needs_proj = 1 + sum(1 for _, _, _, d0 in plan if d0 < lay.off_cb)

    fillers = [functools.partial(_sample_state_step, i, sa_ref, sqkv_ref, sst_ref,
                                 sso_ref, so_ref, lay) for i in range(sample_rows)]
    half = sample_rows // 2

    _emit_overlapped(mix(za_scr, 0), mix_weights, premix(zb_scr), premix_weights,
                     project(lambda: xc_ref[0, tile:2 * tile, :], b_cur, ub_scr, zb_scr), n_proj,
                     premix_needs_proj, fillers[:half])
    _emit_overlapped(mix(zb_scr, tile), mix_weights, premix(za_scr), premix_weights,
                     project(lambda: xn_ref[0], b_next, ua_scr, za_scr), n_proj,
                     premix_needs_proj, fillers[half:])

    @pl.when(jj == steps_per_seq - 1)
    def _():
        sg_ref[0] = s_scr[...]
        sc_ref[0] = p_scr[halo - (ck - 1):halo, :]


def _prompt_vmem_bytes(lay, tile, sample_rows):
    weights = 2 * (lay.d * lay.n_pad + lay.rank_pad * lay.qk + lay.vt * lay.d
                   + lay.cw * lay.d + lay.d * lay.d)
    state = lay.h * lay.dk * lay.dv * 4
    io = (2 * (2 * tile + tile + 2 * tile) * lay.d * 4 + 2 * state
          + 2 * 2 * sample_rows * state + 2 * SUBLANES * (lay.qk + lay.off_gg + lay.vt) * 4)
    chunk_heads = (tile // GLA_CHUNK) * lay.h
    scratch = (2 * tile * lay.d * 2 + 2 * tile * lay.n_pad * 4 + tile * lay.qk * 4
               + (tile + SUBLANES) * lay.cw * 4 + lay.h * lay.dk * lay.dv * 4
               + tile * lay.vt * 2 + tile * lay.cw * 2 + tile * lay.d * 2
               + 3 * tile * lay.qk * 2 + tile * lay.vt * 2 + SUBLANES * lay.qk * 4
               + lay.h * tile * LANES * 2 + chunk_heads * lay.dk * lay.dv * 6)
    temporaries = tile * lay.n_pad * 2
    return min(weights + io + scratch + temporaries, VMEM_PHYSICAL_BYTES * 15 // 16)


def _prompt_layer(x, mod_p, sample_a, sample_z, sample_state, params, prev_states, layer, lay,
                  alpha):
    (w_hd, w_lr, w_tl, w_a2, b_a, gng, conv_w, w_pa, w_pb, w_o, ln_g, ln_b) = params
    bsz, seq, d = x.shape
    depth = w_hd.shape[0]
    tile = PROMPT_TILE
    assert seq % (2 * tile) == 0 and tile % GLA_CHUNK == 0
    ck = conv_w.shape[1]
    n_t = seq // tile
    n_tiles = bsz * n_t
    n_steps = n_tiles // 2
    steps_per_seq = n_t // 2
    n_chunk_heads = (tile // GLA_CHUNK) * lay.h
    assert tile // GLA_CHUNK <= SUBLANES
    n_s = sample_state.shape[1]
    sample_rows = n_s // n_steps
    assert sample_rows * n_steps == n_s and sample_rows % 2 == 0
    sample_a = sample_a.reshape(n_steps, sample_rows, lay.qk)
    sample_qkv = sample_z[:, :lay.off_gg].reshape(n_steps, sample_rows, lay.off_gg)

    def const(arr, single_buffer=True):
        return _layer_spec(arr.shape[1:], layer, single_buffer)

    def next_tile(j):
        g = jnp.minimum(2 * j + 2, n_tiles - 1)
        return (g // n_t, g % n_t, 0)

    sample_state_spec = pl.BlockSpec((None, sample_rows, lay.h, lay.dk, lay.dv),
                                     lambda j: (layer, j, 0, 0, 0))
    in_specs = [
        pl.BlockSpec((1, 2 * tile, d), lambda j: (j // steps_per_seq, j % steps_per_seq, 0)),
        pl.BlockSpec((1, tile, d), next_tile),
        const(mod_p, False),
        pl.BlockSpec((None, sample_rows, lay.qk), lambda j: (j, 0, 0)),
        pl.BlockSpec((None, sample_rows, lay.off_gg), lambda j: (j, 0, 0)),
        sample_state_spec,
    ] + [const(a) for a in params]
    args = [x, x, mod_p, sample_a, sample_qkv, sample_state, *params]
    aliases = {}
    if prev_states is not None:
        for i, prev in enumerate(prev_states):
            in_specs.append(pl.BlockSpec(memory_space=pl.ANY))
            aliases[len(args)] = 1 + i
            args.append(prev)

    kern = functools.partial(_prompt_kernel, lay=lay, tile=tile, alpha=alpha, n_t=n_t,
                             n_tiles=n_tiles, sample_rows=sample_rows)
    return pl.pallas_call(
        kern,
        grid=(n_steps,),
        in_specs=in_specs,
        out_specs=[
            pl.BlockSpec((1, 2 * tile, d), lambda j: (j // steps_per_seq, j % steps_per_seq, 0)),
            pl.BlockSpec((None, 1, lay.h, lay.dk, lay.dv),
                         lambda j: (layer, j // steps_per_seq, 0, 0, 0)),
            pl.BlockSpec((None, 1, ck - 1, lay.cw), lambda j: (layer, j // steps_per_seq, 0, 0)),
            sample_state_spec,
            pl.BlockSpec((None, sample_rows, lay.vt), lambda j: (j, 0, 0)),
        ],
        out_shape=[
            jax.ShapeDtypeStruct((bsz, seq, d), F32),
            jax.ShapeDtypeStruct((depth, bsz, lay.h, lay.dk, lay.dv), F32),
            jax.ShapeDtypeStruct((depth, bsz, ck - 1, lay.cw), F32),
            jax.ShapeDtypeStruct(sample_state.shape, F32),
            jax.ShapeDtypeStruct((n_steps, sample_rows, lay.vt), F32),
        ],
        scratch_shapes=[
            pltpu.VMEM((tile, d), BF16),
            pltpu.VMEM((tile, d), BF16),
            pltpu.VMEM((tile, lay.n_pad), F32),
            pltpu.VMEM((tile, lay.n_pad), F32),
            pltpu.VMEM((tile, lay.qk), F32),
            pltpu.VMEM((tile + SUBLANES, lay.cw), F32),
            pltpu.VMEM((lay.h, lay.dk, lay.dv), F32),
            pltpu.VMEM((tile, lay.vt), BF16),
            pltpu.VMEM((tile, lay.cw), BF16),
            pltpu.VMEM((tile, d), BF16),
            pltpu.VMEM((tile, lay.qk), BF16),
            pltpu.VMEM((tile, lay.qk), BF16),
            pltpu.VMEM((tile, lay.qk), BF16),
            pltpu.VMEM((tile, lay.vt), BF16),
            pltpu.VMEM((SUBLANES, lay.qk), F32),
            pltpu.VMEM((lay.h, tile, GLA_CHUNK), BF16),
            pltpu.VMEM((n_chunk_heads, lay.dk, lay.dv), F32),
            pltpu.VMEM((n_chunk_heads, lay.dk, lay.dv), BF16),
        ],
        input_output_aliases=aliases,
        compiler_params=pltpu.CompilerParams(
            dimension_semantics=("arbitrary",),
            vmem_limit_bytes=_prompt_vmem_bytes(lay, tile, sample_rows)),
        name="prompt_layer",
    )(*args)


def _sample_proj_kernel(x_ref, mod_ref, w_hd_ref, w_lr_ref, w_tl_ref, w_a2_ref, b_a_ref,
                        z_ref, a_ref, *, lay):
    d = lay.d
    shift = mod_ref[:, 0:d]
    scale = mod_ref[:, d:2 * d]
    u = (x_ref[...] * (1.0 + scale) + shift).astype(BF16)
    w_in_refs = (w_hd_ref, w_lr_ref, w_tl_ref)
    for piece, s0, s1, d0 in lay.groups:
        z_ref[:, d0:d0 + s1 - s0] = jnp.dot(u, _unpack_rows(w_in_refs[piece][:, s0:s1]),
                                            preferred_element_type=F32)
    a_lr = z_ref[:, lay.off_a:lay.off_a + lay.rank_pad].astype(BF16)
    la = jnp.dot(a_lr, _unpack_rows(w_a2_ref[...]), preferred_element_type=F32) + b_a_ref[...]
    a_ref[...] = jnp.exp(_log_sigmoid(la) * (1.0 / GLA_TAU))


def _sample_proj(x3, mod_s, w_hd, w_lr, w_tl, w_a2, b_a, layer, lay):
    n, _, d = x3.shape
    kern = functools.partial(_sample_proj_kernel, lay=lay)
    weights = (w_hd, w_lr, w_tl, w_a2, b_a)
    vmem = 2 * d * lay.n_pad + 2 * 4 * n * lay.n_pad + 16 * MIB
    return pl.pallas_call(
        kern,
        grid=(1,),
        in_specs=[pl.BlockSpec((n, None, d), lambda i: (0, 0, 0)),
                  _layer_spec(mod_s.shape[1:], layer)]
        + [_layer_spec(w.shape[1:], layer, True) for w in weights],
        out_specs=[pl.BlockSpec((n, lay.n_pad), lambda i: (0, 0)),
                   pl.BlockSpec((n, lay.qk), lambda i: (0, 0))],
        out_shape=[jax.ShapeDtypeStruct((n, lay.n_pad), F32),
                   jax.ShapeDtypeStruct((n, lay.qk), F32)],
        compiler_params=pltpu.CompilerParams(dimension_semantics=("arbitrary",),
                                             vmem_limit_bytes=vmem),
        name="sample_proj",
    )(x3, mod_s, *weights)


def _sample_out_kernel(o_ref, z_ref, x_ref, mod_ref, buf_ref, gng_ref, cw_ref,
                       w_pa_ref, w_pb_ref, w_o_ref, lng_ref, lnb_ref,
                       y_ref, buf_out_ref, og_scr, *, lay, alpha):
    d, h_n, dv, cw = lay.d, lay.h, lay.dv, lay.cw
    ck = cw_ref.shape[0]
    gate = mod_ref[:, 2 * d:3 * d]
    gng = gng_ref[...]
    for h in range(h_n):
        o = o_ref[:, h * dv:(h + 1) * dv]
        ms = jnp.mean(o * o, axis=-1, keepdims=True)
        on = o * lax.rsqrt(ms + EPS) * gng
        gg = z_ref[:, lay.off_gg + h * dv:lay.off_gg + (h + 1) * dv]
        og_scr[:, h * dv:(h + 1) * dv] = (on * _silu(gg)).astype(BF16)
    pa = jnp.dot(og_scr[...], _unpack_rows(w_pa_ref[...]), preferred_element_type=F32)

    p = z_ref[:, lay.off_cc:lay.off_cc + cw] * z_ref[:, lay.off_ch:lay.off_ch + cw]
    yc = cw_ref[ck - 1:ck, :] * p
    for i in range(ck - 1):
        yc = yc + cw_ref[i:i + 1, :] * buf_ref[:, i * cw:(i + 1) * cw]
    yc = z_ref[:, lay.off_cb:lay.off_cb + cw] * yc * _silu(z_ref[:, lay.off_gc:lay.off_gc + cw])
    pb = jnp.dot(yc.astype(BF16), _unpack_rows(w_pb_ref[...]), preferred_element_type=F32)
    for i in range(ck - 2):
        buf_out_ref[:, i * cw:(i + 1) * cw] = buf_ref[:, (i + 1) * cw:(i + 2) * cw]
    buf_out_ref[:, (ck - 2) * cw:(ck - 1) * cw] = p

    merged = (jax.nn.sigmoid(z_ref[:, lay.off_mg:lay.off_mg + d]) * pa
              + jax.nn.sigmoid(z_ref[:, lay.off_mc:lay.off_mc + d]) * pb)
    out = jnp.dot(merged.astype(BF16), _unpack_rows(w_o_ref[...]),
                  preferred_element_type=F32)
    hres = alpha * x_ref[...] + gate * out
    mu = jnp.mean(hres, axis=-1, keepdims=True)
    dev = hres - mu
    var = jnp.mean(dev * dev, axis=-1, keepdims=True)
    y_ref[...] = dev * lax.rsqrt(var + EPS) * lng_ref[...] + lnb_ref[...]


def _sample_out(o, z, x3, mod_s, buf, gng, conv_w, w_pa, w_pb, w_o, ln_g, ln_b, layer, lay,
                alpha):
    n, _, d = x3.shape
    kern = functools.partial(_sample_out_kernel, lay=lay, alpha=alpha)
    stacked = (gng, conv_w, w_pa, w_pb, w_o, ln_g, ln_b)

    def whole(arr):
        return pl.BlockSpec(arr.shape, lambda i: (0,) * arr.ndim)

    return pl.pallas_call(
        kern,
        grid=(1,),
        in_specs=[whole(o), whole(z), pl.BlockSpec((n, None, d), lambda i: (0, 0, 0)),
                  _layer_spec(mod_s.shape[1:], layer), whole(buf)]
        + [_layer_spec(w.shape[1:], layer) for w in stacked],
        out_specs=[pl.BlockSpec((n, None, d), lambda i: (0, 0, 0)),
                   pl.BlockSpec(buf.shape, lambda i: (0, 0))],
        out_shape=[jax.ShapeDtypeStruct((n, 1, d), F32),
                   jax.ShapeDtypeStruct(buf.shape, F32)],
        scratch_shapes=[pltpu.VMEM((n, lay.vt), BF16)],
        compiler_params=pltpu.CompilerParams(dimension_semantics=("arbitrary",),
                                             vmem_limit_bytes=48 * MIB),
        name="sample_out",
    )(o, z, x3, mod_s, buf, *stacked)


def kernel(x_prompt, x_sample, c_prompt, c_sample, state_gla, state_conv, w_ada, b_ada, w_in,
           w_a2, b_a, gla_norm_g, conv_w, w_pa, w_pb, w_o, ln_g, ln_b):
    depth, n_s, heads, dk, dv = state_gla.shape
    d = x_prompt.shape[2]
    rank = w_a2.shape[1]
    ck, cw = conv_w.shape[1:]
    lay = _Layout(d, heads, dk, dv, rank, cw)
    alpha = (2 * depth) ** 0.25

    mod_p, mod_s = _ada_mod(c_prompt, c_sample, w_ada, b_ada)

    w_hd, w_lr, w_tl = _split_w_in(w_in, lay)
    w_a2_p = _cast_bf16(jnp.concatenate(
        [w_a2, jnp.zeros((depth, lay.rank_pad - rank, lay.qk), w_a2.dtype)], axis=1))
    w_pa_b, w_pb_b, w_o_b = _cast_bf16(w_pa), _cast_bf16(w_pb), _cast_bf16(w_o)
    b_a3 = b_a.reshape(depth, 1, -1)
    gng3 = gla_norm_g.reshape(depth, 1, -1)
    lng3 = ln_g.reshape(depth, 1, -1)
    lnb3 = ln_b.reshape(depth, 1, -1)
    params = (w_hd, w_lr, w_tl, w_a2_p, b_a3, gng3, conv_w, w_pa_b, w_pb_b, w_o_b, lng3, lnb3)

    hp, hs = x_prompt, x_sample
    states = None
    conv_s = []
    for l in range(depth):
        z, a = _sample_proj(hs, mod_s, w_hd, w_lr, w_tl, w_a2_p, b_a3, l, lay)
        hp, *states, o = _prompt_layer(hp, mod_p, a, z, state_gla, params, states, l, lay, alpha)
        hs, buf_new = _sample_out(o.reshape(n_s, lay.vt), z, hs, mod_s,
                                  state_conv[l].reshape(n_s, (ck - 1) * cw),
                                  gng3, conv_w, w_pa_b, w_pb_b, w_o_b, lng3, lnb3, l, lay, alpha)
        conv_s.append(buf_new.reshape(n_s, ck - 1, cw))
    return (hp, hs, states[0], states[1], states[2], jnp.stack(conv_s))
```

```python
import functools

import jax
import jax.numpy as jnp
from jax import lax
from jax.experimental import pallas as pl
from jax.experimental.pallas import tpu as pltpu

F32 = jnp.float32
BF16 = jnp.bfloat16

GLA_TAU = 16.0
GLA_CHUNK = 64
EPS = 1e-5

LANES = 128
SUBLANES = 8
VMEM_PHYSICAL_BYTES = 64 * 1024 * 1024
MIB = 1024 * 1024

PROMPT_TILE = 256
PROJ_COLS = 256
PREP_COLS = 1024


LOG2_E = 1.4426950408889634


def _log_sigmoid(x):
    return jnp.minimum(x, 0.0) - jnp.log(1.0 + jnp.exp(-jnp.abs(x)))


def _cumsum_rows(x):
    row = lax.broadcasted_iota(jnp.int32, (SUBLANES, x.shape[1]), 0)
    groups = []
    carry = None
    for g in range(x.shape[0] // SUBLANES):
        t = x[g * SUBLANES:(g + 1) * SUBLANES, :]
        s = 1
        while s < SUBLANES:
            t = t + jnp.where(row >= s, pltpu.roll(t, s, 0), 0.0)
            s *= 2
        if carry is not None:
            t = t + carry
        carry = jnp.broadcast_to(t[SUBLANES - 1:SUBLANES, :], t.shape)
        groups.append(t)
    return jnp.concatenate(groups, axis=0)


def _silu(x):
    return x * jax.nn.sigmoid(x)


PACKED = jnp.uint32
PACK = 2


def _pack_rows(w_bf16):
    return pltpu.bitcast(w_bf16, PACKED)


def _unpack_rows(w_packed):
    return pltpu.bitcast(w_packed, BF16)


class _Layout:
    def __init__(self, d_model, heads, dk, dv, rank, conv_w):
        self.d = d_model
        self.h = heads
        self.dk = dk
        self.dv = dv
        self.rank = rank
        self.cw = conv_w
        self.qk = heads * dk
        self.vt = heads * dv
        self.rank_pad = -(-rank // LANES) * LANES
        self.off_q = 0
        self.off_k = self.off_q + self.qk
        self.off_v = self.off_k + self.qk
        self.off_gg = self.off_v + self.vt
        self.off_a = self.off_gg + self.vt
        self.off_cb = self.off_a + self.rank_pad
        self.off_cc = self.off_cb + conv_w
        self.off_ch = self.off_cc + conv_w
        self.off_gc = self.off_ch + conv_w
        self.off_mg = self.off_gc + conv_w
        self.off_mc = self.off_mg + d_model
        self.n_pad = self.off_mc + d_model
        self.n_head = self.off_a
        self.n_tail = self.n_pad - self.off_cb
        self.groups = (
            (0, self.off_q, self.off_v, self.off_q),
            (0, self.off_v, self.off_gg, self.off_v),
            (0, self.off_gg, self.off_a, self.off_gg),
            (1, 0, self.rank_pad, self.off_a),
            (2, 0, 2 * conv_w, self.off_cb),
            (2, 2 * conv_w, 4 * conv_w, self.off_ch),
            (2, 4 * conv_w, 4 * conv_w + d_model, self.off_mg),
            (2, 4 * conv_w + d_model, self.n_tail, self.off_mc))


def _layer_spec(shape, layer, single_buffer=False):
    zeros = (0,) * len(shape)
    kwargs = {"pipeline_mode": pl.Buffered(1)} if single_buffer else {}
    return pl.BlockSpec((None,) + tuple(shape), lambda *_: (layer,) + zeros, **kwargs)


def _whole_spec(arr):
    return pl.BlockSpec(arr.shape, lambda *_: (0,) * arr.ndim)


def _layer_rows(ref, layer):
    return ref.at[pl.ds(layer, 1)]


def _layer_taps(ref, layer):
    return ref.at[:, layer]


def _transpose_cast_kernel(w_ref, o_ref, *, valid):
    wt = jnp.transpose(w_ref[0])
    if valid < wt.shape[1]:
        lane = lax.broadcasted_iota(jnp.int32, wt.shape, 1)
        wt = jnp.where(lane < valid, wt, 0.0)
    o_ref[...] = _pack_rows(wt.astype(BF16))


def _transpose_cast(w_t, col0, n_valid, n_out, cols):
    depth, n_in, d = w_t.shape
    assert n_out % cols == 0 and col0 % SUBLANES == 0 and col0 + n_out <= n_in
    kern = functools.partial(_transpose_cast_kernel, valid=min(n_valid, cols))
    return pl.pallas_call(
        kern,
        grid=(depth, n_out // cols),
        in_specs=[pl.BlockSpec((pl.Element(1), pl.Element(cols), pl.Element(d)),
                               lambda l, j: (l, pl.multiple_of(col0 + j * cols, SUBLANES), 0))],
        out_specs=pl.BlockSpec((None, d // PACK, cols), lambda l, j: (l, 0, j)),
        out_shape=jax.ShapeDtypeStruct((depth, d // PACK, n_out), PACKED),
        compiler_params=pltpu.CompilerParams(
            dimension_semantics=("arbitrary", "arbitrary")),
        name="transpose_cast",
    )(w_t)


def _split_w_in(w_in, lay):
    n_in = w_in.shape[2]
    a_end = lay.off_a + lay.rank
    assert n_in == a_end + lay.n_tail and lay.rank <= lay.rank_pad == LANES
    w_t = jnp.swapaxes(w_in, 1, 2)
    head = _transpose_cast(w_t, 0, lay.n_head, lay.n_head, PREP_COLS)
    low = _transpose_cast(w_t, lay.off_a, lay.rank, lay.rank_pad, lay.rank_pad)
    tail = _transpose_cast(w_t, a_end, lay.n_tail, lay.n_tail, PREP_COLS)
    return head, low, tail


def _cast_kernel(w_ref, o_ref):
    o_ref[...] = _pack_rows(w_ref[...].astype(BF16))


def _cast_bf16(w):
    depth, rows, cols = w.shape
    return pl.pallas_call(
        _cast_kernel,
        grid=(depth,),
        in_specs=[pl.BlockSpec((None, rows, cols), lambda l: (l, 0, 0))],
        out_specs=pl.BlockSpec((None, rows // PACK, cols), lambda l: (l, 0, 0)),
        out_shape=jax.ShapeDtypeStruct((depth, rows // PACK, cols), PACKED),
        compiler_params=pltpu.CompilerParams(
            dimension_semantics=("arbitrary",),
            vmem_limit_bytes=2 * rows * cols * 6 + 16 * MIB),
        name="cast_bf16",
    )(w)


def _ada_kernel(cp_ref, cs_ref, w_ref, b_ref, op_ref, os_ref):
    w = w_ref[...].astype(BF16)
    b = b_ref[...]
    op_ref[...] = jnp.dot(cp_ref[...].astype(BF16), w, preferred_element_type=F32) + b
    os_ref[...] = jnp.dot(cs_ref[...].astype(BF16), w, preferred_element_type=F32) + b


def _ada_mod(c_prompt, c_sample, w_ada, b_ada):
    depth, d, n3 = w_ada.shape
    bp, ns = c_prompt.shape[0], c_sample.shape[0]
    tn = d
    return pl.pallas_call(
        _ada_kernel,
        grid=(depth, n3 // tn),
        in_specs=[
            pl.BlockSpec((bp, d), lambda l, j: (0, 0)),
            pl.BlockSpec((ns, d), lambda l, j: (0, 0)),
            pl.BlockSpec((None, d, tn), lambda l, j: (l, 0, j)),
            pl.BlockSpec((None, 1, tn), lambda l, j: (l, 0, j)),
        ],
        out_specs=[pl.BlockSpec((None, bp, tn), lambda l, j: (l, 0, j)),
                   pl.BlockSpec((None, ns, tn), lambda l, j: (l, 0, j))],
        out_shape=[jax.ShapeDtypeStruct((depth, bp, n3), F32),
                   jax.ShapeDtypeStruct((depth, ns, n3), F32)],
        compiler_params=pltpu.CompilerParams(
            dimension_semantics=("arbitrary", "arbitrary")),
        name="ada_mod",
    )(c_prompt, c_sample, w_ada, b_ada.reshape(depth, 1, n3))


def _project_plan(lay):
    plan = []
    for piece, s0, s1, d0 in lay.groups:
        for c0 in range(s0, s1, PROJ_COLS):
            plan.append((piece, c0, min(c0 + PROJ_COLS, s1), d0 + c0 - s0))
    return plan


def _project_stages(x_rows, mod_ref, brow, w_in_refs, u_scr, z_dst, lay):
    d = lay.d
    shift = mod_ref[pl.ds(brow, 1), 0:d]
    scale = mod_ref[pl.ds(brow, 1), d:2 * d]
    u_scr[...] = (x_rows() * (1.0 + scale) + shift).astype(BF16)
    yield
    for piece, c0, c1, d0 in _project_plan(lay):
        z_dst[:, d0:d0 + c1 - c0] = jnp.dot(u_scr[...],
                                            _unpack_rows(w_in_refs[piece][:, c0:c1]),
                                            preferred_element_type=F32)
        yield


def _tile_chunks(tile):
    return [(c, c * GLA_CHUNK, (c + 1) * GLA_CHUNK) for c in range(tile // GLA_CHUNK)]


def _tile_heads(lay):
    return [(h, slice(h * lay.dk, (h + 1) * lay.dk), slice(h * lay.dv, (h + 1) * lay.dv))
            for h in range(lay.h)]


W_DECAY, W_FACTORS, W_SCORES, W_INCR, W_RECUR, W_OUT, W_NORM, W_CONV, W_MERGE, W_LN = (
    4, 4, 1, 1, 5, 1, 11, 10, 10, 10)


def _premix_weights(lay, tile):
    return [W_DECAY] + [W_FACTORS] * (len(_tile_chunks(tile)) * lay.h)


def _premix_stages(z_scr, w_a2_ref, b_a_ref, la_scr, qin_scr, kin_scr, kst_scr, vb_scr, dec_scr,
                   lay, tile):
    dk, dv = lay.dk, lay.dv
    q_scale = dk ** -0.5

    a_lr = z_scr[:, lay.off_a:lay.off_a + lay.rank_pad].astype(BF16)
    la_scr[...] = (jnp.dot(a_lr, _unpack_rows(w_a2_ref[...]), preferred_element_type=F32)
                   + b_a_ref[...])
    yield W_DECAY

    for c, r0, r1 in _tile_chunks(tile):
        for h, ks, vs in _tile_heads(lay):
            log_a = _log_sigmoid(la_scr[r0:r1, ks]) * (1.0 / GLA_TAU)
            b2 = _cumsum_rows(log_a) * LOG2_E
            bl2 = b2[GLA_CHUNK - 1:GLA_CHUNK, :]
            q_in = z_scr[r0:r1, lay.off_q + h * dk:lay.off_q + (h + 1) * dk] * q_scale
            k = z_scr[r0:r1, lay.off_k + h * dk:lay.off_k + (h + 1) * dk]
            qin_scr[r0:r1, ks] = (q_in * jnp.exp2(b2)).astype(BF16)
            kin_scr[r0:r1, ks] = (k * jnp.exp2(-b2)).astype(BF16)
            kst_scr[r0:r1, ks] = (k * jnp.exp2(bl2 - b2)).astype(BF16)
            dec_scr[c:c + 1, ks] = jnp.exp2(bl2)
            vb_scr[r0:r1, vs] = z_scr[r0:r1, lay.off_v + h * dv:lay.off_v + (h + 1) * dv] \
                .astype(BF16)
            yield W_FACTORS


def _mix_weights(lay, tile):
    chunks = len(_tile_chunks(tile))
    head = ([W_SCORES] * chunks + [W_INCR] * lay.h + [W_RECUR] * lay.h + [W_OUT] * chunks
            + [W_NORM] * chunks + [W_CONV] * chunks)
    tail = [0, 0] + [W_MERGE] * chunks + [0] + [W_LN] * chunks
    return head, tail


def _mix_stages(z_scr, x_ref, y_ref, row0, gate, gng_ref, cw_ref,
                w_pa_ref, w_pb_ref, w_o_ref, lng_ref, lnb_ref,
                p_scr, s_scr, og_scr, yc_scr, mg_scr,
                qin_scr, kin_scr, kst_scr, vb_scr, dec_scr, att_scr, ds_scr, sbf_scr,
                lay, tile, alpha):
    d, h_n, dk, dv, cw = lay.d, lay.h, lay.dk, lay.dv, lay.cw
    ck = cw_ref.shape[0]
    halo = SUBLANES
    chunks = _tile_chunks(tile)
    heads = _tile_heads(lay)
    causal = (lax.broadcasted_iota(jnp.int32, (GLA_CHUNK, GLA_CHUNK), 0)
              >= lax.broadcasted_iota(jnp.int32, (GLA_CHUNK, GLA_CHUNK), 1))
    gng = gng_ref[...]

    for c, r0, r1 in chunks:
        for h, ks, vs in heads:
            att = lax.dot_general(qin_scr[r0:r1, ks], kin_scr[r0:r1, ks],
                                  (((1,), (1,)), ((), ())), preferred_element_type=F32)
            att_scr[h, r0:r1, :] = jnp.where(causal, att, 0.0).astype(BF16)
        yield W_SCORES

    for h, ks, vs in heads:
        for c, r0, r1 in chunks:
            ds_scr[c * h_n + h] = lax.dot_general(kst_scr[r0:r1, ks], vb_scr[r0:r1, vs],
                                                  (((0,), (0,)), ((), ())),
                                                  preferred_element_type=F32)
        yield W_INCR
    for h, ks, vs in heads:
        state = s_scr[h]
        for c, r0, r1 in chunks:
            sbf_scr[c * h_n + h] = state.astype(BF16)
            dec = jnp.transpose(jnp.broadcast_to(dec_scr[c:c + 1, ks], (dk, dk)))
            dec = jnp.concatenate([dec] * (dv // dk), axis=1)
            state = dec * state + ds_scr[c * h_n + h]
        s_scr[h] = state
        yield W_RECUR

    for c, r0, r1 in chunks:
        for h, ks, vs in heads:
            lhs = jnp.concatenate([qin_scr[r0:r1, ks], att_scr[h, r0:r1, :]], axis=1)
            rhs = jnp.concatenate([sbf_scr[c * h_n + h], vb_scr[r0:r1, vs]], axis=0)
            z_scr[r0:r1, vs] = jnp.dot(lhs, rhs, preferred_element_type=F32)
        yield W_OUT

    for c, r0, r1 in chunks:
        for h, ks, vs in heads:
            o = z_scr[r0:r1, vs]
            ms = jnp.mean(o * o, axis=-1, keepdims=True)
            on = o * lax.rsqrt(ms + EPS) * gng
            gg = z_scr[r0:r1, lay.off_gg + h * dv:lay.off_gg + (h + 1) * dv]
            og_scr[r0:r1, vs] = (on * _silu(gg)).astype(BF16)
        yield W_NORM

    for c, r0, r1 in chunks:
        for g0 in range(0, cw, LANES):
            gl = slice(g0, g0 + LANES)
            p = (z_scr[r0:r1, lay.off_cc + g0:lay.off_cc + g0 + LANES]
                 * z_scr[r0:r1, lay.off_ch + g0:lay.off_ch + g0 + LANES])
            p_scr[halo + r0:halo + r1, gl] = p
            ext = jnp.concatenate([p_scr[r0:halo + r0, gl], p], axis=0)
            yc = cw_ref[ck - 1:ck, gl] * p
            for i in range(ck - 1):
                lag = ck - 1 - i
                yc = yc + cw_ref[i:i + 1, gl] * pltpu.roll(ext, lag, 0)[halo:, :]
            yc = (z_scr[r0:r1, lay.off_cb + g0:lay.off_cb + g0 + LANES] * yc
                  * _silu(z_scr[r0:r1, lay.off_gc + g0:lay.off_gc + g0 + LANES]))
            yc_scr[r0:r1, gl] = yc.astype(BF16)
        yield W_CONV

    p_scr[0:halo, :] = p_scr[tile:tile + halo, :]

    z_scr[:, 0:d] = jnp.dot(og_scr[...], _unpack_rows(w_pa_ref[...]), preferred_element_type=F32)
    yield 0
    z_scr[:, d:2 * d] = jnp.dot(yc_scr[...], _unpack_rows(w_pb_ref[...]),
                                preferred_element_type=F32)
    yield 0
    for c, r0, r1 in chunks:
        for g0 in range(0, d, 2 * LANES):
            g1 = g0 + 2 * LANES
            merged = (jax.nn.sigmoid(z_scr[r0:r1, lay.off_mg + g0:lay.off_mg + g1])
                      * z_scr[r0:r1, g0:g1]
                      + jax.nn.sigmoid(z_scr[r0:r1, lay.off_mc + g0:lay.off_mc + g1])
                      * z_scr[r0:r1, d + g0:d + g1])
            mg_scr[r0:r1, g0:g1] = merged.astype(BF16)
        yield W_MERGE
    z_scr[:, 2 * d:3 * d] = jnp.dot(mg_scr[...], _unpack_rows(w_o_ref[...]),
                                    preferred_element_type=F32)
    yield 0

    lng = lng_ref[...]
    lnb = lnb_ref[...]
    for c, r0, r1 in chunks:
        hres = alpha * x_ref[0, row0 + r0:row0 + r1, :] + gate * z_scr[r0:r1, 2 * d:3 * d]
        mu = jnp.mean(hres, axis=-1, keepdims=True)
        dev = hres - mu
        var = jnp.mean(dev * dev, axis=-1, keepdims=True)
        y_ref[0, row0 + r0:row0 + r1, :] = dev * lax.rsqrt(var + EPS) * lng + lnb
        yield W_LN


def _emit_overlapped(mix, mix_weights, premix, premix_weights, proj, n_proj, premix_needs_proj,
                     fillers):
    head, tail = mix_weights
    order = [("mix", w) for w in head]
    done = 0
    for i, w in enumerate(tail):
        while done * len(tail) < i * len(premix_weights):
            order.append(("premix", premix_weights[done]))
            done += 1
        order.append(("mix", w))
    order += [("premix", w) for w in premix_weights[done:]]

    total = sum(w for _, w in order)
    matmul_slots = sum(1 for _, w in order if w == 0)
    done_w = done_p = done_f = slot = 0
    for source, weight in order:
        done_w += weight
        while done_p * total < done_w * n_proj or (source == "premix"
                                                    and done_p < premix_needs_proj):
            next(proj)
            done_p += 1
        if weight == 0:
            slot += 1
            while done_f * matmul_slots < slot * len(fillers):
                fillers[done_f]()
                done_f += 1
        assert next(mix if source == "mix" else premix) == weight
    while done_p < n_proj:
        next(proj)
        done_p += 1
    assert done_f == len(fillers) and next(mix, None) is None and next(premix, None) is None


def _step_row_reader(ref, rows, sub, steps_per_block):
    def read(i, lanes):
        val = ref[i:i + 1, lanes]
        for s in range(1, steps_per_block):
            val = jnp.where(sub == s, ref[s * rows + i:s * rows + i + 1, lanes], val)
        return val
    return read


def _sample_state_step(i, a_row, qkv_row, s_ref, s_out_ref, o_ref, lay):
    h_n, dk, dv = lay.h, lay.dk, lay.dv
    q_scale = dk ** -0.5
    rows = [a_row(i, slice(h * dk, (h + 1) * dk)) for h in range(h_n)]
    rows += [qkv_row(i, slice(lay.off_k + h * dk, lay.off_k + (h + 1) * dk)) for h in range(h_n)]
    rows += [qkv_row(i, slice(lay.off_q + h * dk, lay.off_q + (h + 1) * dk)) for h in range(h_n)]
    rows.append(jnp.zeros((dk - len(rows), dk), F32))
    mt = jnp.transpose(jnp.concatenate(rows, axis=0))
    for h in range(h_n):
        a_col = mt[:, h:h + 1]
        k_col = mt[:, h_n + h:h_n + h + 1]
        q_col = mt[:, 2 * h_n + h:2 * h_n + h + 1]
        v = qkv_row(i, slice(lay.off_v + h * dv, lay.off_v + (h + 1) * dv))
        s_new = a_col * s_ref[i, h] + k_col * v
        s_out_ref[i, h] = s_new
        o_ref[i:i + 1, h * dv:(h + 1) * dv] = q_scale * jnp.sum(q_col * s_new, axis=0,
                                                               keepdims=True)


def _prompt_kernel(xc_ref, xn_ref, mod_ref, sa_ref, sqkv_ref, sst_ref,
                   w_hd_ref, w_lr_ref, w_tl_ref, w_a2_ref, b_a_ref,
                   gng_ref, cw_ref, w_pa_ref, w_pb_ref, w_o_ref, lng_ref, lnb_ref, *rest,
                   lay, layer, tile, alpha, n_t, n_tiles, sample_rows):
    (y_ref, sg_ref, sc_ref, sso_ref, so_ref,
     ua_scr, ub_scr, za_scr, zb_scr, la_scr, p_scr, s_scr, og_scr, yc_scr, mg_scr,
     qin_scr, kin_scr, kst_scr, vb_scr, dec_scr, att_scr, ds_scr, sbf_scr) = rest[-23:]
    b_a_ref, gng_ref, lng_ref, lnb_ref = (_layer_rows(r, layer)
                                          for r in (b_a_ref, gng_ref, lng_ref, lnb_ref))
    cw_ref = _layer_taps(cw_ref, layer)
    j = pl.program_id(0)
    steps_per_seq = n_t // 2
    jj = lax.rem(j, steps_per_seq)
    b_cur = lax.div(j, steps_per_seq)
    b_next = lax.div(jnp.minimum(2 * j + 2, n_tiles - 1), n_t)
    d = lay.d
    ck = cw_ref.shape[0]
    halo = SUBLANES
    w_in_refs = (w_hd_ref, w_lr_ref, w_tl_ref)

    @pl.when(jj == 0)
    def _():
        s_scr[...] = jnp.zeros_like(s_scr)
        p_scr[0:halo, :] = jnp.zeros((halo, lay.cw), F32)

    def project(x_rows, brow, u_scr, z_dst):
        return _project_stages(x_rows, mod_ref, brow, w_in_refs, u_scr, z_dst, lay)

    operands = (qin_scr, kin_scr, kst_scr, vb_scr, dec_scr)

    def premix(z_scr):
        return _premix_stages(z_scr, w_a2_ref, b_a_ref, la_scr, *operands, lay, tile)

    def mix(z_scr, row0):
        return _mix_stages(z_scr, xc_ref, y_ref, row0, gate, gng_ref, cw_ref, w_pa_ref, w_pb_ref,
                           w_o_ref, lng_ref, lnb_ref, p_scr, s_scr, og_scr, yc_scr, mg_scr,
                           *operands, att_scr, ds_scr, sbf_scr, lay, tile, alpha)

    @pl.when(j == 0)
    def _():
        for _ in project(lambda: xc_ref[0, 0:tile, :], b_cur, ua_scr, za_scr):
            pass
        for _ in premix(za_scr):
            pass

    gate = mod_ref[pl.ds(b_cur, 1), 2 * d:3 * d]
    mix_weights = _mix_weights(lay, tile)
    premix_weights = _premix_weights(lay, tile)
    plan = _project_plan(lay)
    n_proj = 1 + len(plan)
    premix_needs_proj = 1 + sum(1 for _, _, _, d0 in plan if d0 < lay.off_cb)

    steps_per_block = SUBLANES // sample_rows
    sub = lax.rem(j, steps_per_block)
    a_row = _step_row_reader(sa_ref, sample_rows, sub, steps_per_block)
    qkv_row = _step_row_reader(sqkv_ref, sample_rows, sub, steps_per_block)
    fillers = [functools.partial(_sample_state_step, i, a_row, qkv_row, sst_ref,
                                 sso_ref, so_ref, lay) for i in range(sample_rows)]
    half = sample_rows // 2

    _emit_overlapped(mix(za_scr, 0), mix_weights, premix(zb_scr), premix_weights,
                     project(lambda: xc_ref[0, tile:2 * tile, :], b_cur, ub_scr, zb_scr), n_proj,
                     premix_needs_proj, fillers[:half])
    _emit_overlapped(mix(zb_scr, tile), mix_weights, premix(za_scr), premix_weights,
                     project(lambda: xn_ref[0], b_next, ua_scr, za_scr), n_proj,
                     premix_needs_proj, fillers[half:])

    @pl.when(jj == steps_per_seq - 1)
    def _():
        sg_ref[0] = s_scr[...]
        sc_ref[0] = p_scr[halo - (ck - 1):halo, :]


def _prompt_vmem_bytes(lay, tile, sample_rows):
    weights = 2 * (lay.d * lay.n_pad + lay.rank_pad * lay.qk + lay.vt * lay.d
                   + lay.cw * lay.d + lay.d * lay.d)
    state = lay.h * lay.dk * lay.dv * 4
    io = (2 * (2 * tile + tile + 2 * tile) * lay.d * 4 + 2 * state
          + 2 * 2 * sample_rows * state + 2 * SUBLANES * (lay.qk + lay.off_gg + lay.vt) * 4)
    chunk_heads = (tile // GLA_CHUNK) * lay.h
    scratch = (2 * tile * lay.d * 2 + 2 * tile * lay.n_pad * 4 + tile * lay.qk * 4
               + (tile + SUBLANES) * lay.cw * 4 + lay.h * lay.dk * lay.dv * 4
               + tile * lay.vt * 2 + tile * lay.cw * 2 + tile * lay.d * 2
               + 3 * tile * lay.qk * 2 + tile * lay.vt * 2 + SUBLANES * lay.qk * 4
               + lay.h * tile * LANES * 2 + chunk_heads * lay.dk * lay.dv * 6)
    temporaries = tile * lay.n_pad * 2
    return min(weights + io + scratch + temporaries, VMEM_PHYSICAL_BYTES * 15 // 16)


def _prompt_layer(x, mod_p, sample_a, sample_z, sample_state, params, prev_states, layer, lay,
                  alpha):
    (w_hd, w_lr, w_tl, w_a2, b_a, gng, conv_w, w_pa, w_pb, w_o, ln_g, ln_b) = params
    bsz, seq, d = x.shape
    depth = w_hd.shape[0]
    tile = PROMPT_TILE
    assert seq % (2 * tile) == 0 and tile % GLA_CHUNK == 0
    ck = conv_w.shape[0]
    n_t = seq // tile
    n_tiles = bsz * n_t
    n_steps = n_tiles // 2
    steps_per_seq = n_t // 2
    n_chunk_heads = (tile // GLA_CHUNK) * lay.h
    assert tile // GLA_CHUNK <= SUBLANES
    n_s = sample_state.shape[1]
    sample_rows = n_s // n_steps
    assert sample_rows * n_steps == n_s and sample_rows % 2 == 0 and SUBLANES % sample_rows == 0
    steps_per_block = SUBLANES // sample_rows

    def weight(arr):
        return _layer_spec(arr.shape[1:], layer, True)

    def next_tile(j):
        g = jnp.minimum(2 * j + 2, n_tiles - 1)
        return (g // n_t, g % n_t, 0)

    sample_state_spec = pl.BlockSpec((None, sample_rows, lay.h, lay.dk, lay.dv),
                                     lambda j: (layer, j, 0, 0, 0))
    in_specs = [
        pl.BlockSpec((1, 2 * tile, d), lambda j: (j // steps_per_seq, j % steps_per_seq, 0)),
        pl.BlockSpec((1, tile, d), next_tile),
        _layer_spec(mod_p.shape[1:], layer),
        pl.BlockSpec((SUBLANES, lay.qk), lambda j: (j // steps_per_block, 0)),
        pl.BlockSpec((SUBLANES, lay.off_gg), lambda j: (j // steps_per_block, 0)),
        sample_state_spec,
        weight(w_hd), weight(w_lr), weight(w_tl), weight(w_a2), _whole_spec(b_a),
        _whole_spec(gng), _whole_spec(conv_w), weight(w_pa), weight(w_pb), weight(w_o),
        _whole_spec(ln_g), _whole_spec(ln_b),
    ]
    args = [x, x, mod_p, sample_a, sample_z, sample_state, *params]
    aliases = {}
    if prev_states is not None:
        for i, prev in enumerate(prev_states):
            in_specs.append(pl.BlockSpec(memory_space=pl.ANY))
            aliases[len(args)] = 1 + i
            args.append(prev)

    kern = functools.partial(_prompt_kernel, lay=lay, layer=layer, tile=tile, alpha=alpha,
                             n_t=n_t, n_tiles=n_tiles, sample_rows=sample_rows)
    return pl.pallas_call(
        kern,
        grid=(n_steps,),
        in_specs=in_specs,
        out_specs=[
            pl.BlockSpec((1, 2 * tile, d), lambda j: (j // steps_per_seq, j % steps_per_seq, 0)),
            pl.BlockSpec((None, 1, lay.h, lay.dk, lay.dv),
                         lambda j: (layer, j // steps_per_seq, 0, 0, 0)),
            pl.BlockSpec((None, 1, ck - 1, lay.cw), lambda j: (layer, j // steps_per_seq, 0, 0)),
            sample_state_spec,
            pl.BlockSpec((None, sample_rows, lay.vt), lambda j: (j, 0, 0)),
        ],
        out_shape=[
            jax.ShapeDtypeStruct((bsz, seq, d), F32),
            jax.ShapeDtypeStruct((depth, bsz, lay.h, lay.dk, lay.dv), F32),
            jax.ShapeDtypeStruct((depth, bsz, ck - 1, lay.cw), F32),
            jax.ShapeDtypeStruct(sample_state.shape, F32),
            jax.ShapeDtypeStruct((n_steps, sample_rows, lay.vt), F32),
        ],
        scratch_shapes=[
            pltpu.VMEM((tile, d), BF16),
            pltpu.VMEM((tile, d), BF16),
            pltpu.VMEM((tile, lay.n_pad), F32),
            pltpu.VMEM((tile, lay.n_pad), F32),
            pltpu.VMEM((tile, lay.qk), F32),
            pltpu.VMEM((tile + SUBLANES, lay.cw), F32),
            pltpu.VMEM((lay.h, lay.dk, lay.dv), F32),
            pltpu.VMEM((tile, lay.vt), BF16),
            pltpu.VMEM((tile, lay.cw), BF16),
            pltpu.VMEM((tile, d), BF16),
            pltpu.VMEM((tile, lay.qk), BF16),
            pltpu.VMEM((tile, lay.qk), BF16),
            pltpu.VMEM((tile, lay.qk), BF16),
            pltpu.VMEM((tile, lay.vt), BF16),
            pltpu.VMEM((SUBLANES, lay.qk), F32),
            pltpu.VMEM((lay.h, tile, GLA_CHUNK), BF16),
            pltpu.VMEM((n_chunk_heads, lay.dk, lay.dv), F32),
            pltpu.VMEM((n_chunk_heads, lay.dk, lay.dv), BF16),
        ],
        input_output_aliases=aliases,
        compiler_params=pltpu.CompilerParams(
            dimension_semantics=("arbitrary",),
            vmem_limit_bytes=_prompt_vmem_bytes(lay, tile, sample_rows)),
        name="prompt_layer",
    )(*args)


def _sample_proj_kernel(x_ref, mod_ref, w_hd_ref, w_lr_ref, w_tl_ref, w_a2_ref, b_a_ref,
                        z_ref, a_ref, *, lay, layer):
    d = lay.d
    b_a_ref = _layer_rows(b_a_ref, layer)
    shift = mod_ref[:, 0:d]
    scale = mod_ref[:, d:2 * d]
    u = (x_ref[...] * (1.0 + scale) + shift).astype(BF16)
    w_in_refs = (w_hd_ref, w_lr_ref, w_tl_ref)
    for piece, s0, s1, d0 in lay.groups:
        z_ref[:, d0:d0 + s1 - s0] = jnp.dot(u, _unpack_rows(w_in_refs[piece][:, s0:s1]),
                                            preferred_element_type=F32)
    a_lr = z_ref[:, lay.off_a:lay.off_a + lay.rank_pad].astype(BF16)
    la = jnp.dot(a_lr, _unpack_rows(w_a2_ref[...]), preferred_element_type=F32) + b_a_ref[...]
    a_ref[...] = jnp.exp(_log_sigmoid(la) * (1.0 / GLA_TAU))


def _sample_proj(x3, mod_s, w_hd, w_lr, w_tl, w_a2, b_a, layer, lay):
    n, _, d = x3.shape
    kern = functools.partial(_sample_proj_kernel, lay=lay, layer=layer)
    weights = (w_hd, w_lr, w_tl, w_a2)
    vmem = 2 * d * lay.n_pad + 2 * 4 * n * lay.n_pad + 16 * MIB
    return pl.pallas_call(
        kern,
        grid=(1,),
        in_specs=[pl.BlockSpec((n, None, d), lambda i: (0, 0, 0)),
                  _layer_spec(mod_s.shape[1:], layer)]
        + [_layer_spec(w.shape[1:], layer, True) for w in weights] + [_whole_spec(b_a)],
        out_specs=[pl.BlockSpec((n, lay.n_pad), lambda i: (0, 0)),
                   pl.BlockSpec((n, lay.qk), lambda i: (0, 0))],
        out_shape=[jax.ShapeDtypeStruct((n, lay.n_pad), F32),
                   jax.ShapeDtypeStruct((n, lay.qk), F32)],
        compiler_params=pltpu.CompilerParams(dimension_semantics=("arbitrary",),
                                             vmem_limit_bytes=vmem),
        name="sample_proj",
    )(x3, mod_s, *weights, b_a)


def _sample_out_kernel(o_ref, z_ref, x_ref, mod_ref, buf_ref, gng_ref, cw_ref,
                       w_pa_ref, w_pb_ref, w_o_ref, lng_ref, lnb_ref, *rest, lay, layer, alpha):
    y_ref, buf_out_ref, og_scr = rest[-3:]
    d, h_n, dv, cw = lay.d, lay.h, lay.dv, lay.cw
    gng_ref, lng_ref, lnb_ref = (_layer_rows(r, layer) for r in (gng_ref, lng_ref, lnb_ref))
    cw_ref = _layer_taps(cw_ref, layer)
    ck = cw_ref.shape[0]
    gate = mod_ref[:, 2 * d:3 * d]
    gng = gng_ref[...]
    for h in range(h_n):
        o = o_ref[:, h * dv:(h + 1) * dv]
        ms = jnp.mean(o * o, axis=-1, keepdims=True)
        on = o * lax.rsqrt(ms + EPS) * gng
        gg = z_ref[:, lay.off_gg + h * dv:lay.off_gg + (h + 1) * dv]
        og_scr[:, h * dv:(h + 1) * dv] = (on * _silu(gg)).astype(BF16)
    pa = jnp.dot(og_scr[...], _unpack_rows(w_pa_ref[...]), preferred_element_type=F32)

    p = z_ref[:, lay.off_cc:lay.off_cc + cw] * z_ref[:, lay.off_ch:lay.off_ch + cw]
    yc = cw_ref[ck - 1:ck, :] * p
    for i in range(ck - 1):
        yc = yc + cw_ref[i:i + 1, :] * buf_ref[:, i * cw:(i + 1) * cw]
    yc = z_ref[:, lay.off_cb:lay.off_cb + cw] * yc * _silu(z_ref[:, lay.off_gc:lay.off_gc + cw])
    pb = jnp.dot(yc.astype(BF16), _unpack_rows(w_pb_ref[...]), preferred_element_type=F32)
    for i in range(ck - 2):
        buf_out_ref[:, i * cw:(i + 1) * cw] = buf_ref[:, (i + 1) * cw:(i + 2) * cw]
    buf_out_ref[:, (ck - 2) * cw:(ck - 1) * cw] = p

    merged = (jax.nn.sigmoid(z_ref[:, lay.off_mg:lay.off_mg + d]) * pa
              + jax.nn.sigmoid(z_ref[:, lay.off_mc:lay.off_mc + d]) * pb)
    out = jnp.dot(merged.astype(BF16), _unpack_rows(w_o_ref[...]),
                  preferred_element_type=F32)
    hres = alpha * x_ref[...] + gate * out
    mu = jnp.mean(hres, axis=-1, keepdims=True)
    dev = hres - mu
    var = jnp.mean(dev * dev, axis=-1, keepdims=True)
    y_ref[...] = dev * lax.rsqrt(var + EPS) * lng_ref[...] + lnb_ref[...]


def _sample_out(o, z, x3, mod_s, conv_state, prev_conv_out, gng, conv_w, w_pa, w_pb, w_o,
                ln_g, ln_b, layer, lay, alpha):
    n, _, d = x3.shape
    kern = functools.partial(_sample_out_kernel, lay=lay, layer=layer, alpha=alpha)
    conv_spec = _layer_spec(conv_state.shape[1:], layer)
    in_specs = [_whole_spec(o), _whole_spec(z), pl.BlockSpec((n, None, d), lambda i: (0, 0, 0)),
                _layer_spec(mod_s.shape[1:], layer), conv_spec,
                _whole_spec(gng), _whole_spec(conv_w), _layer_spec(w_pa.shape[1:], layer),
                _layer_spec(w_pb.shape[1:], layer), _layer_spec(w_o.shape[1:], layer),
                _whole_spec(ln_g), _whole_spec(ln_b)]
    args = [o, z, x3, mod_s, conv_state, gng, conv_w, w_pa, w_pb, w_o, ln_g, ln_b]
    aliases = {}
    if prev_conv_out is not None:
        in_specs.append(pl.BlockSpec(memory_space=pl.ANY))
        aliases = {len(args): 1}
        args.append(prev_conv_out)
    return pl.pallas_call(
        kern,
        grid=(1,),
        in_specs=in_specs,
        out_specs=[pl.BlockSpec((n, None, d), lambda i: (0, 0, 0)), conv_spec],
        out_shape=[jax.ShapeDtypeStruct((n, 1, d), F32),
                   jax.ShapeDtypeStruct(conv_state.shape, F32)],
        scratch_shapes=[pltpu.VMEM((n, lay.vt), BF16)],
        input_output_aliases=aliases,
        compiler_params=pltpu.CompilerParams(dimension_semantics=("arbitrary",),
                                             vmem_limit_bytes=48 * MIB),
        name="sample_out",
    )(*args)


def kernel(x_prompt, x_sample, c_prompt, c_sample, state_gla, state_conv, w_ada, b_ada, w_in,
           w_a2, b_a, gla_norm_g, conv_w, w_pa, w_pb, w_o, ln_g, ln_b):
    depth, n_s, heads, dk, dv = state_gla.shape
    d = x_prompt.shape[2]
    rank = w_a2.shape[1]
    ck, cw = conv_w.shape[1:]
    lay = _Layout(d, heads, dk, dv, rank, cw)
    alpha = (2 * depth) ** 0.25

    mod_p, mod_s = _ada_mod(c_prompt, c_sample, w_ada, b_ada)

    w_hd, w_lr, w_tl = _split_w_in(w_in, lay)
    w_a2_p = _cast_bf16(jnp.concatenate(
        [w_a2, jnp.zeros((depth, lay.rank_pad - rank, lay.qk), w_a2.dtype)], axis=1))
    w_pa_b, w_pb_b, w_o_b = _cast_bf16(w_pa), _cast_bf16(w_pb), _cast_bf16(w_o)
    taps = jnp.swapaxes(conv_w, 0, 1)
    params = (w_hd, w_lr, w_tl, w_a2_p, b_a, gla_norm_g, taps, w_pa_b, w_pb_b, w_o_b, ln_g, ln_b)

    hp, hs = x_prompt, x_sample
    states = None
    conv_in = state_conv.reshape(depth, n_s, (ck - 1) * cw)
    conv_s = None
    for l in range(depth):
        z, a = _sample_proj(hs, mod_s, w_hd, w_lr, w_tl, w_a2_p, b_a, l, lay)
        hp, *states, o = _prompt_layer(hp, mod_p, a, z, state_gla, params, states, l, lay, alpha)
        hs, conv_s = _sample_out(o.reshape(n_s, lay.vt), z, hs, mod_s, conv_in, conv_s,
                                 gla_norm_g, taps, w_pa_b, w_pb_b, w_o_b, ln_g, ln_b, l, lay,
                                 alpha)
    return (hp, hs, states[0], states[1], states[2], conv_s.reshape(state_conv.shape))
```

```python
import functools

import jax
import jax.numpy as jnp
from jax import lax
from jax.experimental import pallas as pl
from jax.experimental.pallas import tpu as pltpu

F32 = jnp.float32
BF16 = jnp.bfloat16

GLA_TAU = 16.0
GLA_CHUNK = 64
EPS = 1e-5

LANES = 128
SUBLANES = 8
VMEM_PHYSICAL_BYTES = 64 * 1024 * 1024
MIB = 1024 * 1024

PROMPT_TILE = 256
PROJ_COLS = 256
PREP_STEPS = 2
ADA_STEPS = 2


LOG2_E = 1.4426950408889634


def _log_sigmoid(x):
    return jnp.minimum(x, 0.0) - jnp.log(1.0 + jnp.exp(-jnp.abs(x)))


def _cumsum_rows(x):
    row = lax.broadcasted_iota(jnp.int32, (SUBLANES, x.shape[1]), 0)
    groups = []
    carry = None
    for g in range(x.shape[0] // SUBLANES):
        t = x[g * SUBLANES:(g + 1) * SUBLANES, :]
        s = 1
        while s < SUBLANES:
            t = t + jnp.where(row >= s, pltpu.roll(t, s, 0), 0.0)
            s *= 2
        if carry is not None:
            t = t + carry
        carry = jnp.broadcast_to(t[SUBLANES - 1:SUBLANES, :], t.shape)
        groups.append(t)
    return jnp.concatenate(groups, axis=0)


def _silu(x):
    return x * jax.nn.sigmoid(x)


PACKED = jnp.uint32
PACK = 2


def _pack_rows(w_bf16):
    return pltpu.bitcast(w_bf16, PACKED)


def _unpack_rows(w_packed):
    return pltpu.bitcast(w_packed, BF16)


class _Layout:
    def __init__(self, d_model, heads, dk, dv, rank, conv_w):
        self.d = d_model
        self.h = heads
        self.dk = dk
        self.dv = dv
        self.rank = rank
        self.cw = conv_w
        self.qk = heads * dk
        self.vt = heads * dv
        self.rank_pad = -(-rank // LANES) * LANES
        self.off_q = 0
        self.off_k = self.off_q + self.qk
        self.off_v = self.off_k + self.qk
        self.off_gg = self.off_v + self.vt
        self.off_a = self.off_gg + self.vt
        self.off_cb = self.off_a + self.rank_pad
        self.off_cc = self.off_cb + conv_w
        self.off_ch = self.off_cc + conv_w
        self.off_gc = self.off_ch + conv_w
        self.off_mg = self.off_gc + conv_w
        self.off_mc = self.off_mg + d_model
        self.n_pad = self.off_mc + d_model
        self.n_head = self.off_a
        self.n_tail = self.n_pad - self.off_cb
        self.groups = (
            (0, self.off_q, self.off_v, self.off_q),
            (0, self.off_v, self.off_gg, self.off_v),
            (0, self.off_gg, self.off_a, self.off_gg),
            (1, 0, self.rank_pad, self.off_a),
            (2, 0, 2 * conv_w, self.off_cb),
            (2, 2 * conv_w, 4 * conv_w, self.off_ch),
            (2, 4 * conv_w, 4 * conv_w + d_model, self.off_mg),
            (2, 4 * conv_w + d_model, self.n_tail, self.off_mc))


def _layer_spec(shape, layer, single_buffer=False):
    zeros = (0,) * len(shape)
    kwargs = {"pipeline_mode": pl.Buffered(1)} if single_buffer else {}
    return pl.BlockSpec((None,) + tuple(shape), lambda *_: (layer,) + zeros, **kwargs)


def _whole_spec(arr):
    return pl.BlockSpec(arr.shape, lambda *_: (0,) * arr.ndim)


def _layer_rows(ref, layer):
    return ref.at[pl.ds(layer, 1)]


def _layer_taps(ref, layer):
    return ref.at[:, layer]


def _transpose_cast_kernel(w_ref, o_ref, *, valid):
    wt = jnp.transpose(w_ref[0])
    if valid < wt.shape[1]:
        lane = lax.broadcasted_iota(jnp.int32, wt.shape, 1)
        wt = jnp.where(lane < valid, wt, 0.0)
    o_ref[...] = _pack_rows(wt.astype(BF16))


def _transpose_cast(w_t, col0, n_valid, n_out, cols):
    depth, n_in, d = w_t.shape
    assert n_out % cols == 0 and col0 % SUBLANES == 0 and col0 + n_out <= n_in
    kern = functools.partial(_transpose_cast_kernel, valid=min(n_valid, cols))
    return pl.pallas_call(
        kern,
        grid=(depth, n_out // cols),
        in_specs=[pl.BlockSpec((pl.Element(1), pl.Element(cols), pl.Element(d)),
                               lambda l, j: (l, pl.multiple_of(col0 + j * cols, SUBLANES), 0))],
        out_specs=pl.BlockSpec((None, d // PACK, cols), lambda l, j: (l, 0, j)),
        out_shape=jax.ShapeDtypeStruct((depth, d // PACK, n_out), PACKED),
        compiler_params=pltpu.CompilerParams(
            dimension_semantics=("arbitrary", "arbitrary"),
            vmem_limit_bytes=cols * d * (2 * 4 + 2 * 2 + 4 + 2) + 8 * MIB),
        name="transpose_cast",
    )(w_t)


def _split_w_in(w_in, lay):
    n_in = w_in.shape[2]
    a_end = lay.off_a + lay.rank
    assert n_in == a_end + lay.n_tail and lay.rank <= lay.rank_pad == LANES
    w_t = jnp.swapaxes(w_in, 1, 2)
    head = _transpose_cast(w_t, 0, lay.n_head, lay.n_head, lay.n_head // PREP_STEPS)
    low = _transpose_cast(w_t, lay.off_a, lay.rank, lay.rank_pad, lay.rank_pad)
    tail = _transpose_cast(w_t, a_end, lay.n_tail, lay.n_tail, lay.n_tail // PREP_STEPS)
    return head, low, tail


def _cast_kernel(w_ref, o_ref):
    o_ref[...] = _pack_rows(w_ref[...].astype(BF16))


def _cast_bf16(w):
    depth, rows, cols = w.shape
    return pl.pallas_call(
        _cast_kernel,
        grid=(depth,),
        in_specs=[pl.BlockSpec((None, rows, cols), lambda l: (l, 0, 0))],
        out_specs=pl.BlockSpec((None, rows // PACK, cols), lambda l: (l, 0, 0)),
        out_shape=jax.ShapeDtypeStruct((depth, rows // PACK, cols), PACKED),
        compiler_params=pltpu.CompilerParams(
            dimension_semantics=("arbitrary",),
            vmem_limit_bytes=2 * rows * cols * 6 + 16 * MIB),
        name="cast_bf16",
    )(w)


def _ada_kernel(cp_ref, cs_ref, w_ref, b_ref, op_ref, os_ref):
    w = w_ref[...].astype(BF16)
    b = b_ref[...]
    op_ref[...] = jnp.dot(cp_ref[...].astype(BF16), w, preferred_element_type=F32) + b
    os_ref[...] = jnp.dot(cs_ref[...].astype(BF16), w, preferred_element_type=F32) + b


def _ada_mod(c_prompt, c_sample, w_ada, b_ada):
    depth, d, n3 = w_ada.shape
    bp, ns = c_prompt.shape[0], c_sample.shape[0]
    tn = n3 // ADA_STEPS
    return pl.pallas_call(
        _ada_kernel,
        grid=(depth, n3 // tn),
        in_specs=[
            pl.BlockSpec((bp, d), lambda l, j: (0, 0)),
            pl.BlockSpec((ns, d), lambda l, j: (0, 0)),
            pl.BlockSpec((None, d, tn), lambda l, j: (l, 0, j)),
            pl.BlockSpec((None, 1, tn), lambda l, j: (l, 0, j)),
        ],
        out_specs=[pl.BlockSpec((None, bp, tn), lambda l, j: (l, 0, j)),
                   pl.BlockSpec((None, ns, tn), lambda l, j: (l, 0, j))],
        out_shape=[jax.ShapeDtypeStruct((depth, bp, n3), F32),
                   jax.ShapeDtypeStruct((depth, ns, n3), F32)],
        compiler_params=pltpu.CompilerParams(
            dimension_semantics=("arbitrary", "arbitrary"),
            vmem_limit_bytes=d * tn * (2 * 4 + 2) + 8 * MIB),
        name="ada_mod",
    )(c_prompt, c_sample, w_ada, b_ada.reshape(depth, 1, n3))


def _project_plan(lay):
    plan = []
    for piece, s0, s1, d0 in lay.groups:
        for c0 in range(s0, s1, PROJ_COLS):
            plan.append((piece, c0, min(c0 + PROJ_COLS, s1), d0 + c0 - s0))
    return plan


def _project_stages(x_rows, mod_ref, brow, w_in_refs, u_scr, z_dst, lay):
    d = lay.d
    shift = mod_ref[pl.ds(brow, 1), 0:d]
    scale = mod_ref[pl.ds(brow, 1), d:2 * d]
    u_scr[...] = (x_rows() * (1.0 + scale) + shift).astype(BF16)
    yield
    for piece, c0, c1, d0 in _project_plan(lay):
        z_dst[:, d0:d0 + c1 - c0] = jnp.dot(u_scr[...],
                                            _unpack_rows(w_in_refs[piece][:, c0:c1]),
                                            preferred_element_type=F32)
        yield


def _tile_chunks(tile):
    return [(c, c * GLA_CHUNK, (c + 1) * GLA_CHUNK) for c in range(tile // GLA_CHUNK)]


def _tile_heads(lay):
    return [(h, slice(h * lay.dk, (h + 1) * lay.dk), slice(h * lay.dv, (h + 1) * lay.dv))
            for h in range(lay.h)]


W_DECAY, W_FACTORS, W_SCORES, W_INCR, W_RECUR, W_OUT, W_NORM, W_CONV, W_MERGE, W_LN = (
    4, 4, 1, 1, 5, 1, 11, 10, 10, 10)


def _premix_weights(lay, tile):
    return [W_DECAY] + [W_FACTORS] * (len(_tile_chunks(tile)) * lay.h)


def _premix_stages(z_scr, w_a2_ref, b_a_ref, la_scr, qin_scr, kin_scr, kst_scr, vb_scr, dec_scr,
                   lay, tile):
    dk, dv = lay.dk, lay.dv
    q_scale = dk ** -0.5

    a_lr = z_scr[:, lay.off_a:lay.off_a + lay.rank_pad].astype(BF16)
    la_scr[...] = (jnp.dot(a_lr, _unpack_rows(w_a2_ref[...]), preferred_element_type=F32)
                   + b_a_ref[...])
    yield W_DECAY

    for c, r0, r1 in _tile_chunks(tile):
        for h, ks, vs in _tile_heads(lay):
            log_a = _log_sigmoid(la_scr[r0:r1, ks]) * (1.0 / GLA_TAU)
            b2 = _cumsum_rows(log_a) * LOG2_E
            bl2 = b2[GLA_CHUNK - 1:GLA_CHUNK, :]
            q_in = z_scr[r0:r1, lay.off_q + h * dk:lay.off_q + (h + 1) * dk] * q_scale
            k = z_scr[r0:r1, lay.off_k + h * dk:lay.off_k + (h + 1) * dk]
            qin_scr[r0:r1, ks] = (q_in * jnp.exp2(b2)).astype(BF16)
            kin_scr[r0:r1, ks] = (k * jnp.exp2(-b2)).astype(BF16)
            kst_scr[r0:r1, ks] = (k * jnp.exp2(bl2 - b2)).astype(BF16)
            dec_scr[c:c + 1, ks] = jnp.exp2(bl2)
            vb_scr[r0:r1, vs] = z_scr[r0:r1, lay.off_v + h * dv:lay.off_v + (h + 1) * dv] \
                .astype(BF16)
            yield W_FACTORS


def _mix_weights(lay, tile):
    chunks = len(_tile_chunks(tile))
    head = ([W_SCORES] * chunks + [W_INCR] * lay.h + [W_RECUR] * lay.h + [W_OUT] * chunks
            + [W_NORM] * chunks + [W_CONV] * chunks)
    tail = [0, 0] + [W_MERGE] * chunks + [0] + [W_LN] * chunks
    return head, tail


def _mix_stages(z_scr, x_ref, y_ref, row0, gate, gng_ref, cw_ref,
                w_pa_ref, w_pb_ref, w_o_ref, lng_ref, lnb_ref,
                p_scr, s_scr, og_scr, yc_scr, mg_scr,
                qin_scr, kin_scr, kst_scr, vb_scr, dec_scr, att_scr, ds_scr, sbf_scr,
                lay, tile, alpha):
    d, h_n, dk, dv, cw = lay.d, lay.h, lay.dk, lay.dv, lay.cw
    ck = cw_ref.shape[0]
    halo = SUBLANES
    chunks = _tile_chunks(tile)
    heads = _tile_heads(lay)
    causal = (lax.broadcasted_iota(jnp.int32, (GLA_CHUNK, GLA_CHUNK), 0)
              >= lax.broadcasted_iota(jnp.int32, (GLA_CHUNK, GLA_CHUNK), 1))
    gng = gng_ref[...]

    for c, r0, r1 in chunks:
        for h, ks, vs in heads:
            att = lax.dot_general(qin_scr[r0:r1, ks], kin_scr[r0:r1, ks],
                                  (((1,), (1,)), ((), ())), preferred_element_type=F32)
            att_scr[h, r0:r1, :] = jnp.where(causal, att, 0.0).astype(BF16)
        yield W_SCORES

    for h, ks, vs in heads:
        for c, r0, r1 in chunks:
            ds_scr[c * h_n + h] = lax.dot_general(kst_scr[r0:r1, ks], vb_scr[r0:r1, vs],
                                                  (((0,), (0,)), ((), ())),
                                                  preferred_element_type=F32)
        yield W_INCR
    for h, ks, vs in heads:
        state = s_scr[h]
        for c, r0, r1 in chunks:
            sbf_scr[c * h_n + h] = state.astype(BF16)
            dec = jnp.transpose(jnp.broadcast_to(dec_scr[c:c + 1, ks], (dk, dk)))
            dec = jnp.concatenate([dec] * (dv // dk), axis=1)
            state = dec * state + ds_scr[c * h_n + h]
        s_scr[h] = state
        yield W_RECUR

    for c, r0, r1 in chunks:
        for h, ks, vs in heads:
            lhs = jnp.concatenate([qin_scr[r0:r1, ks], att_scr[h, r0:r1, :]], axis=1)
            rhs = jnp.concatenate([sbf_scr[c * h_n + h], vb_scr[r0:r1, vs]], axis=0)
            z_scr[r0:r1, vs] = jnp.dot(lhs, rhs, preferred_element_type=F32)
        yield W_OUT

    for c, r0, r1 in chunks:
        for h, ks, vs in heads:
            o = z_scr[r0:r1, vs]
            ms = jnp.mean(o * o, axis=-1, keepdims=True)
            on = o * lax.rsqrt(ms + EPS) * gng
            gg = z_scr[r0:r1, lay.off_gg + h * dv:lay.off_gg + (h + 1) * dv]
            og_scr[r0:r1, vs] = (on * _silu(gg)).astype(BF16)
        yield W_NORM

    for c, r0, r1 in chunks:
        for g0 in range(0, cw, LANES):
            gl = slice(g0, g0 + LANES)
            p = (z_scr[r0:r1, lay.off_cc + g0:lay.off_cc + g0 + LANES]
                 * z_scr[r0:r1, lay.off_ch + g0:lay.off_ch + g0 + LANES])
            p_scr[halo + r0:halo + r1, gl] = p
            ext = jnp.concatenate([p_scr[r0:halo + r0, gl], p], axis=0)
            yc = cw_ref[ck - 1:ck, gl] * p
            for i in range(ck - 1):
                lag = ck - 1 - i
                yc = yc + cw_ref[i:i + 1, gl] * pltpu.roll(ext, lag, 0)[halo:, :]
            yc = (z_scr[r0:r1, lay.off_cb + g0:lay.off_cb + g0 + LANES] * yc
                  * _silu(z_scr[r0:r1, lay.off_gc + g0:lay.off_gc + g0 + LANES]))
            yc_scr[r0:r1, gl] = yc.astype(BF16)
        yield W_CONV

    p_scr[0:halo, :] = p_scr[tile:tile + halo, :]

    z_scr[:, 0:d] = jnp.dot(og_scr[...], _unpack_rows(w_pa_ref[...]), preferred_element_type=F32)
    yield 0
    z_scr[:, d:2 * d] = jnp.dot(yc_scr[...], _unpack_rows(w_pb_ref[...]),
                                preferred_element_type=F32)
    yield 0
    for c, r0, r1 in chunks:
        for g0 in range(0, d, 2 * LANES):
            g1 = g0 + 2 * LANES
            merged = (jax.nn.sigmoid(z_scr[r0:r1, lay.off_mg + g0:lay.off_mg + g1])
                      * z_scr[r0:r1, g0:g1]
                      + jax.nn.sigmoid(z_scr[r0:r1, lay.off_mc + g0:lay.off_mc + g1])
                      * z_scr[r0:r1, d + g0:d + g1])
            mg_scr[r0:r1, g0:g1] = merged.astype(BF16)
        yield W_MERGE
    z_scr[:, 2 * d:3 * d] = jnp.dot(mg_scr[...], _unpack_rows(w_o_ref[...]),
                                    preferred_element_type=F32)
    yield 0

    lng = lng_ref[...]
    lnb = lnb_ref[...]
    for c, r0, r1 in chunks:
        hres = alpha * x_ref[0, row0 + r0:row0 + r1, :] + gate * z_scr[r0:r1, 2 * d:3 * d]
        mu = jnp.mean(hres, axis=-1, keepdims=True)
        dev = hres - mu
        var = jnp.mean(dev * dev, axis=-1, keepdims=True)
        y_ref[0, row0 + r0:row0 + r1, :] = dev * lax.rsqrt(var + EPS) * lng + lnb
        yield W_LN


def _emit_overlapped(mix, mix_weights, premix, premix_weights, proj, n_proj, premix_needs_proj,
                     fillers):
    head, tail = mix_weights
    order = [("mix", w) for w in head]
    done = 0
    for i, w in enumerate(tail):
        while done * len(tail) < i * len(premix_weights):
            order.append(("premix", premix_weights[done]))
            done += 1
        order.append(("mix", w))
    order += [("premix", w) for w in premix_weights[done:]]

    total = sum(w for _, w in order)
    matmul_slots = sum(1 for _, w in order if w == 0)
    done_w = done_p = done_f = slot = 0
    for source, weight in order:
        done_w += weight
        while done_p * total < done_w * n_proj or (source == "premix"
                                                    and done_p < premix_needs_proj):
            next(proj)
            done_p += 1
        if weight == 0:
            slot += 1
            while done_f * matmul_slots < slot * len(fillers):
                fillers[done_f]()
                done_f += 1
        assert next(mix if source == "mix" else premix) == weight
    while done_p < n_proj:
        next(proj)
        done_p += 1
    assert done_f == len(fillers) and next(mix, None) is None and next(premix, None) is None


def _step_row_reader(ref, rows, sub, steps_per_block):
    def read(i, lanes):
        val = ref[i:i + 1, lanes]
        for s in range(1, steps_per_block):
            val = jnp.where(sub == s, ref[s * rows + i:s * rows + i + 1, lanes], val)
        return val
    return read


def _sample_state_step(i, a_row, qkv_row, s_ref, s_out_ref, o_ref, lay):
    h_n, dk, dv = lay.h, lay.dk, lay.dv
    q_scale = dk ** -0.5
    rows = [a_row(i, slice(h * dk, (h + 1) * dk)) for h in range(h_n)]
    rows += [qkv_row(i, slice(lay.off_k + h * dk, lay.off_k + (h + 1) * dk)) for h in range(h_n)]
    rows += [qkv_row(i, slice(lay.off_q + h * dk, lay.off_q + (h + 1) * dk)) for h in range(h_n)]
    rows.append(jnp.zeros((dk - len(rows), dk), F32))
    mt = jnp.transpose(jnp.concatenate(rows, axis=0))
    for h in range(h_n):
        a_col = mt[:, h:h + 1]
        k_col = mt[:, h_n + h:h_n + h + 1]
        q_col = mt[:, 2 * h_n + h:2 * h_n + h + 1]
        v = qkv_row(i, slice(lay.off_v + h * dv, lay.off_v + (h + 1) * dv))
        s_new = a_col * s_ref[i, h] + k_col * v
        s_out_ref[i, h] = s_new
        o_ref[i:i + 1, h * dv:(h + 1) * dv] = q_scale * jnp.sum(q_col * s_new, axis=0,
                                                               keepdims=True)


def _prompt_kernel(xc_ref, xn_ref, mod_ref, sa_ref, sqkv_ref, sst_ref,
                   w_hd_ref, w_lr_ref, w_tl_ref, w_a2_ref, b_a_ref,
                   gng_ref, cw_ref, w_pa_ref, w_pb_ref, w_o_ref, lng_ref, lnb_ref, *rest,
                   lay, layer, tile, alpha, n_t, n_tiles, sample_rows):
    (y_ref, sg_ref, sc_ref, sso_ref, so_ref,
     ua_scr, ub_scr, za_scr, zb_scr, la_scr, p_scr, s_scr, og_scr, yc_scr, mg_scr,
     qin_scr, kin_scr, kst_scr, vb_scr, dec_scr, att_scr, ds_scr, sbf_scr) = rest[-23:]
    b_a_ref, gng_ref, lng_ref, lnb_ref = (_layer_rows(r, layer)
                                          for r in (b_a_ref, gng_ref, lng_ref, lnb_ref))
    cw_ref = _layer_taps(cw_ref, layer)
    j = pl.program_id(0)
    steps_per_seq = n_t // 2
    jj = lax.rem(j, steps_per_seq)
    b_cur = lax.div(j, steps_per_seq)
    b_next = lax.div(jnp.minimum(2 * j + 2, n_tiles - 1), n_t)
    d = lay.d
    ck = cw_ref.shape[0]
    halo = SUBLANES
    w_in_refs = (w_hd_ref, w_lr_ref, w_tl_ref)

    @pl.when(jj == 0)
    def _():
        s_scr[...] = jnp.zeros_like(s_scr)
        p_scr[0:halo, :] = jnp.zeros((halo, lay.cw), F32)

    def project(x_rows, brow, u_scr, z_dst):
        return _project_stages(x_rows, mod_ref, brow, w_in_refs, u_scr, z_dst, lay)

    operands = (qin_scr, kin_scr, kst_scr, vb_scr, dec_scr)

    def premix(z_scr):
        return _premix_stages(z_scr, w_a2_ref, b_a_ref, la_scr, *operands, lay, tile)

    def mix(z_scr, row0):
        return _mix_stages(z_scr, xc_ref, y_ref, row0, gate, gng_ref, cw_ref, w_pa_ref, w_pb_ref,
                           w_o_ref, lng_ref, lnb_ref, p_scr, s_scr, og_scr, yc_scr, mg_scr,
                           *operands, att_scr, ds_scr, sbf_scr, lay, tile, alpha)

    @pl.when(j == 0)
    def _():
        for _ in project(lambda: xc_ref[0, 0:tile, :], b_cur, ua_scr, za_scr):
            pass
        for _ in premix(za_scr):
            pass

    gate = mod_ref[pl.ds(b_cur, 1), 2 * d:3 * d]
    mix_weights = _mix_weights(lay, tile)
    premix_weights = _premix_weights(lay, tile)
    plan = _project_plan(lay)
    n_proj = 1 + len(plan)
    premix_needs_proj = 1 + sum(1 for _, _, _, d0 in plan if d0 < lay.off_cb)

    steps_per_block = SUBLANES // sample_rows
    sub = lax.rem(j, steps_per_block)
    a_row = _step_row_reader(sa_ref, sample_rows, sub, steps_per_block)
    qkv_row = _step_row_reader(sqkv_ref, sample_rows, sub, steps_per_block)
    fillers = [functools.partial(_sample_state_step, i, a_row, qkv_row, sst_ref,
                                 sso_ref, so_ref, lay) for i in range(sample_rows)]
    half = sample_rows // 2

    _emit_overlapped(mix(za_scr, 0), mix_weights, premix(zb_scr), premix_weights,
                     project(lambda: xc_ref[0, tile:2 * tile, :], b_cur, ub_scr, zb_scr), n_proj,
                     premix_needs_proj, fillers[:half])
    _emit_overlapped(mix(zb_scr, tile), mix_weights, premix(za_scr), premix_weights,
                     project(lambda: xn_ref[0], b_next, ua_scr, za_scr), n_proj,
                     premix_needs_proj, fillers[half:])

    @pl.when(jj == steps_per_seq - 1)
    def _():
        sg_ref[0] = s_scr[...]
        sc_ref[0] = p_scr[halo - (ck - 1):halo, :]


def _prompt_vmem_bytes(lay, tile, sample_rows):
    weights = 2 * (lay.d * lay.n_pad + lay.rank_pad * lay.qk + lay.vt * lay.d
                   + lay.cw * lay.d + lay.d * lay.d)
    state = lay.h * lay.dk * lay.dv * 4
    io = (2 * (2 * tile + tile + 2 * tile) * lay.d * 4 + 2 * state
          + 2 * 2 * sample_rows * state + 2 * SUBLANES * (lay.qk + lay.off_gg + lay.vt) * 4)
    chunk_heads = (tile // GLA_CHUNK) * lay.h
    scratch = (2 * tile * lay.d * 2 + 2 * tile * lay.n_pad * 4 + tile * lay.qk * 4
               + (tile + SUBLANES) * lay.cw * 4 + lay.h * lay.dk * lay.dv * 4
               + tile * lay.vt * 2 + tile * lay.cw * 2 + tile * lay.d * 2
               + 3 * tile * lay.qk * 2 + tile * lay.vt * 2 + SUBLANES * lay.qk * 4
               + lay.h * tile * LANES * 2 + chunk_heads * lay.dk * lay.dv * 6)
    temporaries = tile * lay.n_pad * 2
    return min(weights + io + scratch + temporaries, VMEM_PHYSICAL_BYTES * 15 // 16)


def _prompt_layer(x, mod_p, sample_a, sample_z, sample_state, params, prev_states, layer, lay,
                  alpha):
    (w_hd, w_lr, w_tl, w_a2, b_a, gng, conv_w, w_pa, w_pb, w_o, ln_g, ln_b) = params
    bsz, seq, d = x.shape
    depth = w_hd.shape[0]
    tile = PROMPT_TILE
    assert seq % (2 * tile) == 0 and tile % GLA_CHUNK == 0
    ck = conv_w.shape[0]
    n_t = seq // tile
    n_tiles = bsz * n_t
    n_steps = n_tiles // 2
    steps_per_seq = n_t // 2
    n_chunk_heads = (tile // GLA_CHUNK) * lay.h
    assert tile // GLA_CHUNK <= SUBLANES
    n_s = sample_state.shape[1]
    sample_rows = n_s // n_steps
    assert sample_rows * n_steps == n_s and sample_rows % 2 == 0 and SUBLANES % sample_rows == 0
    steps_per_block = SUBLANES // sample_rows

    def weight(arr):
        return _layer_spec(arr.shape[1:], layer, True)

    def next_tile(j):
        g = jnp.minimum(2 * j + 2, n_tiles - 1)
        return (g // n_t, g % n_t, 0)

    sample_state_spec = pl.BlockSpec((None, sample_rows, lay.h, lay.dk, lay.dv),
                                     lambda j: (layer, j, 0, 0, 0))
    in_specs = [
        pl.BlockSpec((1, 2 * tile, d), lambda j: (j // steps_per_seq, j % steps_per_seq, 0)),
        pl.BlockSpec((1, tile, d), next_tile),
        _layer_spec(mod_p.shape[1:], layer),
        pl.BlockSpec((SUBLANES, lay.qk), lambda j: (j // steps_per_block, 0)),
        pl.BlockSpec((SUBLANES, lay.off_gg), lambda j: (j // steps_per_block, 0)),
        sample_state_spec,
        weight(w_hd), weight(w_lr), weight(w_tl), weight(w_a2), _whole_spec(b_a),
        _whole_spec(gng), _whole_spec(conv_w), weight(w_pa), weight(w_pb), weight(w_o),
        _whole_spec(ln_g), _whole_spec(ln_b),
    ]
    args = [x, x, mod_p, sample_a, sample_z, sample_state, *params]
    aliases = {}
    if prev_states is not None:
        for i, prev in enumerate(prev_states):
            in_specs.append(pl.BlockSpec(memory_space=pl.ANY))
            aliases[len(args)] = 1 + i
            args.append(prev)

    kern = functools.partial(_prompt_kernel, lay=lay, layer=layer, tile=tile, alpha=alpha,
                             n_t=n_t, n_tiles=n_tiles, sample_rows=sample_rows)
    return pl.pallas_call(
        kern,
        grid=(n_steps,),
        in_specs=in_specs,
        out_specs=[
            pl.BlockSpec((1, 2 * tile, d), lambda j: (j // steps_per_seq, j % steps_per_seq, 0)),
            pl.BlockSpec((None, 1, lay.h, lay.dk, lay.dv),
                         lambda j: (layer, j // steps_per_seq, 0, 0, 0)),
            pl.BlockSpec((None, 1, ck - 1, lay.cw), lambda j: (layer, j // steps_per_seq, 0, 0)),
            sample_state_spec,
            pl.BlockSpec((None, sample_rows, lay.vt), lambda j: (j, 0, 0)),
        ],
        out_shape=[
            jax.ShapeDtypeStruct((bsz, seq, d), F32),
            jax.ShapeDtypeStruct((depth, bsz, lay.h, lay.dk, lay.dv), F32),
            jax.ShapeDtypeStruct((depth, bsz, ck - 1, lay.cw), F32),
            jax.ShapeDtypeStruct(sample_state.shape, F32),
            jax.ShapeDtypeStruct((n_steps, sample_rows, lay.vt), F32),
        ],
        scratch_shapes=[
            pltpu.VMEM((tile, d), BF16),
            pltpu.VMEM((tile, d), BF16),
            pltpu.VMEM((tile, lay.n_pad), F32),
            pltpu.VMEM((tile, lay.n_pad), F32),
            pltpu.VMEM((tile, lay.qk), F32),
            pltpu.VMEM((tile + SUBLANES, lay.cw), F32),
            pltpu.VMEM((lay.h, lay.dk, lay.dv), F32),
            pltpu.VMEM((tile, lay.vt), BF16),
            pltpu.VMEM((tile, lay.cw), BF16),
            pltpu.VMEM((tile, d), BF16),
            pltpu.VMEM((tile, lay.qk), BF16),
            pltpu.VMEM((tile, lay.qk), BF16),
            pltpu.VMEM((tile, lay.qk), BF16),
            pltpu.VMEM((tile, lay.vt), BF16),
            pltpu.VMEM((SUBLANES, lay.qk), F32),
            pltpu.VMEM((lay.h, tile, GLA_CHUNK), BF16),
            pltpu.VMEM((n_chunk_heads, lay.dk, lay.dv), F32),
            pltpu.VMEM((n_chunk_heads, lay.dk, lay.dv), BF16),
        ],
        input_output_aliases=aliases,
        compiler_params=pltpu.CompilerParams(
            dimension_semantics=("arbitrary",),
            vmem_limit_bytes=_prompt_vmem_bytes(lay, tile, sample_rows)),
        name="prompt_layer",
    )(*args)


def _sample_proj_kernel(x_ref, mod_ref, w_hd_ref, w_lr_ref, w_tl_ref, w_a2_ref, b_a_ref,
                        z_ref, a_ref, *, lay, layer):
    d = lay.d
    b_a_ref = _layer_rows(b_a_ref, layer)
    shift = mod_ref[:, 0:d]
    scale = mod_ref[:, d:2 * d]
    u = (x_ref[...] * (1.0 + scale) + shift).astype(BF16)
    w_in_refs = (w_hd_ref, w_lr_ref, w_tl_ref)
    for piece, s0, s1, d0 in lay.groups:
        z_ref[:, d0:d0 + s1 - s0] = jnp.dot(u, _unpack_rows(w_in_refs[piece][:, s0:s1]),
                                            preferred_element_type=F32)
    a_lr = z_ref[:, lay.off_a:lay.off_a + lay.rank_pad].astype(BF16)
    la = jnp.dot(a_lr, _unpack_rows(w_a2_ref[...]), preferred_element_type=F32) + b_a_ref[...]
    a_ref[...] = jnp.exp(_log_sigmoid(la) * (1.0 / GLA_TAU))


def _sample_proj(x3, mod_s, w_hd, w_lr, w_tl, w_a2, b_a, layer, lay):
    n, _, d = x3.shape
    kern = functools.partial(_sample_proj_kernel, lay=lay, layer=layer)
    weights = (w_hd, w_lr, w_tl, w_a2)
    vmem = 2 * d * lay.n_pad + 2 * 4 * n * lay.n_pad + 16 * MIB
    return pl.pallas_call(
        kern,
        grid=(1,),
        in_specs=[pl.BlockSpec((n, None, d), lambda i: (0, 0, 0)),
                  _layer_spec(mod_s.shape[1:], layer)]
        + [_layer_spec(w.shape[1:], layer, True) for w in weights] + [_whole_spec(b_a)],
        out_specs=[pl.BlockSpec((n, lay.n_pad), lambda i: (0, 0)),
                   pl.BlockSpec((n, lay.qk), lambda i: (0, 0))],
        out_shape=[jax.ShapeDtypeStruct((n, lay.n_pad), F32),
                   jax.ShapeDtypeStruct((n, lay.qk), F32)],
        compiler_params=pltpu.CompilerParams(dimension_semantics=("arbitrary",),
                                             vmem_limit_bytes=vmem),
        name="sample_proj",
    )(x3, mod_s, *weights, b_a)


def _sample_out_kernel(o_ref, z_ref, x_ref, mod_ref, buf_ref, gng_ref, cw_ref,
                       w_pa_ref, w_pb_ref, w_o_ref, lng_ref, lnb_ref, *rest, lay, layer, alpha):
    y_ref, buf_out_ref, og_scr = rest[-3:]
    d, h_n, dv, cw = lay.d, lay.h, lay.dv, lay.cw
    gng_ref, lng_ref, lnb_ref = (_layer_rows(r, layer) for r in (gng_ref, lng_ref, lnb_ref))
    cw_ref = _layer_taps(cw_ref, layer)
    ck = cw_ref.shape[0]
    gate = mod_ref[:, 2 * d:3 * d]
    gng = gng_ref[...]
    for h in range(h_n):
        o = o_ref[:, h * dv:(h + 1) * dv]
        ms = jnp.mean(o * o, axis=-1, keepdims=True)
        on = o * lax.rsqrt(ms + EPS) * gng
        gg = z_ref[:, lay.off_gg + h * dv:lay.off_gg + (h + 1) * dv]
        og_scr[:, h * dv:(h + 1) * dv] = (on * _silu(gg)).astype(BF16)
    pa = jnp.dot(og_scr[...], _unpack_rows(w_pa_ref[...]), preferred_element_type=F32)

    p = z_ref[:, lay.off_cc:lay.off_cc + cw] * z_ref[:, lay.off_ch:lay.off_ch + cw]
    yc = cw_ref[ck - 1:ck, :] * p
    for i in range(ck - 1):
        yc = yc + cw_ref[i:i + 1, :] * buf_ref[:, i * cw:(i + 1) * cw]
    yc = z_ref[:, lay.off_cb:lay.off_cb + cw] * yc * _silu(z_ref[:, lay.off_gc:lay.off_gc + cw])
    pb = jnp.dot(yc.astype(BF16), _unpack_rows(w_pb_ref[...]), preferred_element_type=F32)
    for i in range(ck - 2):
        buf_out_ref[:, i * cw:(i + 1) * cw] = buf_ref[:, (i + 1) * cw:(i + 2) * cw]
    buf_out_ref[:, (ck - 2) * cw:(ck - 1) * cw] = p

    merged = (jax.nn.sigmoid(z_ref[:, lay.off_mg:lay.off_mg + d]) * pa
              + jax.nn.sigmoid(z_ref[:, lay.off_mc:lay.off_mc + d]) * pb)
    out = jnp.dot(merged.astype(BF16), _unpack_rows(w_o_ref[...]),
                  preferred_element_type=F32)
    hres = alpha * x_ref[...] + gate * out
    mu = jnp.mean(hres, axis=-1, keepdims=True)
    dev = hres - mu
    var = jnp.mean(dev * dev, axis=-1, keepdims=True)
    y_ref[...] = dev * lax.rsqrt(var + EPS) * lng_ref[...] + lnb_ref[...]


def _sample_out(o, z, x3, mod_s, conv_state, prev_conv_out, gng, conv_w, w_pa, w_pb, w_o,
                ln_g, ln_b, layer, lay, alpha):
    n, _, d = x3.shape
    kern = functools.partial(_sample_out_kernel, lay=lay, layer=layer, alpha=alpha)
    conv_spec = _layer_spec(conv_state.shape[1:], layer)
    in_specs = [_whole_spec(o), _whole_spec(z), pl.BlockSpec((n, None, d), lambda i: (0, 0, 0)),
                _layer_spec(mod_s.shape[1:], layer), conv_spec,
                _whole_spec(gng), _whole_spec(conv_w), _layer_spec(w_pa.shape[1:], layer),
                _layer_spec(w_pb.shape[1:], layer), _layer_spec(w_o.shape[1:], layer),
                _whole_spec(ln_g), _whole_spec(ln_b)]
    args = [o, z, x3, mod_s, conv_state, gng, conv_w, w_pa, w_pb, w_o, ln_g, ln_b]
    aliases = {}
    if prev_conv_out is not None:
        in_specs.append(pl.BlockSpec(memory_space=pl.ANY))
        aliases = {len(args): 1}
        args.append(prev_conv_out)
    return pl.pallas_call(
        kern,
        grid=(1,),
        in_specs=in_specs,
        out_specs=[pl.BlockSpec((n, None, d), lambda i: (0, 0, 0)), conv_spec],
        out_shape=[jax.ShapeDtypeStruct((n, 1, d), F32),
                   jax.ShapeDtypeStruct(conv_state.shape, F32)],
        scratch_shapes=[pltpu.VMEM((n, lay.vt), BF16)],
        input_output_aliases=aliases,
        compiler_params=pltpu.CompilerParams(dimension_semantics=("arbitrary",),
                                             vmem_limit_bytes=48 * MIB),
        name="sample_out",
    )(*args)


def kernel(x_prompt, x_sample, c_prompt, c_sample, state_gla, state_conv, w_ada, b_ada, w_in,
           w_a2, b_a, gla_norm_g, conv_w, w_pa, w_pb, w_o, ln_g, ln_b):
    depth, n_s, heads, dk, dv = state_gla.shape
    d = x_prompt.shape[2]
    rank = w_a2.shape[1]
    ck, cw = conv_w.shape[1:]
    lay = _Layout(d, heads, dk, dv, rank, cw)
    alpha = (2 * depth) ** 0.25

    mod_p, mod_s = _ada_mod(c_prompt, c_sample, w_ada, b_ada)

    w_hd, w_lr, w_tl = _split_w_in(w_in, lay)
    w_a2_p = _cast_bf16(jnp.concatenate(
        [w_a2, jnp.zeros((depth, lay.rank_pad - rank, lay.qk), w_a2.dtype)], axis=1))
    w_pa_b, w_pb_b, w_o_b = _cast_bf16(w_pa), _cast_bf16(w_pb), _cast_bf16(w_o)
    taps = jnp.swapaxes(conv_w, 0, 1)
    params = (w_hd, w_lr, w_tl, w_a2_p, b_a, gla_norm_g, taps, w_pa_b, w_pb_b, w_o_b, ln_g, ln_b)

    hp, hs = x_prompt, x_sample
    states = None
    conv_in = state_conv.reshape(depth, n_s, (ck - 1) * cw)
    conv_s = None
    for l in range(depth):
        z, a = _sample_proj(hs, mod_s, w_hd, w_lr, w_tl, w_a2_p, b_a, l, lay)
        hp, *states, o = _prompt_layer(hp, mod_p, a, z, state_gla, params, states, l, lay, alpha)
        hs, conv_s = _sample_out(o.reshape(n_s, lay.vt), z, hs, mod_s, conv_in, conv_s,
                                 gla_norm_g, taps, w_pa_b, w_pb_b, w_o_b, ln_g, ln_b, l, lay,
                                 alpha)
    return (hp, hs, states[0], states[1], states[2], conv_s.reshape(state_conv.shape))
```

```python
import functools

import jax
import jax.numpy as jnp
from jax import lax
from jax.experimental import pallas as pl
from jax.experimental.pallas import tpu as pltpu

F32 = jnp.float32
BF16 = jnp.bfloat16

GLA_TAU = 16.0
GLA_CHUNK = 64
EPS = 1e-5

LANES = 128
SUBLANES = 8
VMEM_PHYSICAL_BYTES = 64 * 1024 * 1024
MIB = 1024 * 1024
STREAMING_VMEM_BYTES = VMEM_PHYSICAL_BYTES * 7 // 8

PROMPT_TILE = 256
PROJ_COLS = 256
PREP_STEPS = 2
ADA_STEPS = 2


LOG2_E = 1.4426950408889634


def _log_sigmoid(x):
    return jnp.minimum(x, 0.0) - jnp.log(1.0 + jnp.exp(-jnp.abs(x)))


def _cumsum_rows(x):
    row = lax.broadcasted_iota(jnp.int32, (SUBLANES, x.shape[1]), 0)
    groups = []
    carry = None
    for g in range(x.shape[0] // SUBLANES):
        t = x[g * SUBLANES:(g + 1) * SUBLANES, :]
        s = 1
        while s < SUBLANES:
            t = t + jnp.where(row >= s, pltpu.roll(t, s, 0), 0.0)
            s *= 2
        if carry is not None:
            t = t + carry
        carry = jnp.broadcast_to(t[SUBLANES - 1:SUBLANES, :], t.shape)
        groups.append(t)
    return jnp.concatenate(groups, axis=0)


def _silu(x):
    return x * jax.nn.sigmoid(x)


PACKED = jnp.uint32
PACK = 2


def _pack_rows(w_bf16):
    return pltpu.bitcast(w_bf16, PACKED)


def _unpack_rows(w_packed):
    return pltpu.bitcast(w_packed, BF16)


class _Layout:
    def __init__(self, d_model, heads, dk, dv, rank, conv_w):
        self.d = d_model
        self.h = heads
        self.dk = dk
        self.dv = dv
        self.rank = rank
        self.cw = conv_w
        self.qk = heads * dk
        self.vt = heads * dv
        self.rank_pad = -(-rank // LANES) * LANES
        self.off_q = 0
        self.off_k = self.off_q + self.qk
        self.off_v = self.off_k + self.qk
        self.off_gg = self.off_v + self.vt
        self.off_a = self.off_gg + self.vt
        self.off_cb = self.off_a + self.rank_pad
        self.off_cc = self.off_cb + conv_w
        self.off_ch = self.off_cc + conv_w
        self.off_gc = self.off_ch + conv_w
        self.off_mg = self.off_gc + conv_w
        self.off_mc = self.off_mg + d_model
        self.n_pad = self.off_mc + d_model
        self.n_head = self.off_a
        self.n_tail = self.n_pad - self.off_cb
        self.groups = (
            (0, self.off_q, self.off_v, self.off_q),
            (0, self.off_v, self.off_gg, self.off_v),
            (0, self.off_gg, self.off_a, self.off_gg),
            (1, 0, self.rank_pad, self.off_a),
            (2, 0, 2 * conv_w, self.off_cb),
            (2, 2 * conv_w, 4 * conv_w, self.off_ch),
            (2, 4 * conv_w, 4 * conv_w + d_model, self.off_mg),
            (2, 4 * conv_w + d_model, self.n_tail, self.off_mc))


def _layer_spec(shape, layer, single_buffer=False):
    zeros = (0,) * len(shape)
    kwargs = {"pipeline_mode": pl.Buffered(1)} if single_buffer else {}
    return pl.BlockSpec((None,) + tuple(shape), lambda *_: (layer,) + zeros, **kwargs)


def _whole_spec(arr):
    return pl.BlockSpec(arr.shape, lambda *_: (0,) * arr.ndim)


def _layer_rows(ref, layer):
    return ref.at[pl.ds(layer, 1)]


def _layer_taps(ref, layer):
    return ref.at[:, layer]


def _transpose_cast_kernel(w_ref, o_ref, *, valid):
    wt = jnp.transpose(w_ref[0])
    if valid < wt.shape[1]:
        lane = lax.broadcasted_iota(jnp.int32, wt.shape, 1)
        wt = jnp.where(lane < valid, wt, 0.0)
    o_ref[...] = _pack_rows(wt.astype(BF16))


def _transpose_cast(w_t, col0, n_valid, n_out, cols):
    depth, n_in, d = w_t.shape
    assert n_out % cols == 0 and col0 % SUBLANES == 0 and col0 + n_out <= n_in
    kern = functools.partial(_transpose_cast_kernel, valid=min(n_valid, cols))
    return pl.pallas_call(
        kern,
        grid=(depth, n_out // cols),
        in_specs=[pl.BlockSpec((pl.Element(1), pl.Element(cols), pl.Element(d)),
                               lambda l, j: (l, pl.multiple_of(col0 + j * cols, SUBLANES), 0))],
        out_specs=pl.BlockSpec((None, d // PACK, cols), lambda l, j: (l, 0, j)),
        out_shape=jax.ShapeDtypeStruct((depth, d // PACK, n_out), PACKED),
        compiler_params=pltpu.CompilerParams(
            dimension_semantics=("arbitrary", "arbitrary"),
            vmem_limit_bytes=STREAMING_VMEM_BYTES),
        name="transpose_cast",
    )(w_t)


def _split_w_in(w_in, lay):
    n_in = w_in.shape[2]
    a_end = lay.off_a + lay.rank
    assert n_in == a_end + lay.n_tail and lay.rank <= lay.rank_pad == LANES
    w_t = jnp.swapaxes(w_in, 1, 2)
    head = _transpose_cast(w_t, 0, lay.n_head, lay.n_head, lay.n_head // PREP_STEPS)
    low = _transpose_cast(w_t, lay.off_a, lay.rank, lay.rank_pad, lay.rank_pad)
    tail = _transpose_cast(w_t, a_end, lay.n_tail, lay.n_tail, lay.n_tail // PREP_STEPS)
    return head, low, tail


def _cast_kernel(w_ref, o_ref):
    o_ref[...] = _pack_rows(w_ref[...].astype(BF16))


def _cast_bf16(w):
    depth, rows, cols = w.shape
    return pl.pallas_call(
        _cast_kernel,
        grid=(depth,),
        in_specs=[pl.BlockSpec((None, rows, cols), lambda l: (l, 0, 0))],
        out_specs=pl.BlockSpec((None, rows // PACK, cols), lambda l: (l, 0, 0)),
        out_shape=jax.ShapeDtypeStruct((depth, rows // PACK, cols), PACKED),
        compiler_params=pltpu.CompilerParams(
            dimension_semantics=("arbitrary",),
            vmem_limit_bytes=STREAMING_VMEM_BYTES),
        name="cast_bf16",
    )(w)


def _ada_kernel(cp_ref, cs_ref, w_ref, b_ref, op_ref, os_ref):
    w = w_ref[...].astype(BF16)
    b = b_ref[...]
    op_ref[...] = jnp.dot(cp_ref[...].astype(BF16), w, preferred_element_type=F32) + b
    os_ref[...] = jnp.dot(cs_ref[...].astype(BF16), w, preferred_element_type=F32) + b


def _ada_mod(c_prompt, c_sample, w_ada, b_ada):
    depth, d, n3 = w_ada.shape
    bp, ns = c_prompt.shape[0], c_sample.shape[0]
    tn = n3 // ADA_STEPS
    return pl.pallas_call(
        _ada_kernel,
        grid=(depth, n3 // tn),
        in_specs=[
            pl.BlockSpec((bp, d), lambda l, j: (0, 0)),
            pl.BlockSpec((ns, d), lambda l, j: (0, 0)),
            pl.BlockSpec((None, d, tn), lambda l, j: (l, 0, j)),
            pl.BlockSpec((None, 1, tn), lambda l, j: (l, 0, j)),
        ],
        out_specs=[pl.BlockSpec((None, bp, tn), lambda l, j: (l, 0, j)),
                   pl.BlockSpec((None, ns, tn), lambda l, j: (l, 0, j))],
        out_shape=[jax.ShapeDtypeStruct((depth, bp, n3), F32),
                   jax.ShapeDtypeStruct((depth, ns, n3), F32)],
        compiler_params=pltpu.CompilerParams(
            dimension_semantics=("arbitrary", "arbitrary"),
            vmem_limit_bytes=STREAMING_VMEM_BYTES),
        name="ada_mod",
    )(c_prompt, c_sample, w_ada, b_ada.reshape(depth, 1, n3))


def _project_plan(lay):
    plan = []
    for piece, s0, s1, d0 in lay.groups:
        for c0 in range(s0, s1, PROJ_COLS):
            plan.append((piece, c0, min(c0 + PROJ_COLS, s1), d0 + c0 - s0))
    return plan


def _project_stages(x_rows, mod_ref, brow, w_in_refs, u_scr, z_dst, lay):
    d = lay.d
    shift = mod_ref[pl.ds(brow, 1), 0:d]
    scale = mod_ref[pl.ds(brow, 1), d:2 * d]
    u_scr[...] = (x_rows() * (1.0 + scale) + shift).astype(BF16)
    yield
    for piece, c0, c1, d0 in _project_plan(lay):
        z_dst[:, d0:d0 + c1 - c0] = jnp.dot(u_scr[...],
                                            _unpack_rows(w_in_refs[piece][:, c0:c1]),
                                            preferred_element_type=F32)
        yield


def _tile_chunks(tile):
    return [(c, c * GLA_CHUNK, (c + 1) * GLA_CHUNK) for c in range(tile // GLA_CHUNK)]


def _tile_heads(lay):
    return [(h, slice(h * lay.dk, (h + 1) * lay.dk), slice(h * lay.dv, (h + 1) * lay.dv))
            for h in range(lay.h)]


W_DECAY, W_FACTORS, W_SCORES, W_INCR, W_RECUR, W_OUT, W_NORM, W_CONV, W_MERGE, W_LN = (
    4, 4, 1, 1, 5, 1, 11, 10, 10, 10)


def _premix_weights(lay, tile):
    return [W_DECAY] + [W_FACTORS] * (len(_tile_chunks(tile)) * lay.h)


def _premix_stages(z_scr, w_a2_ref, b_a_ref, la_scr, qin_scr, kin_scr, kst_scr, vb_scr, dec_scr,
                   lay, tile):
    dk, dv = lay.dk, lay.dv
    q_scale = dk ** -0.5

    a_lr = z_scr[:, lay.off_a:lay.off_a + lay.rank_pad].astype(BF16)
    la_scr[...] = (jnp.dot(a_lr, _unpack_rows(w_a2_ref[...]), preferred_element_type=F32)
                   + b_a_ref[...])
    yield W_DECAY

    for c, r0, r1 in _tile_chunks(tile):
        for h, ks, vs in _tile_heads(lay):
            log_a = _log_sigmoid(la_scr[r0:r1, ks]) * (1.0 / GLA_TAU)
            b2 = _cumsum_rows(log_a) * LOG2_E
            bl2 = b2[GLA_CHUNK - 1:GLA_CHUNK, :]
            q_in = z_scr[r0:r1, lay.off_q + h * dk:lay.off_q + (h + 1) * dk] * q_scale
            k = z_scr[r0:r1, lay.off_k + h * dk:lay.off_k + (h + 1) * dk]
            qin_scr[r0:r1, ks] = (q_in * jnp.exp2(b2)).astype(BF16)
            kin_scr[r0:r1, ks] = (k * jnp.exp2(-b2)).astype(BF16)
            kst_scr[r0:r1, ks] = (k * jnp.exp2(bl2 - b2)).astype(BF16)
            dec_scr[c:c + 1, ks] = jnp.exp2(bl2)
            vb_scr[r0:r1, vs] = z_scr[r0:r1, lay.off_v + h * dv:lay.off_v + (h + 1) * dv] \
                .astype(BF16)
            yield W_FACTORS


def _mix_weights(lay, tile):
    chunks = len(_tile_chunks(tile))
    head = ([W_SCORES] * chunks + [W_INCR] * lay.h + [W_RECUR] * lay.h + [W_OUT] * chunks
            + [W_NORM] * chunks + [W_CONV] * chunks)
    tail = [0, 0] + [W_MERGE] * chunks + [0] + [W_LN] * chunks
    return head, tail


def _mix_stages(z_scr, x_ref, y_ref, row0, gate, gng_ref, cw_ref,
                w_pa_ref, w_pb_ref, w_o_ref, lng_ref, lnb_ref,
                p_scr, s_scr, og_scr, yc_scr, mg_scr,
                qin_scr, kin_scr, kst_scr, vb_scr, dec_scr, att_scr, ds_scr, sbf_scr,
                lay, tile, alpha):
    d, h_n, dk, dv, cw = lay.d, lay.h, lay.dk, lay.dv, lay.cw
    ck = cw_ref.shape[0]
    halo = SUBLANES
    chunks = _tile_chunks(tile)
    heads = _tile_heads(lay)
    causal = (lax.broadcasted_iota(jnp.int32, (GLA_CHUNK, GLA_CHUNK), 0)
              >= lax.broadcasted_iota(jnp.int32, (GLA_CHUNK, GLA_CHUNK), 1))
    gng = gng_ref[...]

    for c, r0, r1 in chunks:
        for h, ks, vs in heads:
            att = lax.dot_general(qin_scr[r0:r1, ks], kin_scr[r0:r1, ks],
                                  (((1,), (1,)), ((), ())), preferred_element_type=F32)
            att_scr[h, r0:r1, :] = jnp.where(causal, att, 0.0).astype(BF16)
        yield W_SCORES

    for h, ks, vs in heads:
        for c, r0, r1 in chunks:
            ds_scr[c * h_n + h] = lax.dot_general(kst_scr[r0:r1, ks], vb_scr[r0:r1, vs],
                                                  (((0,), (0,)), ((), ())),
                                                  preferred_element_type=F32)
        yield W_INCR
    for h, ks, vs in heads:
        state = s_scr[h]
        for c, r0, r1 in chunks:
            sbf_scr[c * h_n + h] = state.astype(BF16)
            dec = jnp.transpose(jnp.broadcast_to(dec_scr[c:c + 1, ks], (dk, dk)))
            dec = jnp.concatenate([dec] * (dv // dk), axis=1)
            state = dec * state + ds_scr[c * h_n + h]
        s_scr[h] = state
        yield W_RECUR

    for c, r0, r1 in chunks:
        for h, ks, vs in heads:
            lhs = jnp.concatenate([qin_scr[r0:r1, ks], att_scr[h, r0:r1, :]], axis=1)
            rhs = jnp.concatenate([sbf_scr[c * h_n + h], vb_scr[r0:r1, vs]], axis=0)
            z_scr[r0:r1, vs] = jnp.dot(lhs, rhs, preferred_element_type=F32)
        yield W_OUT

    for c, r0, r1 in chunks:
        for h, ks, vs in heads:
            o = z_scr[r0:r1, vs]
            ms = jnp.mean(o * o, axis=-1, keepdims=True)
            on = o * lax.rsqrt(ms + EPS) * gng
            gg = z_scr[r0:r1, lay.off_gg + h * dv:lay.off_gg + (h + 1) * dv]
            og_scr[r0:r1, vs] = (on * _silu(gg)).astype(BF16)
        yield W_NORM

    for c, r0, r1 in chunks:
        for g0 in range(0, cw, LANES):
            gl = slice(g0, g0 + LANES)
            p = (z_scr[r0:r1, lay.off_cc + g0:lay.off_cc + g0 + LANES]
                 * z_scr[r0:r1, lay.off_ch + g0:lay.off_ch + g0 + LANES])
            p_scr[halo + r0:halo + r1, gl] = p
            ext = jnp.concatenate([p_scr[r0:halo + r0, gl], p], axis=0)
            yc = cw_ref[ck - 1:ck, gl] * p
            for i in range(ck - 1):
                lag = ck - 1 - i
                yc = yc + cw_ref[i:i + 1, gl] * pltpu.roll(ext, lag, 0)[halo:, :]
            yc = (z_scr[r0:r1, lay.off_cb + g0:lay.off_cb + g0 + LANES] * yc
                  * _silu(z_scr[r0:r1, lay.off_gc + g0:lay.off_gc + g0 + LANES]))
            yc_scr[r0:r1, gl] = yc.astype(BF16)
        yield W_CONV

    p_scr[0:halo, :] = p_scr[tile:tile + halo, :]

    z_scr[:, 0:d] = jnp.dot(og_scr[...], _unpack_rows(w_pa_ref[...]), preferred_element_type=F32)
    yield 0
    z_scr[:, d:2 * d] = jnp.dot(yc_scr[...], _unpack_rows(w_pb_ref[...]),
                                preferred_element_type=F32)
    yield 0
    for c, r0, r1 in chunks:
        for g0 in range(0, d, 2 * LANES):
            g1 = g0 + 2 * LANES
            merged = (jax.nn.sigmoid(z_scr[r0:r1, lay.off_mg + g0:lay.off_mg + g1])
                      * z_scr[r0:r1, g0:g1]
                      + jax.nn.sigmoid(z_scr[r0:r1, lay.off_mc + g0:lay.off_mc + g1])
                      * z_scr[r0:r1, d + g0:d + g1])
            mg_scr[r0:r1, g0:g1] = merged.astype(BF16)
        yield W_MERGE
    z_scr[:, 2 * d:3 * d] = jnp.dot(mg_scr[...], _unpack_rows(w_o_ref[...]),
                                    preferred_element_type=F32)
    yield 0

    lng = lng_ref[...]
    lnb = lnb_ref[...]
    for c, r0, r1 in chunks:
        hres = alpha * x_ref[0, row0 + r0:row0 + r1, :] + gate * z_scr[r0:r1, 2 * d:3 * d]
        mu = jnp.mean(hres, axis=-1, keepdims=True)
        dev = hres - mu
        var = jnp.mean(dev * dev, axis=-1, keepdims=True)
        y_ref[0, row0 + r0:row0 + r1, :] = dev * lax.rsqrt(var + EPS) * lng + lnb
        yield W_LN


def _emit_overlapped(mix, mix_weights, premix, premix_weights, proj, n_proj, premix_needs_proj,
                     fillers):
    head, tail = mix_weights
    order = [("mix", w) for w in head]
    done = 0
    for i, w in enumerate(tail):
        while done * len(tail) < i * len(premix_weights):
            order.append(("premix", premix_weights[done]))
            done += 1
        order.append(("mix", w))
    order += [("premix", w) for w in premix_weights[done:]]

    total = sum(w for _, w in order)
    matmul_slots = sum(1 for _, w in order if w == 0)
    done_w = done_p = done_f = slot = 0
    for source, weight in order:
        done_w += weight
        while done_p * total < done_w * n_proj or (source == "premix"
                                                    and done_p < premix_needs_proj):
            next(proj)
            done_p += 1
        if weight == 0:
            slot += 1
            while done_f * matmul_slots < slot * len(fillers):
                fillers[done_f]()
                done_f += 1
        assert next(mix if source == "mix" else premix) == weight
    while done_p < n_proj:
        next(proj)
        done_p += 1
    assert done_f == len(fillers) and next(mix, None) is None and next(premix, None) is None


def _step_row_reader(ref, rows, sub, steps_per_block):
    def read(i, lanes):
        val = ref[i:i + 1, lanes]
        for s in range(1, steps_per_block):
            val = jnp.where(sub == s, ref[s * rows + i:s * rows + i + 1, lanes], val)
        return val
    return read


def _sample_state_step(i, a_row, qkv_row, s_ref, s_out_ref, o_ref, lay):
    h_n, dk, dv = lay.h, lay.dk, lay.dv
    q_scale = dk ** -0.5
    rows = [a_row(i, slice(h * dk, (h + 1) * dk)) for h in range(h_n)]
    rows += [qkv_row(i, slice(lay.off_k + h * dk, lay.off_k + (h + 1) * dk)) for h in range(h_n)]
    rows += [qkv_row(i, slice(lay.off_q + h * dk, lay.off_q + (h + 1) * dk)) for h in range(h_n)]
    rows.append(jnp.zeros((dk - len(rows), dk), F32))
    mt = jnp.transpose(jnp.concatenate(rows, axis=0))
    for h in range(h_n):
        a_col = mt[:, h:h + 1]
        k_col = mt[:, h_n + h:h_n + h + 1]
        q_col = mt[:, 2 * h_n + h:2 * h_n + h + 1]
        v = qkv_row(i, slice(lay.off_v + h * dv, lay.off_v + (h + 1) * dv))
        s_new = a_col * s_ref[i, h] + k_col * v
        s_out_ref[i, h] = s_new
        o_ref[i:i + 1, h * dv:(h + 1) * dv] = q_scale * jnp.sum(q_col * s_new, axis=0,
                                                               keepdims=True)


def _prompt_kernel(xc_ref, xn_ref, mod_ref, sa_ref, sqkv_ref, sst_ref,
                   w_hd_ref, w_lr_ref, w_tl_ref, w_a2_ref, b_a_ref,
                   gng_ref, cw_ref, w_pa_ref, w_pb_ref, w_o_ref, lng_ref, lnb_ref, *rest,
                   lay, layer, tile, alpha, n_t, n_tiles, sample_rows):
    (y_ref, sg_ref, sc_ref, sso_ref, so_ref,
     ua_scr, ub_scr, za_scr, zb_scr, la_scr, p_scr, s_scr, og_scr, yc_scr, mg_scr,
     qin_scr, kin_scr, kst_scr, vb_scr, dec_scr, att_scr, ds_scr, sbf_scr) = rest[-23:]
    b_a_ref, gng_ref, lng_ref, lnb_ref = (_layer_rows(r, layer)
                                          for r in (b_a_ref, gng_ref, lng_ref, lnb_ref))
    cw_ref = _layer_taps(cw_ref, layer)
    j = pl.program_id(0)
    steps_per_seq = n_t // 2
    jj = lax.rem(j, steps_per_seq)
    b_cur = lax.div(j, steps_per_seq)
    b_next = lax.div(jnp.minimum(2 * j + 2, n_tiles - 1), n_t)
    d = lay.d
    ck = cw_ref.shape[0]
    halo = SUBLANES
    w_in_refs = (w_hd_ref, w_lr_ref, w_tl_ref)

    @pl.when(jj == 0)
    def _():
        s_scr[...] = jnp.zeros_like(s_scr)
        p_scr[0:halo, :] = jnp.zeros((halo, lay.cw), F32)

    def project(x_rows, brow, u_scr, z_dst):
        return _project_stages(x_rows, mod_ref, brow, w_in_refs, u_scr, z_dst, lay)

    operands = (qin_scr, kin_scr, kst_scr, vb_scr, dec_scr)

    def premix(z_scr):
        return _premix_stages(z_scr, w_a2_ref, b_a_ref, la_scr, *operands, lay, tile)

    def mix(z_scr, row0):
        return _mix_stages(z_scr, xc_ref, y_ref, row0, gate, gng_ref, cw_ref, w_pa_ref, w_pb_ref,
                           w_o_ref, lng_ref, lnb_ref, p_scr, s_scr, og_scr, yc_scr, mg_scr,
                           *operands, att_scr, ds_scr, sbf_scr, lay, tile, alpha)

    @pl.when(j == 0)
    def _():
        for _ in project(lambda: xc_ref[0, 0:tile, :], b_cur, ua_scr, za_scr):
            pass
        for _ in premix(za_scr):
            pass

    gate = mod_ref[pl.ds(b_cur, 1), 2 * d:3 * d]
    mix_weights = _mix_weights(lay, tile)
    premix_weights = _premix_weights(lay, tile)
    plan = _project_plan(lay)
    n_proj = 1 + len(plan)
    premix_needs_proj = 1 + sum(1 for _, _, _, d0 in plan if d0 < lay.off_cb)

    steps_per_block = SUBLANES // sample_rows
    sub = lax.rem(j, steps_per_block)
    a_row = _step_row_reader(sa_ref, sample_rows, sub, steps_per_block)
    qkv_row = _step_row_reader(sqkv_ref, sample_rows, sub, steps_per_block)
    fillers = [functools.partial(_sample_state_step, i, a_row, qkv_row, sst_ref,
                                 sso_ref, so_ref, lay) for i in range(sample_rows)]
    half = sample_rows // 2

    _emit_overlapped(mix(za_scr, 0), mix_weights, premix(zb_scr), premix_weights,
                     project(lambda: xc_ref[0, tile:2 * tile, :], b_cur, ub_scr, zb_scr), n_proj,
                     premix_needs_proj, fillers[:half])
    _emit_overlapped(mix(zb_scr, tile), mix_weights, premix(za_scr), premix_weights,
                     project(lambda: xn_ref[0], b_next, ua_scr, za_scr), n_proj,
                     premix_needs_proj, fillers[half:])

    @pl.when(jj == steps_per_seq - 1)
    def _():
        sg_ref[0] = s_scr[...]
        sc_ref[0] = p_scr[halo - (ck - 1):halo, :]


def _prompt_vmem_bytes(lay, tile, sample_rows):
    weights = 2 * (lay.d * lay.n_pad + lay.rank_pad * lay.qk + lay.vt * lay.d
                   + lay.cw * lay.d + lay.d * lay.d)
    state = lay.h * lay.dk * lay.dv * 4
    io = (2 * (2 * tile + tile + 2 * tile) * lay.d * 4 + 2 * state
          + 2 * 2 * sample_rows * state + 2 * SUBLANES * (lay.qk + lay.off_gg + lay.vt) * 4)
    chunk_heads = (tile // GLA_CHUNK) * lay.h
    scratch = (2 * tile * lay.d * 2 + 2 * tile * lay.n_pad * 4 + tile * lay.qk * 4
               + (tile + SUBLANES) * lay.cw * 4 + lay.h * lay.dk * lay.dv * 4
               + tile * lay.vt * 2 + tile * lay.cw * 2 + tile * lay.d * 2
               + 3 * tile * lay.qk * 2 + tile * lay.vt * 2 + SUBLANES * lay.qk * 4
               + lay.h * tile * LANES * 2 + chunk_heads * lay.dk * lay.dv * 6)
    temporaries = tile * lay.n_pad * 2
    return min(weights + io + scratch + temporaries, VMEM_PHYSICAL_BYTES * 15 // 16)


def _prompt_layer(x, mod_p, sample_a, sample_z, sample_state, params, prev_states, layer, lay,
                  alpha):
    (w_hd, w_lr, w_tl, w_a2, b_a, gng, conv_w, w_pa, w_pb, w_o, ln_g, ln_b) = params
    bsz, seq, d = x.shape
    depth = w_hd.shape[0]
    tile = PROMPT_TILE
    assert seq % (2 * tile) == 0 and tile % GLA_CHUNK == 0
    ck = conv_w.shape[0]
    n_t = seq // tile
    n_tiles = bsz * n_t
    n_steps = n_tiles // 2
    steps_per_seq = n_t // 2
    n_chunk_heads = (tile // GLA_CHUNK) * lay.h
    assert tile // GLA_CHUNK <= SUBLANES
    n_s = sample_state.shape[1]
    sample_rows = n_s // n_steps
    assert sample_rows * n_steps == n_s and sample_rows % 2 == 0 and SUBLANES % sample_rows == 0
    steps_per_block = SUBLANES // sample_rows

    def weight(arr):
        return _layer_spec(arr.shape[1:], layer, True)

    def next_tile(j):
        g = jnp.minimum(2 * j + 2, n_tiles - 1)
        return (g // n_t, g % n_t, 0)

    sample_state_spec = pl.BlockSpec((None, sample_rows, lay.h, lay.dk, lay.dv),
                                     lambda j: (layer, j, 0, 0, 0))
    in_specs = [
        pl.BlockSpec((1, 2 * tile, d), lambda j: (j // steps_per_seq, j % steps_per_seq, 0)),
        pl.BlockSpec((1, tile, d), next_tile),
        _layer_spec(mod_p.shape[1:], layer),
        pl.BlockSpec((SUBLANES, lay.qk), lambda j: (j // steps_per_block, 0)),
        pl.BlockSpec((SUBLANES, lay.off_gg), lambda j: (j // steps_per_block, 0)),
        sample_state_spec,
        weight(w_hd), weight(w_lr), weight(w_tl), weight(w_a2), _whole_spec(b_a),
        _whole_spec(gng), _whole_spec(conv_w), weight(w_pa), weight(w_pb), weight(w_o),
        _whole_spec(ln_g), _whole_spec(ln_b),
    ]
    args = [x, x, mod_p, sample_a, sample_z, sample_state, *params]
    aliases = {}
    if prev_states is not None:
        for i, prev in enumerate(prev_states):
            in_specs.append(pl.BlockSpec(memory_space=pl.ANY))
            aliases[len(args)] = 1 + i
            args.append(prev)

    kern = functools.partial(_prompt_kernel, lay=lay, layer=layer, tile=tile, alpha=alpha,
                             n_t=n_t, n_tiles=n_tiles, sample_rows=sample_rows)
    return pl.pallas_call(
        kern,
        grid=(n_steps,),
        in_specs=in_specs,
        out_specs=[
            pl.BlockSpec((1, 2 * tile, d), lambda j: (j // steps_per_seq, j % steps_per_seq, 0)),
            pl.BlockSpec((None, 1, lay.h, lay.dk, lay.dv),
                         lambda j: (layer, j // steps_per_seq, 0, 0, 0)),
            pl.BlockSpec((None, 1, ck - 1, lay.cw), lambda j: (layer, j // steps_per_seq, 0, 0)),
            sample_state_spec,
            pl.BlockSpec((None, sample_rows, lay.vt), lambda j: (j, 0, 0)),
        ],
        out_shape=[
            jax.ShapeDtypeStruct((bsz, seq, d), F32),
            jax.ShapeDtypeStruct((depth, bsz, lay.h, lay.dk, lay.dv), F32),
            jax.ShapeDtypeStruct((depth, bsz, ck - 1, lay.cw), F32),
            jax.ShapeDtypeStruct(sample_state.shape, F32),
            jax.ShapeDtypeStruct((n_steps, sample_rows, lay.vt), F32),
        ],
        scratch_shapes=[
            pltpu.VMEM((tile, d), BF16),
            pltpu.VMEM((tile, d), BF16),
            pltpu.VMEM((tile, lay.n_pad), F32),
            pltpu.VMEM((tile, lay.n_pad), F32),
            pltpu.VMEM((tile, lay.qk), F32),
            pltpu.VMEM((tile + SUBLANES, lay.cw), F32),
            pltpu.VMEM((lay.h, lay.dk, lay.dv), F32),
            pltpu.VMEM((tile, lay.vt), BF16),
            pltpu.VMEM((tile, lay.cw), BF16),
            pltpu.VMEM((tile, d), BF16),
            pltpu.VMEM((tile, lay.qk), BF16),
            pltpu.VMEM((tile, lay.qk), BF16),
            pltpu.VMEM((tile, lay.qk), BF16),
            pltpu.VMEM((tile, lay.vt), BF16),
            pltpu.VMEM((SUBLANES, lay.qk), F32),
            pltpu.VMEM((lay.h, tile, GLA_CHUNK), BF16),
            pltpu.VMEM((n_chunk_heads, lay.dk, lay.dv), F32),
            pltpu.VMEM((n_chunk_heads, lay.dk, lay.dv), BF16),
        ],
        input_output_aliases=aliases,
        compiler_params=pltpu.CompilerParams(
            dimension_semantics=("arbitrary",),
            vmem_limit_bytes=_prompt_vmem_bytes(lay, tile, sample_rows)),
        name="prompt_layer",
    )(*args)


def _sample_proj_kernel(x_ref, mod_ref, w_hd_ref, w_lr_ref, w_tl_ref, w_a2_ref, b_a_ref,
                        z_ref, a_ref, *, lay, layer):
    d = lay.d
    b_a_ref = _layer_rows(b_a_ref, layer)
    shift = mod_ref[:, 0:d]
    scale = mod_ref[:, d:2 * d]
    u = (x_ref[...] * (1.0 + scale) + shift).astype(BF16)
    w_in_refs = (w_hd_ref, w_lr_ref, w_tl_ref)
    for piece, s0, s1, d0 in lay.groups:
        z_ref[:, d0:d0 + s1 - s0] = jnp.dot(u, _unpack_rows(w_in_refs[piece][:, s0:s1]),
                                            preferred_element_type=F32)
    a_lr = z_ref[:, lay.off_a:lay.off_a + lay.rank_pad].astype(BF16)
    la = jnp.dot(a_lr, _unpack_rows(w_a2_ref[...]), preferred_element_type=F32) + b_a_ref[...]
    a_ref[...] = jnp.exp(_log_sigmoid(la) * (1.0 / GLA_TAU))


def _sample_proj(x3, mod_s, w_hd, w_lr, w_tl, w_a2, b_a, layer, lay):
    n, _, d = x3.shape
    kern = functools.partial(_sample_proj_kernel, lay=lay, layer=layer)
    weights = (w_hd, w_lr, w_tl, w_a2)
    return pl.pallas_call(
        kern,
        grid=(1,),
        in_specs=[pl.BlockSpec((n, None, d), lambda i: (0, 0, 0)),
                  _layer_spec(mod_s.shape[1:], layer)]
        + [_layer_spec(w.shape[1:], layer, True) for w in weights] + [_whole_spec(b_a)],
        out_specs=[pl.BlockSpec((n, lay.n_pad), lambda i: (0, 0)),
                   pl.BlockSpec((n, lay.qk), lambda i: (0, 0))],
        out_shape=[jax.ShapeDtypeStruct((n, lay.n_pad), F32),
                   jax.ShapeDtypeStruct((n, lay.qk), F32)],
        compiler_params=pltpu.CompilerParams(dimension_semantics=("arbitrary",),
                                             vmem_limit_bytes=STREAMING_VMEM_BYTES),
        name="sample_proj",
    )(x3, mod_s, *weights, b_a)


def _sample_out_kernel(o_ref, z_ref, x_ref, mod_ref, buf_ref, gng_ref, cw_ref,
                       w_pa_ref, w_pb_ref, w_o_ref, lng_ref, lnb_ref, *rest, lay, layer, alpha):
    y_ref, buf_out_ref, og_scr = rest[-3:]
    d, h_n, dv, cw = lay.d, lay.h, lay.dv, lay.cw
    gng_ref, lng_ref, lnb_ref = (_layer_rows(r, layer) for r in (gng_ref, lng_ref, lnb_ref))
    cw_ref = _layer_taps(cw_ref, layer)
    ck = cw_ref.shape[0]
    gate = mod_ref[:, 2 * d:3 * d]
    gng = gng_ref[...]
    for h in range(h_n):
        o = o_ref[:, h * dv:(h + 1) * dv]
        ms = jnp.mean(o * o, axis=-1, keepdims=True)
        on = o * lax.rsqrt(ms + EPS) * gng
        gg = z_ref[:, lay.off_gg + h * dv:lay.off_gg + (h + 1) * dv]
        og_scr[:, h * dv:(h + 1) * dv] = (on * _silu(gg)).astype(BF16)
    pa = jnp.dot(og_scr[...], _unpack_rows(w_pa_ref[...]), preferred_element_type=F32)

    p = z_ref[:, lay.off_cc:lay.off_cc + cw] * z_ref[:, lay.off_ch:lay.off_ch + cw]
    yc = cw_ref[ck - 1:ck, :] * p
    for i in range(ck - 1):
        yc = yc + cw_ref[i:i + 1, :] * buf_ref[:, i * cw:(i + 1) * cw]
    yc = z_ref[:, lay.off_cb:lay.off_cb + cw] * yc * _silu(z_ref[:, lay.off_gc:lay.off_gc + cw])
    pb = jnp.dot(yc.astype(BF16), _unpack_rows(w_pb_ref[...]), preferred_element_type=F32)
    for i in range(ck - 2):
        buf_out_ref[:, i * cw:(i + 1) * cw] = buf_ref[:, (i + 1) * cw:(i + 2) * cw]
    buf_out_ref[:, (ck - 2) * cw:(ck - 1) * cw] = p

    merged = (jax.nn.sigmoid(z_ref[:, lay.off_mg:lay.off_mg + d]) * pa
              + jax.nn.sigmoid(z_ref[:, lay.off_mc:lay.off_mc + d]) * pb)
    out = jnp.dot(merged.astype(BF16), _unpack_rows(w_o_ref[...]),
                  preferred_element_type=F32)
    hres = alpha * x_ref[...] + gate * out
    mu = jnp.mean(hres, axis=-1, keepdims=True)
    dev = hres - mu
    var = jnp.mean(dev * dev, axis=-1, keepdims=True)
    y_ref[...] = dev * lax.rsqrt(var + EPS) * lng_ref[...] + lnb_ref[...]


def _sample_out(o, z, x3, mod_s, conv_state, prev_conv_out, gng, conv_w, w_pa, w_pb, w_o,
                ln_g, ln_b, layer, lay, alpha):
    n, _, d = x3.shape
    kern = functools.partial(_sample_out_kernel, lay=lay, layer=layer, alpha=alpha)
    conv_spec = _layer_spec(conv_state.shape[1:], layer)
    in_specs = [_whole_spec(o), _whole_spec(z), pl.BlockSpec((n, None, d), lambda i: (0, 0, 0)),
                _layer_spec(mod_s.shape[1:], layer), conv_spec,
                _whole_spec(gng), _whole_spec(conv_w), _layer_spec(w_pa.shape[1:], layer),
                _layer_spec(w_pb.shape[1:], layer), _layer_spec(w_o.shape[1:], layer),
                _whole_spec(ln_g), _whole_spec(ln_b)]
    args = [o, z, x3, mod_s, conv_state, gng, conv_w, w_pa, w_pb, w_o, ln_g, ln_b]
    aliases = {}
    if prev_conv_out is not None:
        in_specs.append(pl.BlockSpec(memory_space=pl.ANY))
        aliases = {len(args): 1}
        args.append(prev_conv_out)
    return pl.pallas_call(
        kern,
        grid=(1,),
        in_specs=in_specs,
        out_specs=[pl.BlockSpec((n, None, d), lambda i: (0, 0, 0)), conv_spec],
        out_shape=[jax.ShapeDtypeStruct((n, 1, d), F32),
                   jax.ShapeDtypeStruct(conv_state.shape, F32)],
        scratch_shapes=[pltpu.VMEM((n, lay.vt), BF16)],
        input_output_aliases=aliases,
        compiler_params=pltpu.CompilerParams(dimension_semantics=("arbitrary",),
                                             vmem_limit_bytes=STREAMING_VMEM_BYTES),
        name="sample_out",
    )(*args)


def kernel(x_prompt, x_sample, c_prompt, c_sample, state_gla, state_conv, w_ada, b_ada, w_in,
           w_a2, b_a, gla_norm_g, conv_w, w_pa, w_pb, w_o, ln_g, ln_b):
    depth, n_s, heads, dk, dv = state_gla.shape
    d = x_prompt.shape[2]
    rank = w_a2.shape[1]
    ck, cw = conv_w.shape[1:]
    lay = _Layout(d, heads, dk, dv, rank, cw)
    alpha = (2 * depth) ** 0.25

    mod_p, mod_s = _ada_mod(c_prompt, c_sample, w_ada, b_ada)

    w_hd, w_lr, w_tl = _split_w_in(w_in, lay)
    w_a2_p = _cast_bf16(jnp.concatenate(
        [w_a2, jnp.zeros((depth, lay.rank_pad - rank, lay.qk), w_a2.dtype)], axis=1))
    w_pa_b, w_pb_b, w_o_b = _cast_bf16(w_pa), _cast_bf16(w_pb), _cast_bf16(w_o)
    taps = jnp.swapaxes(conv_w, 0, 1)
    params = (w_hd, w_lr, w_tl, w_a2_p, b_a, gla_norm_g, taps, w_pa_b, w_pb_b, w_o_b, ln_g, ln_b)

    hp, hs = x_prompt, x_sample
    states = None
    conv_in = state_conv.reshape(depth, n_s, (ck - 1) * cw)
    conv_s = None
    for l in range(depth):
        z, a = _sample_proj(hs, mod_s, w_hd, w_lr, w_tl, w_a2_p, b_a, l, lay)
        hp, *states, o = _prompt_layer(hp, mod_p, a, z, state_gla, params, states, l, lay, alpha)
        hs, conv_s = _sample_out(o.reshape(n_s, lay.vt), z, hs, mod_s, conv_in, conv_s,
                                 gla_norm_g, taps, w_pa_b, w_pb_b, w_o_b, ln_g, ln_b, l, lay,
                                 alpha)
    return (hp, hs, states[0], states[1], states[2], conv_s.reshape(state_conv.shape))
```

```python
import functools

import jax
import jax.numpy as jnp
from jax import lax
from jax.experimental import pallas as pl
from jax.experimental.pallas import tpu as pltpu

F32 = jnp.float32
BF16 = jnp.bfloat16

GLA_TAU = 16.0
GLA_CHUNK = 64
EPS = 1e-5

LANES = 128
SUBLANES = 8
VMEM_PHYSICAL_BYTES = 64 * 1024 * 1024
MIB = 1024 * 1024
STREAMING_VMEM_BYTES = VMEM_PHYSICAL_BYTES * 7 // 8

PROMPT_TILE = 256
PROJ_COLS = 256
PREP_STEPS = 2
ADA_STEPS = 2
CAST_STEPS = 4


LOG2_E = 1.4426950408889634


def _log_sigmoid(x):
    return jnp.minimum(x, 0.0) - jnp.log(1.0 + jnp.exp(-jnp.abs(x)))


def _cumsum_rows(x):
    row = lax.broadcasted_iota(jnp.int32, (SUBLANES, x.shape[1]), 0)
    groups = []
    carry = None
    for g in range(x.shape[0] // SUBLANES):
        t = x[g * SUBLANES:(g + 1) * SUBLANES, :]
        s = 1
        while s < SUBLANES:
            t = t + jnp.where(row >= s, pltpu.roll(t, s, 0), 0.0)
            s *= 2
        if carry is not None:
            t = t + carry
        carry = jnp.broadcast_to(t[SUBLANES - 1:SUBLANES, :], t.shape)
        groups.append(t)
    return jnp.concatenate(groups, axis=0)


def _silu(x):
    return x * jax.nn.sigmoid(x)


PACKED = jnp.uint32
PACK = 2


def _pack_rows(w_bf16):
    return pltpu.bitcast(w_bf16, PACKED)


def _unpack_rows(w_packed):
    return pltpu.bitcast(w_packed, BF16)


class _Layout:
    def __init__(self, d_model, heads, dk, dv, rank, conv_w):
        self.d = d_model
        self.h = heads
        self.dk = dk
        self.dv = dv
        self.rank = rank
        self.cw = conv_w
        self.qk = heads * dk
        self.vt = heads * dv
        self.rank_pad = -(-rank // LANES) * LANES
        self.off_q = 0
        self.off_k = self.off_q + self.qk
        self.off_v = self.off_k + self.qk
        self.off_gg = self.off_v + self.vt
        self.off_a = self.off_gg + self.vt
        self.off_cb = self.off_a + self.rank_pad
        self.off_cc = self.off_cb + conv_w
        self.off_ch = self.off_cc + conv_w
        self.off_gc = self.off_ch + conv_w
        self.off_mg = self.off_gc + conv_w
        self.off_mc = self.off_mg + d_model
        self.n_pad = self.off_mc + d_model
        self.n_head = self.off_a
        self.n_tail = self.n_pad - self.off_cb
        self.groups = (
            (0, self.off_q, self.off_v, self.off_q),
            (0, self.off_v, self.off_gg, self.off_v),
            (0, self.off_gg, self.off_a, self.off_gg),
            (1, 0, self.rank_pad, self.off_a),
            (2, 0, 2 * conv_w, self.off_cb),
            (2, 2 * conv_w, 4 * conv_w, self.off_ch),
            (2, 4 * conv_w, 4 * conv_w + d_model, self.off_mg),
            (2, 4 * conv_w + d_model, self.n_tail, self.off_mc))


def _layer_spec(shape, layer, single_buffer=False):
    zeros = (0,) * len(shape)
    kwargs = {"pipeline_mode": pl.Buffered(1)} if single_buffer else {}
    return pl.BlockSpec((None,) + tuple(shape), lambda *_: (layer,) + zeros, **kwargs)


def _whole_spec(arr):
    return pl.BlockSpec(arr.shape, lambda *_: (0,) * arr.ndim)


def _layer_rows(ref, layer):
    return ref.at[pl.ds(layer, 1)]


def _layer_taps(ref, layer):
    return ref.at[:, layer]


def _transpose_cast_kernel(w_ref, o_ref, *, valid):
    wt = jnp.transpose(w_ref[0])
    if valid < wt.shape[1]:
        lane = lax.broadcasted_iota(jnp.int32, wt.shape, 1)
        wt = jnp.where(lane < valid, wt, 0.0)
    o_ref[...] = _pack_rows(wt.astype(BF16))


def _transpose_cast(w_t, col0, n_valid, n_out, cols):
    depth, n_in, d = w_t.shape
    assert n_out % cols == 0 and col0 % SUBLANES == 0 and col0 + n_out <= n_in
    kern = functools.partial(_transpose_cast_kernel, valid=min(n_valid, cols))
    return pl.pallas_call(
        kern,
        grid=(depth, n_out // cols),
        in_specs=[pl.BlockSpec((pl.Element(1), pl.Element(cols), pl.Element(d)),
                               lambda l, j: (l, pl.multiple_of(col0 + j * cols, SUBLANES), 0))],
        out_specs=pl.BlockSpec((None, d // PACK, cols), lambda l, j: (l, 0, j)),
        out_shape=jax.ShapeDtypeStruct((depth, d // PACK, n_out), PACKED),
        compiler_params=pltpu.CompilerParams(
            dimension_semantics=("arbitrary", "arbitrary"),
            vmem_limit_bytes=STREAMING_VMEM_BYTES),
        name="transpose_cast",
    )(w_t)


def _split_w_in(w_in, lay):
    n_in = w_in.shape[2]
    a_end = lay.off_a + lay.rank
    assert n_in == a_end + lay.n_tail and lay.rank <= lay.rank_pad == LANES
    w_t = jnp.swapaxes(w_in, 1, 2)
    head = _transpose_cast(w_t, 0, lay.n_head, lay.n_head, lay.n_head // PREP_STEPS)
    low = _transpose_cast(w_t, lay.off_a, lay.rank, lay.rank_pad, lay.rank_pad)
    tail = _transpose_cast(w_t, a_end, lay.n_tail, lay.n_tail, lay.n_tail // PREP_STEPS)
    return head, low, tail


def _cast_kernel(*refs, rank_pad):
    n = len(refs) // 2
    w_a2_ref, o_a2_ref = refs[n - 1], refs[-1]
    for w_ref, o_ref in zip(refs[:n - 1], refs[n:-1]):
        o_ref[...] = _pack_rows(w_ref[...].astype(BF16))
    w_a2 = w_a2_ref[...]
    pad = jnp.zeros((rank_pad - w_a2.shape[0], w_a2.shape[1]), F32)
    o_a2_ref[...] = _pack_rows(jnp.concatenate([w_a2, pad], axis=0).astype(BF16))


def _cast_weights(weights, w_a2, rank_pad):
    depth = w_a2.shape[0]
    for w in weights:
        assert w.shape[1] % (CAST_STEPS * PACK * SUBLANES) == 0

    def rows_spec(rows, cols):
        return pl.BlockSpec((None, rows, cols), lambda l, r: (l, r, 0))

    def layer_spec(rows, cols):
        return pl.BlockSpec((None, rows, cols), lambda l, r: (l, 0, 0))

    kern = functools.partial(_cast_kernel, rank_pad=rank_pad)
    return pl.pallas_call(
        kern,
        grid=(depth, CAST_STEPS),
        in_specs=[rows_spec(w.shape[1] // CAST_STEPS, w.shape[2]) for w in weights]
        + [layer_spec(*w_a2.shape[1:])],
        out_specs=[rows_spec(w.shape[1] // (CAST_STEPS * PACK), w.shape[2]) for w in weights]
        + [layer_spec(rank_pad // PACK, w_a2.shape[2])],
        out_shape=[jax.ShapeDtypeStruct((depth, w.shape[1] // PACK, w.shape[2]), PACKED)
                   for w in weights]
        + [jax.ShapeDtypeStruct((depth, rank_pad // PACK, w_a2.shape[2]), PACKED)],
        compiler_params=pltpu.CompilerParams(
            dimension_semantics=("arbitrary", "arbitrary"),
            vmem_limit_bytes=STREAMING_VMEM_BYTES),
        name="cast_weights",
    )(*weights, w_a2)


def _ada_kernel(cp_ref, cs_ref, w_ref, b_ref, op_ref, os_ref):
    w = w_ref[...].astype(BF16)
    b = b_ref[...]
    op_ref[...] = jnp.dot(cp_ref[...].astype(BF16), w, preferred_element_type=F32) + b
    os_ref[...] = jnp.dot(cs_ref[...].astype(BF16), w, preferred_element_type=F32) + b


def _ada_mod(c_prompt, c_sample, w_ada, b_ada):
    depth, d, n3 = w_ada.shape
    bp, ns = c_prompt.shape[0], c_sample.shape[0]
    tn = n3 // ADA_STEPS
    return pl.pallas_call(
        _ada_kernel,
        grid=(depth, n3 // tn),
        in_specs=[
            pl.BlockSpec((bp, d), lambda l, j: (0, 0)),
            pl.BlockSpec((ns, d), lambda l, j: (0, 0)),
            pl.BlockSpec((None, d, tn), lambda l, j: (l, 0, j)),
            pl.BlockSpec((None, 1, tn), lambda l, j: (l, 0, j)),
        ],
        out_specs=[pl.BlockSpec((None, bp, tn), lambda l, j: (l, 0, j)),
                   pl.BlockSpec((None, ns, tn), lambda l, j: (l, 0, j))],
        out_shape=[jax.ShapeDtypeStruct((depth, bp, n3), F32),
                   jax.ShapeDtypeStruct((depth, ns, n3), F32)],
        compiler_params=pltpu.CompilerParams(
            dimension_semantics=("arbitrary", "arbitrary"),
            vmem_limit_bytes=STREAMING_VMEM_BYTES),
        name="ada_mod",
    )(c_prompt, c_sample, w_ada, b_ada.reshape(depth, 1, n3))


def _project_plan(lay):
    plan = []
    for piece, s0, s1, d0 in lay.groups:
        for c0 in range(s0, s1, PROJ_COLS):
            plan.append((piece, c0, min(c0 + PROJ_COLS, s1), d0 + c0 - s0))
    return plan


def _project_stages(x_rows, mod_ref, brow, w_in_refs, u_scr, z_dst, lay):
    d = lay.d
    shift = mod_ref[pl.ds(brow, 1), 0:d]
    scale = mod_ref[pl.ds(brow, 1), d:2 * d]
    u_scr[...] = (x_rows() * (1.0 + scale) + shift).astype(BF16)
    yield
    for piece, c0, c1, d0 in _project_plan(lay):
        z_dst[:, d0:d0 + c1 - c0] = jnp.dot(u_scr[...],
                                            _unpack_rows(w_in_refs[piece][:, c0:c1]),
                                            preferred_element_type=F32)
        yield


def _tile_chunks(tile):
    return [(c, c * GLA_CHUNK, (c + 1) * GLA_CHUNK) for c in range(tile // GLA_CHUNK)]


def _tile_heads(lay):
    return [(h, slice(h * lay.dk, (h + 1) * lay.dk), slice(h * lay.dv, (h + 1) * lay.dv))
            for h in range(lay.h)]


W_DECAY, W_FACTORS, W_SCORES, W_INCR, W_RECUR, W_OUT, W_NORM, W_CONV, W_MERGE, W_LN = (
    4, 4, 1, 1, 5, 1, 11, 10, 10, 10)


def _premix_weights(lay, tile):
    return [W_DECAY] + [W_FACTORS] * (len(_tile_chunks(tile)) * lay.h)


def _premix_stages(z_scr, w_a2_ref, b_a_ref, la_scr, qin_scr, kin_scr, kst_scr, vb_scr, dec_scr,
                   lay, tile):
    dk, dv = lay.dk, lay.dv
    q_scale = dk ** -0.5

    a_lr = z_scr[:, lay.off_a:lay.off_a + lay.rank_pad].astype(BF16)
    la_scr[...] = (jnp.dot(a_lr, _unpack_rows(w_a2_ref[...]), preferred_element_type=F32)
                   + b_a_ref[...])
    yield W_DECAY

    for c, r0, r1 in _tile_chunks(tile):
        for h, ks, vs in _tile_heads(lay):
            log_a = _log_sigmoid(la_scr[r0:r1, ks]) * (1.0 / GLA_TAU)
            b2 = _cumsum_rows(log_a) * LOG2_E
            bl2 = b2[GLA_CHUNK - 1:GLA_CHUNK, :]
            q_in = z_scr[r0:r1, lay.off_q + h * dk:lay.off_q + (h + 1) * dk] * q_scale
            k = z_scr[r0:r1, lay.off_k + h * dk:lay.off_k + (h + 1) * dk]
            qin_scr[r0:r1, ks] = (q_in * jnp.exp2(b2)).astype(BF16)
            kin_scr[r0:r1, ks] = (k * jnp.exp2(-b2)).astype(BF16)
            kst_scr[r0:r1, ks] = (k * jnp.exp2(bl2 - b2)).astype(BF16)
            dec_scr[c:c + 1, ks] = jnp.exp2(bl2)
            vb_scr[r0:r1, vs] = z_scr[r0:r1, lay.off_v + h * dv:lay.off_v + (h + 1) * dv] \
                .astype(BF16)
            yield W_FACTORS


def _mix_weights(lay, tile):
    chunks = len(_tile_chunks(tile))
    head = ([W_SCORES] * chunks + [W_INCR] * lay.h + [W_RECUR] * lay.h + [W_OUT] * chunks
            + [W_NORM] * chunks + [W_CONV] * chunks)
    tail = [0, 0] + [W_MERGE] * chunks + [0] + [W_LN] * chunks
    return head, tail


def _mix_stages(z_scr, x_ref, y_ref, row0, gate, gng_ref, cw_ref,
                w_pa_ref, w_pb_ref, w_o_ref, lng_ref, lnb_ref,
                p_scr, s_scr, og_scr, yc_scr, mg_scr,
                qin_scr, kin_scr, kst_scr, vb_scr, dec_scr, att_scr, ds_scr, sbf_scr,
                lay, tile, alpha):
    d, h_n, dk, dv, cw = lay.d, lay.h, lay.dk, lay.dv, lay.cw
    ck = cw_ref.shape[0]
    halo = SUBLANES
    chunks = _tile_chunks(tile)
    heads = _tile_heads(lay)
    causal = (lax.broadcasted_iota(jnp.int32, (GLA_CHUNK, GLA_CHUNK), 0)
              >= lax.broadcasted_iota(jnp.int32, (GLA_CHUNK, GLA_CHUNK), 1))
    gng = gng_ref[...]

    for c, r0, r1 in chunks:
        for h, ks, vs in heads:
            att = lax.dot_general(qin_scr[r0:r1, ks], kin_scr[r0:r1, ks],
                                  (((1,), (1,)), ((), ())), preferred_element_type=F32)
            att_scr[h, r0:r1, :] = jnp.where(causal, att, 0.0).astype(BF16)
        yield W_SCORES

    for h, ks, vs in heads:
        for c, r0, r1 in chunks:
            ds_scr[c * h_n + h] = lax.dot_general(kst_scr[r0:r1, ks], vb_scr[r0:r1, vs],
                                                  (((0,), (0,)), ((), ())),
                                                  preferred_element_type=F32)
        yield W_INCR
    for h, ks, vs in heads:
        state = s_scr[h]
        for c, r0, r1 in chunks:
            sbf_scr[c * h_n + h] = state.astype(BF16)
            dec = jnp.transpose(jnp.broadcast_to(dec_scr[c:c + 1, ks], (dk, dk)))
            dec = jnp.concatenate([dec] * (dv // dk), axis=1)
            state = dec * state + ds_scr[c * h_n + h]
        s_scr[h] = state
        yield W_RECUR

    for c, r0, r1 in chunks:
        for h, ks, vs in heads:
            lhs = jnp.concatenate([qin_scr[r0:r1, ks], att_scr[h, r0:r1, :]], axis=1)
            rhs = jnp.concatenate([sbf_scr[c * h_n + h], vb_scr[r0:r1, vs]], axis=0)
            z_scr[r0:r1, vs] = jnp.dot(lhs, rhs, preferred_element_type=F32)
        yield W_OUT

    for c, r0, r1 in chunks:
        for h, ks, vs in heads:
            o = z_scr[r0:r1, vs]
            ms = jnp.mean(o * o, axis=-1, keepdims=True)
            on = o * lax.rsqrt(ms + EPS) * gng
            gg = z_scr[r0:r1, lay.off_gg + h * dv:lay.off_gg + (h + 1) * dv]
            og_scr[r0:r1, vs] = (on * _silu(gg)).astype(BF16)
        yield W_NORM

    for c, r0, r1 in chunks:
        for g0 in range(0, cw, LANES):
            gl = slice(g0, g0 + LANES)
            p = (z_scr[r0:r1, lay.off_cc + g0:lay.off_cc + g0 + LANES]
                 * z_scr[r0:r1, lay.off_ch + g0:lay.off_ch + g0 + LANES])
            p_scr[halo + r0:halo + r1, gl] = p
            ext = jnp.concatenate([p_scr[r0:halo + r0, gl], p], axis=0)
            yc = cw_ref[ck - 1:ck, gl] * p
            for i in range(ck - 1):
                lag = ck - 1 - i
                yc = yc + cw_ref[i:i + 1, gl] * pltpu.roll(ext, lag, 0)[halo:, :]
            yc = (z_scr[r0:r1, lay.off_cb + g0:lay.off_cb + g0 + LANES] * yc
                  * _silu(z_scr[r0:r1, lay.off_gc + g0:lay.off_gc + g0 + LANES]))
            yc_scr[r0:r1, gl] = yc.astype(BF16)
        yield W_CONV

    p_scr[0:halo, :] = p_scr[tile:tile + halo, :]

    z_scr[:, 0:d] = jnp.dot(og_scr[...], _unpack_rows(w_pa_ref[...]), preferred_element_type=F32)
    yield 0
    z_scr[:, d:2 * d] = jnp.dot(yc_scr[...], _unpack_rows(w_pb_ref[...]),
                                preferred_element_type=F32)
    yield 0
    for c, r0, r1 in chunks:
        for g0 in range(0, d, 2 * LANES):
            g1 = g0 + 2 * LANES
            merged = (jax.nn.sigmoid(z_scr[r0:r1, lay.off_mg + g0:lay.off_mg + g1])
                      * z_scr[r0:r1, g0:g1]
                      + jax.nn.sigmoid(z_scr[r0:r1, lay.off_mc + g0:lay.off_mc + g1])
                      * z_scr[r0:r1, d + g0:d + g1])
            mg_scr[r0:r1, g0:g1] = merged.astype(BF16)
        yield W_MERGE
    z_scr[:, 2 * d:3 * d] = jnp.dot(mg_scr[...], _unpack_rows(w_o_ref[...]),
                                    preferred_element_type=F32)
    yield 0

    lng = lng_ref[...]
    lnb = lnb_ref[...]
    for c, r0, r1 in chunks:
        hres = alpha * x_ref[0, row0 + r0:row0 + r1, :] + gate * z_scr[r0:r1, 2 * d:3 * d]
        mu = jnp.mean(hres, axis=-1, keepdims=True)
        dev = hres - mu
        var = jnp.mean(dev * dev, axis=-1, keepdims=True)
        y_ref[0, row0 + r0:row0 + r1, :] = dev * lax.rsqrt(var + EPS) * lng + lnb
        yield W_LN


def _emit_overlapped(mix, mix_weights, premix, premix_weights, proj, n_proj, premix_needs_proj,
                     fillers):
    head, tail = mix_weights
    order = [("mix", w) for w in head]
    done = 0
    for i, w in enumerate(tail):
        while done * len(tail) < i * len(premix_weights):
            order.append(("premix", premix_weights[done]))
            done += 1
        order.append(("mix", w))
    order += [("premix", w) for w in premix_weights[done:]]

    total = sum(w for _, w in order)
    matmul_slots = sum(1 for _, w in order if w == 0)
    done_w = done_p = done_f = slot = 0
    for source, weight in order:
        done_w += weight
        while done_p * total < done_w * n_proj or (source == "premix"
                                                    and done_p < premix_needs_proj):
            next(proj)
            done_p += 1
        if weight == 0:
            slot += 1
            while done_f * matmul_slots < slot * len(fillers):
                fillers[done_f]()
                done_f += 1
        assert next(mix if source == "mix" else premix) == weight
    while done_p < n_proj:
        next(proj)
        done_p += 1
    assert done_f == len(fillers) and next(mix, None) is None and next(premix, None) is None


def _step_row_reader(ref, rows, block, sub, steps_per_block):
    base = pl.multiple_of(block * SUBLANES, SUBLANES)

    def read(i, lanes):
        blk = ref[pl.ds(base, SUBLANES), lanes]
        val = blk[i:i + 1, :]
        for s in range(1, steps_per_block):
            val = jnp.where(sub == s, blk[s * rows + i:s * rows + i + 1, :], val)
        return val
    return read


def _project_samples(x_ref, mod_ref, w_in_refs, w_a2_ref, b_a_ref, z_scr, a_scr, qkv_scr,
                     z_hbm_ref, sem, lay):
    n, d = x_ref.shape
    shift = mod_ref[:, 0:d]
    scale = mod_ref[:, d:2 * d]
    u = (x_ref[...] * (1.0 + scale) + shift).astype(BF16)
    for piece, s0, s1, d0 in lay.groups:
        z_scr[0:n, d0:d0 + s1 - s0] = jnp.dot(u, _unpack_rows(w_in_refs[piece][:, s0:s1]),
                                              preferred_element_type=F32)
    copy = pltpu.make_async_copy(z_scr.at[pl.ds(0, n)], z_hbm_ref, sem)
    copy.start()
    a_lr = z_scr[0:n, lay.off_a:lay.off_a + lay.rank_pad].astype(BF16)
    la = jnp.dot(a_lr, _unpack_rows(w_a2_ref[...]), preferred_element_type=F32) + b_a_ref[...]
    a_scr[...] = jnp.exp(_log_sigmoid(la) * (1.0 / GLA_TAU))
    qkv_scr[...] = z_scr[0:n, 0:lay.off_gg]
    copy.wait()


def _sample_state_step(i, a_row, qkv_row, s_ref, s_out_ref, o_ref, lay):
    h_n, dk, dv = lay.h, lay.dk, lay.dv
    q_scale = dk ** -0.5
    rows = [a_row(i, slice(h * dk, (h + 1) * dk)) for h in range(h_n)]
    rows += [qkv_row(i, slice(lay.off_k + h * dk, lay.off_k + (h + 1) * dk)) for h in range(h_n)]
    rows += [qkv_row(i, slice(lay.off_q + h * dk, lay.off_q + (h + 1) * dk)) for h in range(h_n)]
    rows.append(jnp.zeros((dk - len(rows), dk), F32))
    mt = jnp.transpose(jnp.concatenate(rows, axis=0))
    for h in range(h_n):
        a_col = mt[:, h:h + 1]
        k_col = mt[:, h_n + h:h_n + h + 1]
        q_col = mt[:, 2 * h_n + h:2 * h_n + h + 1]
        v = qkv_row(i, slice(lay.off_v + h * dv, lay.off_v + (h + 1) * dv))
        s_new = a_col * s_ref[i, h] + k_col * v
        s_out_ref[i, h] = s_new
        o_ref[i:i + 1, h * dv:(h + 1) * dv] = q_scale * jnp.sum(q_col * s_new, axis=0,
                                                               keepdims=True)


def _prompt_kernel(xc_ref, xn_ref, mod_ref, xs_ref, mods_ref, sst_ref,
                   w_hd_ref, w_lr_ref, w_tl_ref, w_a2_ref, b_a_ref,
                   gng_ref, cw_ref, w_pa_ref, w_pb_ref, w_o_ref, lng_ref, lnb_ref, *rest,
                   lay, layer, tile, alpha, n_t, n_tiles, sample_rows):
    (y_ref, sg_ref, sc_ref, sso_ref, so_ref, zs_hbm_ref,
     ua_scr, ub_scr, za_scr, zb_scr, la_scr, p_scr, s_scr, og_scr, yc_scr, mg_scr,
     qin_scr, kin_scr, kst_scr, vb_scr, dec_scr, att_scr, ds_scr, sbf_scr,
     sa_scr, sqkv_scr, zs_sem) = rest[-27:]
    b_a_ref, gng_ref, lng_ref, lnb_ref = (_layer_rows(r, layer)
                                          for r in (b_a_ref, gng_ref, lng_ref, lnb_ref))
    cw_ref = _layer_taps(cw_ref, layer)
    j = pl.program_id(0)
    steps_per_seq = n_t // 2
    jj = lax.rem(j, steps_per_seq)
    b_cur = lax.div(j, steps_per_seq)
    b_next = lax.div(jnp.minimum(2 * j + 2, n_tiles - 1), n_t)
    d = lay.d
    ck = cw_ref.shape[0]
    halo = SUBLANES
    w_in_refs = (w_hd_ref, w_lr_ref, w_tl_ref)

    @pl.when(jj == 0)
    def _():
        s_scr[...] = jnp.zeros_like(s_scr)
        p_scr[0:halo, :] = jnp.zeros((halo, lay.cw), F32)

    def project(x_rows, brow, u_scr, z_dst):
        return _project_stages(x_rows, mod_ref, brow, w_in_refs, u_scr, z_dst, lay)

    operands = (qin_scr, kin_scr, kst_scr, vb_scr, dec_scr)

    def premix(z_scr):
        return _premix_stages(z_scr, w_a2_ref, b_a_ref, la_scr, *operands, lay, tile)

    def mix(z_scr, row0):
        return _mix_stages(z_scr, xc_ref, y_ref, row0, gate, gng_ref, cw_ref, w_pa_ref, w_pb_ref,
                           w_o_ref, lng_ref, lnb_ref, p_scr, s_scr, og_scr, yc_scr, mg_scr,
                           *operands, att_scr, ds_scr, sbf_scr, lay, tile, alpha)

    @pl.when(j == 0)
    def _():
        _project_samples(xs_ref, mods_ref, w_in_refs, w_a2_ref, b_a_ref, za_scr, sa_scr,
                         sqkv_scr, zs_hbm_ref, zs_sem, lay)
        for _ in project(lambda: xc_ref[0, 0:tile, :], b_cur, ua_scr, za_scr):
            pass
        for _ in premix(za_scr):
            pass

    gate = mod_ref[pl.ds(b_cur, 1), 2 * d:3 * d]
    mix_weights = _mix_weights(lay, tile)
    premix_weights = _premix_weights(lay, tile)
    plan = _project_plan(lay)
    n_proj = 1 + len(plan)
    premix_needs_proj = 1 + sum(1 for _, _, _, d0 in plan if d0 < lay.off_cb)

    steps_per_block = SUBLANES // sample_rows
    sub = lax.rem(j, steps_per_block)
    block = lax.div(j, steps_per_block)
    a_row = _step_row_reader(sa_scr, sample_rows, block, sub, steps_per_block)
    qkv_row = _step_row_reader(sqkv_scr, sample_rows, block, sub, steps_per_block)
    fillers = [functools.partial(_sample_state_step, i, a_row, qkv_row, sst_ref,
                                 sso_ref, so_ref, lay) for i in range(sample_rows)]
    half = sample_rows // 2

    _emit_overlapped(mix(za_scr, 0), mix_weights, premix(zb_scr), premix_weights,
                     project(lambda: xc_ref[0, tile:2 * tile, :], b_cur, ub_scr, zb_scr), n_proj,
                     premix_needs_proj, fillers[:half])
    _emit_overlapped(mix(zb_scr, tile), mix_weights, premix(za_scr), premix_weights,
                     project(lambda: xn_ref[0], b_next, ua_scr, za_scr), n_proj,
                     premix_needs_proj, fillers[half:])

    @pl.when(jj == steps_per_seq - 1)
    def _():
        sg_ref[0] = s_scr[...]
        sc_ref[0] = p_scr[halo - (ck - 1):halo, :]


def _prompt_vmem_bytes(lay, tile, sample_rows, n_s):
    weights = 2 * (lay.d * lay.n_pad + lay.rank_pad * lay.qk + lay.vt * lay.d
                   + lay.cw * lay.d + lay.d * lay.d)
    state = lay.h * lay.dk * lay.dv * 4
    io = (2 * (2 * tile + tile + 2 * tile) * lay.d * 4 + 2 * state
          + 2 * 2 * sample_rows * state + n_s * (3 * lay.d + lay.qk + lay.off_gg) * 4)
    chunk_heads = (tile // GLA_CHUNK) * lay.h
    scratch = (2 * tile * lay.d * 2 + 2 * tile * lay.n_pad * 4 + tile * lay.qk * 4
               + (tile + SUBLANES) * lay.cw * 4 + lay.h * lay.dk * lay.dv * 4
               + tile * lay.vt * 2 + tile * lay.cw * 2 + tile * lay.d * 2
               + 3 * tile * lay.qk * 2 + tile * lay.vt * 2 + SUBLANES * lay.qk * 4
               + lay.h * tile * LANES * 2 + chunk_heads * lay.dk * lay.dv * 6)
    temporaries = tile * lay.n_pad * 2
    return min(weights + io + scratch + temporaries, VMEM_PHYSICAL_BYTES * 15 // 16)


def _prompt_layer(x, mod_p, x_sample, mod_s, sample_state, params, prev_states, layer, lay,
                  alpha):
    (w_hd, w_lr, w_tl, w_a2, b_a, gng, conv_w, w_pa, w_pb, w_o, ln_g, ln_b) = params
    bsz, seq, d = x.shape
    depth = w_hd.shape[0]
    tile = PROMPT_TILE
    assert seq % (2 * tile) == 0 and tile % GLA_CHUNK == 0
    ck = conv_w.shape[0]
    n_t = seq // tile
    n_tiles = bsz * n_t
    n_steps = n_tiles // 2
    steps_per_seq = n_t // 2
    n_chunk_heads = (tile // GLA_CHUNK) * lay.h
    assert tile // GLA_CHUNK <= SUBLANES
    n_s = sample_state.shape[1]
    sample_rows = n_s // n_steps
    assert sample_rows * n_steps == n_s and sample_rows % 2 == 0 and SUBLANES % sample_rows == 0
    steps_per_block = SUBLANES // sample_rows

    def weight(arr):
        return _layer_spec(arr.shape[1:], layer, True)

    def next_tile(j):
        g = jnp.minimum(2 * j + 2, n_tiles - 1)
        return (g // n_t, g % n_t, 0)

    sample_state_spec = pl.BlockSpec((None, sample_rows, lay.h, lay.dk, lay.dv),
                                     lambda j: (layer, j, 0, 0, 0))
    in_specs = [
        pl.BlockSpec((1, 2 * tile, d), lambda j: (j // steps_per_seq, j % steps_per_seq, 0)),
        pl.BlockSpec((1, tile, d), next_tile),
        _layer_spec(mod_p.shape[1:], layer),
        pl.BlockSpec((n_s, None, d), lambda j: (0, 0, 0), pipeline_mode=pl.Buffered(1)),
        pl.BlockSpec((None, n_s, 2 * d), lambda j: (layer, 0, 0), pipeline_mode=pl.Buffered(1)),
        sample_state_spec,
        weight(w_hd), weight(w_lr), weight(w_tl), weight(w_a2), _whole_spec(b_a),
        _whole_spec(gng), _whole_spec(conv_w), weight(w_pa), weight(w_pb), weight(w_o),
        _whole_spec(ln_g), _whole_spec(ln_b),
    ]
    args = [x, x, mod_p, x_sample, mod_s, sample_state, *params]
    aliases = {}
    if prev_states is not None:
        for i, prev in enumerate(prev_states):
            in_specs.append(pl.BlockSpec(memory_space=pl.ANY))
            aliases[len(args)] = 1 + i
            args.append(prev)

    kern = functools.partial(_prompt_kernel, lay=lay, layer=layer, tile=tile, alpha=alpha,
                             n_t=n_t, n_tiles=n_tiles, sample_rows=sample_rows)
    return pl.pallas_call(
        kern,
        grid=(n_steps,),
        in_specs=in_specs,
        out_specs=[
            pl.BlockSpec((1, 2 * tile, d), lambda j: (j // steps_per_seq, j % steps_per_seq, 0)),
            pl.BlockSpec((None, 1, lay.h, lay.dk, lay.dv),
                         lambda j: (layer, j // steps_per_seq, 0, 0, 0)),
            pl.BlockSpec((None, 1, ck - 1, lay.cw), lambda j: (layer, j // steps_per_seq, 0, 0)),
            sample_state_spec,
            pl.BlockSpec((None, sample_rows, lay.vt), lambda j: (j, 0, 0)),
            pl.BlockSpec(memory_space=pl.ANY),
        ],
        out_shape=[
            jax.ShapeDtypeStruct((bsz, seq, d), F32),
            jax.ShapeDtypeStruct((depth, bsz, lay.h, lay.dk, lay.dv), F32),
            jax.ShapeDtypeStruct((depth, bsz, ck - 1, lay.cw), F32),
            jax.ShapeDtypeStruct(sample_state.shape, F32),
            jax.ShapeDtypeStruct((n_steps, sample_rows, lay.vt), F32),
            jax.ShapeDtypeStruct((n_s, lay.n_pad), F32),
        ],
        scratch_shapes=[
            pltpu.VMEM((tile, d), BF16),
            pltpu.VMEM((tile, d), BF16),
            pltpu.VMEM((tile, lay.n_pad), F32),
            pltpu.VMEM((tile, lay.n_pad), F32),
            pltpu.VMEM((tile, lay.qk), F32),
            pltpu.VMEM((tile + SUBLANES, lay.cw), F32),
            pltpu.VMEM((lay.h, lay.dk, lay.dv), F32),
            pltpu.VMEM((tile, lay.vt), BF16),
            pltpu.VMEM((tile, lay.cw), BF16),
            pltpu.VMEM((tile, d), BF16),
            pltpu.VMEM((tile, lay.qk), BF16),
            pltpu.VMEM((tile, lay.qk), BF16),
            pltpu.VMEM((tile, lay.qk), BF16),
            pltpu.VMEM((tile, lay.vt), BF16),
            pltpu.VMEM((SUBLANES, lay.qk), F32),
            pltpu.VMEM((lay.h, tile, GLA_CHUNK), BF16),
            pltpu.VMEM((n_chunk_heads, lay.dk, lay.dv), F32),
            pltpu.VMEM((n_chunk_heads, lay.dk, lay.dv), BF16),
            pltpu.VMEM((n_s, lay.qk), F32),
            pltpu.VMEM((n_s, lay.off_gg), F32),
            pltpu.SemaphoreType.DMA(()),
        ],
        input_output_aliases=aliases,
        compiler_params=pltpu.CompilerParams(
            dimension_semantics=("arbitrary",),
            vmem_limit_bytes=_prompt_vmem_bytes(lay, tile, sample_rows, n_s)),
        name="prompt_layer",
    )(*args)


def _sample_out_kernel(o_ref, z_ref, x_ref, mod_ref, buf_ref, gng_ref, cw_ref,
                       w_pa_ref, w_pb_ref, w_o_ref, lng_ref, lnb_ref, *rest, lay, layer, alpha):
    y_ref, buf_out_ref, og_scr = rest[-3:]
    d, h_n, dv, cw = lay.d, lay.h, lay.dv, lay.cw
    gng_ref, lng_ref, lnb_ref = (_layer_rows(r, layer) for r in (gng_ref, lng_ref, lnb_ref))
    cw_ref = _layer_taps(cw_ref, layer)
    ck = cw_ref.shape[0]
    gate = mod_ref[:, 2 * d:3 * d]
    gng = gng_ref[...]
    for h in range(h_n):
        o = o_ref[:, h * dv:(h + 1) * dv]
        ms = jnp.mean(o * o, axis=-1, keepdims=True)
        on = o * lax.rsqrt(ms + EPS) * gng
        gg = z_ref[:, lay.off_gg + h * dv:lay.off_gg + (h + 1) * dv]
        og_scr[:, h * dv:(h + 1) * dv] = (on * _silu(gg)).astype(BF16)
    pa = jnp.dot(og_scr[...], _unpack_rows(w_pa_ref[...]), preferred_element_type=F32)

    p = z_ref[:, lay.off_cc:lay.off_cc + cw] * z_ref[:, lay.off_ch:lay.off_ch + cw]
    yc = cw_ref[ck - 1:ck, :] * p
    for i in range(ck - 1):
        yc = yc + cw_ref[i:i + 1, :] * buf_ref[:, i * cw:(i + 1) * cw]
    yc = z_ref[:, lay.off_cb:lay.off_cb + cw] * yc * _silu(z_ref[:, lay.off_gc:lay.off_gc + cw])
    pb = jnp.dot(yc.astype(BF16), _unpack_rows(w_pb_ref[...]), preferred_element_type=F32)
    for i in range(ck - 2):
        buf_out_ref[:, i * cw:(i + 1) * cw] = buf_ref[:, (i + 1) * cw:(i + 2) * cw]
    buf_out_ref[:, (ck - 2) * cw:(ck - 1) * cw] = p

    merged = (jax.nn.sigmoid(z_ref[:, lay.off_mg:lay.off_mg + d]) * pa
              + jax.nn.sigmoid(z_ref[:, lay.off_mc:lay.off_mc + d]) * pb)
    out = jnp.dot(merged.astype(BF16), _unpack_rows(w_o_ref[...]),
                  preferred_element_type=F32)
    hres = alpha * x_ref[...] + gate * out
    mu = jnp.mean(hres, axis=-1, keepdims=True)
    dev = hres - mu
    var = jnp.mean(dev * dev, axis=-1, keepdims=True)
    y_ref[...] = dev * lax.rsqrt(var + EPS) * lng_ref[...] + lnb_ref[...]


def _sample_out(o, z, x3, mod_s, conv_state, prev_conv_out, gng, conv_w, w_pa, w_pb, w_o,
                ln_g, ln_b, layer, lay, alpha):
    n, _, d = x3.shape
    kern = functools.partial(_sample_out_kernel, lay=lay, layer=layer, alpha=alpha)
    conv_spec = _layer_spec(conv_state.shape[1:], layer)
    in_specs = [_whole_spec(o), _whole_spec(z), pl.BlockSpec((n, None, d), lambda i: (0, 0, 0)),
                _layer_spec(mod_s.shape[1:], layer), conv_spec,
                _whole_spec(gng), _whole_spec(conv_w), _layer_spec(w_pa.shape[1:], layer),
                _layer_spec(w_pb.shape[1:], layer), _layer_spec(w_o.shape[1:], layer),
                _whole_spec(ln_g), _whole_spec(ln_b)]
    args = [o, z, x3, mod_s, conv_state, gng, conv_w, w_pa, w_pb, w_o, ln_g, ln_b]
    aliases = {}
    if prev_conv_out is not None:
        in_specs.append(pl.BlockSpec(memory_space=pl.ANY))
        aliases = {len(args): 1}
        args.append(prev_conv_out)
    return pl.pallas_call(
        kern,
        grid=(1,),
        in_specs=in_specs,
        out_specs=[pl.BlockSpec((n, None, d), lambda i: (0, 0, 0)), conv_spec],
        out_shape=[jax.ShapeDtypeStruct((n, 1, d), F32),
                   jax.ShapeDtypeStruct(conv_state.shape, F32)],
        scratch_shapes=[pltpu.VMEM((n, lay.vt), BF16)],
        input_output_aliases=aliases,
        compiler_params=pltpu.CompilerParams(dimension_semantics=("arbitrary",),
                                             vmem_limit_bytes=STREAMING_VMEM_BYTES),
        name="sample_out",
    )(*args)


def kernel(x_prompt, x_sample, c_prompt, c_sample, state_gla, state_conv, w_ada, b_ada, w_in,
           w_a2, b_a, gla_norm_g, conv_w, w_pa, w_pb, w_o, ln_g, ln_b):
    depth, n_s, heads, dk, dv = state_gla.shape
    d = x_prompt.shape[2]
    rank = w_a2.shape[1]
    ck, cw = conv_w.shape[1:]
    lay = _Layout(d, heads, dk, dv, rank, cw)
    alpha = (2 * depth) ** 0.25

    mod_p, mod_s = _ada_mod(c_prompt, c_sample, w_ada, b_ada)

    w_hd, w_lr, w_tl = _split_w_in(w_in, lay)
    w_pa_b, w_pb_b, w_o_b, w_a2_p = _cast_weights((w_pa, w_pb, w_o), w_a2, lay.rank_pad)
    taps = jnp.swapaxes(conv_w, 0, 1)
    params = (w_hd, w_lr, w_tl, w_a2_p, b_a, gla_norm_g, taps, w_pa_b, w_pb_b, w_o_b, ln_g, ln_b)

    hp, hs = x_prompt, x_sample
    states = None
    conv_in = state_conv.reshape(depth, n_s, (ck - 1) * cw)
    conv_s = None
    for l in range(depth):
        hp, *states, o, z = _prompt_layer(hp, mod_p, hs, mod_s, state_gla, params, states, l, lay,
                                          alpha)
        hs, conv_s = _sample_out(o.reshape(n_s, lay.vt), z, hs, mod_s, conv_in, conv_s,
                                 gla_norm_g, taps, w_pa_b, w_pb_b, w_o_b, ln_g, ln_b, l, lay,
                                 alpha)
    return (hp, hs, states[0], states[1], states[2], conv_s.reshape(state_conv.shape))
```

```python
import functools

import jax
import jax.numpy as jnp
from jax import lax
from jax.experimental import pallas as pl
from jax.experimental.pallas import tpu as pltpu

F32 = jnp.float32
BF16 = jnp.bfloat16

GLA_TAU = 16.0
GLA_CHUNK = 64
EPS = 1e-5

LANES = 128
SUBLANES = 8
VMEM_PHYSICAL_BYTES = 64 * 1024 * 1024
MIB = 1024 * 1024
STREAMING_VMEM_BYTES = VMEM_PHYSICAL_BYTES * 7 // 8

PROMPT_TILE = 256
PROJ_COLS = 256
PREP_STEPS = 2
ADA_STEPS = 2
CAST_STEPS = 4


LOG2_E = 1.4426950408889634


def _log_sigmoid(x):
    return jnp.minimum(x, 0.0) - jnp.log(1.0 + jnp.exp(-jnp.abs(x)))


def _cumsum_rows(x):
    row = lax.broadcasted_iota(jnp.int32, (SUBLANES, x.shape[1]), 0)
    groups = []
    carry = None
    for g in range(x.shape[0] // SUBLANES):
        t = x[g * SUBLANES:(g + 1) * SUBLANES, :]
        s = 1
        while s < SUBLANES:
            t = t + jnp.where(row >= s, pltpu.roll(t, s, 0), 0.0)
            s *= 2
        if carry is not None:
            t = t + carry
        carry = jnp.broadcast_to(t[SUBLANES - 1:SUBLANES, :], t.shape)
        groups.append(t)
    return jnp.concatenate(groups, axis=0)


def _silu(x):
    return x * jax.nn.sigmoid(x)


PACKED = jnp.uint32
PACK = 2


def _pack_rows(w_bf16):
    return pltpu.bitcast(w_bf16, PACKED)


def _unpack_rows(w_packed):
    return pltpu.bitcast(w_packed, BF16)


class _Layout:
    def __init__(self, d_model, heads, dk, dv, rank, conv_w):
        self.d = d_model
        self.h = heads
        self.dk = dk
        self.dv = dv
        self.rank = rank
        self.cw = conv_w
        self.qk = heads * dk
        self.vt = heads * dv
        self.rank_pad = -(-rank // LANES) * LANES
        self.off_q = 0
        self.off_k = self.off_q + self.qk
        self.off_v = self.off_k + self.qk
        self.off_gg = self.off_v + self.vt
        self.off_a = self.off_gg + self.vt
        self.off_cb = self.off_a + self.rank_pad
        self.off_cc = self.off_cb + conv_w
        self.off_ch = self.off_cc + conv_w
        self.off_gc = self.off_ch + conv_w
        self.off_mg = self.off_gc + conv_w
        self.off_mc = self.off_mg + d_model
        self.n_pad = self.off_mc + d_model
        self.n_head = self.off_a
        self.n_tail = self.n_pad - self.off_cb
        self.groups = (
            (0, self.off_q, self.off_v, self.off_q),
            (0, self.off_v, self.off_gg, self.off_v),
            (0, self.off_gg, self.off_a, self.off_gg),
            (1, 0, self.rank_pad, self.off_a),
            (2, 0, 2 * conv_w, self.off_cb),
            (2, 2 * conv_w, 4 * conv_w, self.off_ch),
            (2, 4 * conv_w, 4 * conv_w + d_model, self.off_mg),
            (2, 4 * conv_w + d_model, self.n_tail, self.off_mc))


def _layer_spec(shape, layer, single_buffer=False):
    zeros = (0,) * len(shape)
    kwargs = {"pipeline_mode": pl.Buffered(1)} if single_buffer else {}
    return pl.BlockSpec((None,) + tuple(shape), lambda *_: (layer,) + zeros, **kwargs)


def _whole_spec(arr):
    return pl.BlockSpec(arr.shape, lambda *_: (0,) * arr.ndim)


def _layer_rows(ref, layer):
    return ref.at[pl.ds(layer, 1)]


def _layer_taps(ref, layer):
    return ref.at[:, layer]


def _transpose_cast_kernel(w_ref, o_ref, *, valid):
    wt = jnp.transpose(w_ref[0])
    if valid < wt.shape[1]:
        lane = lax.broadcasted_iota(jnp.int32, wt.shape, 1)
        wt = jnp.where(lane < valid, wt, 0.0)
    o_ref[...] = _pack_rows(wt.astype(BF16))


def _transpose_cast(w_t, col0, n_valid, n_out, cols):
    depth, n_in, d = w_t.shape
    assert n_out % cols == 0 and col0 % SUBLANES == 0 and col0 + n_out <= n_in
    kern = functools.partial(_transpose_cast_kernel, valid=min(n_valid, cols))
    return pl.pallas_call(
        kern,
        grid=(depth, n_out // cols),
        in_specs=[pl.BlockSpec((pl.Element(1), pl.Element(cols), pl.Element(d)),
                               lambda l, j: (l, pl.multiple_of(col0 + j * cols, SUBLANES), 0))],
        out_specs=pl.BlockSpec((None, d // PACK, cols), lambda l, j: (l, 0, j)),
        out_shape=jax.ShapeDtypeStruct((depth, d // PACK, n_out), PACKED),
        compiler_params=pltpu.CompilerParams(
            dimension_semantics=("arbitrary", "arbitrary"),
            vmem_limit_bytes=STREAMING_VMEM_BYTES),
        name="transpose_cast",
    )(w_t)


def _split_w_in(w_in, lay):
    n_in = w_in.shape[2]
    a_end = lay.off_a + lay.rank
    assert n_in == a_end + lay.n_tail and lay.rank <= lay.rank_pad == LANES
    w_t = jnp.swapaxes(w_in, 1, 2)
    head = _transpose_cast(w_t, 0, lay.n_head, lay.n_head, lay.n_head // PREP_STEPS)
    low = _transpose_cast(w_t, lay.off_a, lay.rank, lay.rank_pad, lay.rank_pad)
    tail = _transpose_cast(w_t, a_end, lay.n_tail, lay.n_tail, lay.n_tail // PREP_STEPS)
    return head, low, tail


def _cast_kernel(*refs, rank_pad):
    n = len(refs) // 2
    w_a2_ref, o_a2_ref = refs[n - 1], refs[-1]
    for w_ref, o_ref in zip(refs[:n - 1], refs[n:-1]):
        o_ref[...] = _pack_rows(w_ref[...].astype(BF16))
    w_a2 = w_a2_ref[...]
    pad = jnp.zeros((rank_pad - w_a2.shape[0], w_a2.shape[1]), F32)
    o_a2_ref[...] = _pack_rows(jnp.concatenate([w_a2, pad], axis=0).astype(BF16))


def _cast_weights(weights, w_a2, rank_pad):
    depth = w_a2.shape[0]
    for w in weights:
        assert w.shape[1] % (CAST_STEPS * PACK * SUBLANES) == 0

    def rows_spec(rows, cols):
        return pl.BlockSpec((None, rows, cols), lambda l, r: (l, r, 0))

    def layer_spec(rows, cols):
        return pl.BlockSpec((None, rows, cols), lambda l, r: (l, 0, 0))

    kern = functools.partial(_cast_kernel, rank_pad=rank_pad)
    return pl.pallas_call(
        kern,
        grid=(depth, CAST_STEPS),
        in_specs=[rows_spec(w.shape[1] // CAST_STEPS, w.shape[2]) for w in weights]
        + [layer_spec(*w_a2.shape[1:])],
        out_specs=[rows_spec(w.shape[1] // (CAST_STEPS * PACK), w.shape[2]) for w in weights]
        + [layer_spec(rank_pad // PACK, w_a2.shape[2])],
        out_shape=[jax.ShapeDtypeStruct((depth, w.shape[1] // PACK, w.shape[2]), PACKED)
                   for w in weights]
        + [jax.ShapeDtypeStruct((depth, rank_pad // PACK, w_a2.shape[2]), PACKED)],
        compiler_params=pltpu.CompilerParams(
            dimension_semantics=("arbitrary", "arbitrary"),
            vmem_limit_bytes=STREAMING_VMEM_BYTES),
        name="cast_weights",
    )(*weights, w_a2)


def _ada_kernel(cp_ref, cs_ref, w_ref, b_ref, op_ref, os_ref):
    w = w_ref[...].astype(BF16)
    b = b_ref[...]
    op_ref[...] = jnp.dot(cp_ref[...].astype(BF16), w, preferred_element_type=F32) + b
    os_ref[...] = jnp.dot(cs_ref[...].astype(BF16), w, preferred_element_type=F32) + b


def _ada_mod(c_prompt, c_sample, w_ada, b_ada):
    depth, d, n3 = w_ada.shape
    bp, ns = c_prompt.shape[0], c_sample.shape[0]
    tn = n3 // ADA_STEPS
    return pl.pallas_call(
        _ada_kernel,
        grid=(depth, n3 // tn),
        in_specs=[
            pl.BlockSpec((bp, d), lambda l, j: (0, 0)),
            pl.BlockSpec((ns, d), lambda l, j: (0, 0)),
            pl.BlockSpec((None, d, tn), lambda l, j: (l, 0, j)),
            pl.BlockSpec((None, 1, tn), lambda l, j: (l, 0, j)),
        ],
        out_specs=[pl.BlockSpec((None, bp, tn), lambda l, j: (l, 0, j)),
                   pl.BlockSpec((None, ns, tn), lambda l, j: (l, 0, j))],
        out_shape=[jax.ShapeDtypeStruct((depth, bp, n3), F32),
                   jax.ShapeDtypeStruct((depth, ns, n3), F32)],
        compiler_params=pltpu.CompilerParams(
            dimension_semantics=("arbitrary", "arbitrary"),
            vmem_limit_bytes=STREAMING_VMEM_BYTES),
        name="ada_mod",
    )(c_prompt, c_sample, w_ada, b_ada.reshape(depth, 1, n3))


def _project_plan(lay):
    plan = []
    for piece, s0, s1, d0 in lay.groups:
        for c0 in range(s0, s1, PROJ_COLS):
            plan.append((piece, c0, min(c0 + PROJ_COLS, s1), d0 + c0 - s0))
    return plan


def _project_stages(x_rows, mod_ref, brow, w_in_refs, u_scr, z_dst, lay):
    d = lay.d
    shift = mod_ref[pl.ds(brow, 1), 0:d]
    scale = mod_ref[pl.ds(brow, 1), d:2 * d]
    u_scr[...] = (x_rows() * (1.0 + scale) + shift).astype(BF16)
    yield
    for piece, c0, c1, d0 in _project_plan(lay):
        z_dst[:, d0:d0 + c1 - c0] = jnp.dot(u_scr[...],
                                            _unpack_rows(w_in_refs[piece][:, c0:c1]),
                                            preferred_element_type=F32)
        yield


def _tile_chunks(tile):
    return [(c, c * GLA_CHUNK, (c + 1) * GLA_CHUNK) for c in range(tile // GLA_CHUNK)]


def _tile_heads(lay):
    return [(h, slice(h * lay.dk, (h + 1) * lay.dk), slice(h * lay.dv, (h + 1) * lay.dv))
            for h in range(lay.h)]


W_DECAY, W_FACTORS, W_SCORES, W_INCR, W_RECUR, W_OUT, W_NORM, W_CONV, W_MERGE, W_LN = (
    4, 4, 1, 1, 5, 1, 11, 10, 10, 10)


def _premix_weights(lay, tile):
    return [W_DECAY] + [W_FACTORS] * (len(_tile_chunks(tile)) * lay.h)


def _premix_stages(z_scr, w_a2_ref, b_a_ref, la_scr, qin_scr, kin_scr, kst_scr, vb_scr, dec_scr,
                   lay, tile):
    dk, dv = lay.dk, lay.dv
    q_scale = dk ** -0.5

    a_lr = z_scr[:, lay.off_a:lay.off_a + lay.rank_pad].astype(BF16)
    la_scr[...] = (jnp.dot(a_lr, _unpack_rows(w_a2_ref[...]), preferred_element_type=F32)
                   + b_a_ref[...])
    yield W_DECAY

    for c, r0, r1 in _tile_chunks(tile):
        for h, ks, vs in _tile_heads(lay):
            log_a = _log_sigmoid(la_scr[r0:r1, ks]) * (1.0 / GLA_TAU)
            b2 = _cumsum_rows(log_a) * LOG2_E
            bl2 = b2[GLA_CHUNK - 1:GLA_CHUNK, :]
            q_in = z_scr[r0:r1, lay.off_q + h * dk:lay.off_q + (h + 1) * dk] * q_scale
            k = z_scr[r0:r1, lay.off_k + h * dk:lay.off_k + (h + 1) * dk]
            qin_scr[r0:r1, ks] = (q_in * jnp.exp2(b2)).astype(BF16)
            kin_scr[r0:r1, ks] = (k * jnp.exp2(-b2)).astype(BF16)
            kst_scr[r0:r1, ks] = (k * jnp.exp2(bl2 - b2)).astype(BF16)
            dec_scr[c:c + 1, ks] = jnp.exp2(bl2)
            vb_scr[r0:r1, vs] = z_scr[r0:r1, lay.off_v + h * dv:lay.off_v + (h + 1) * dv] \
                .astype(BF16)
            yield W_FACTORS


def _mix_weights(lay, tile):
    chunks = len(_tile_chunks(tile))
    head = ([W_SCORES] * chunks + [W_INCR] * lay.h + [W_RECUR] * lay.h + [W_OUT] * chunks
            + [W_NORM] * chunks + [W_CONV] * chunks)
    tail = [0, 0] + [W_MERGE] * chunks + [0] + [W_LN] * chunks
    return head, tail


def _mix_stages(z_scr, x_ref, y_ref, row0, gate, gng_ref, cw_ref,
                w_pa_ref, w_pb_ref, w_o_ref, lng_ref, lnb_ref,
                p_scr, s_scr, og_scr, yc_scr, mg_scr,
                qin_scr, kin_scr, kst_scr, vb_scr, dec_scr, att_scr, ds_scr, sbf_scr,
                lay, tile, alpha):
    d, h_n, dk, dv, cw = lay.d, lay.h, lay.dk, lay.dv, lay.cw
    ck = cw_ref.shape[0]
    halo = SUBLANES
    chunks = _tile_chunks(tile)
    heads = _tile_heads(lay)
    causal = (lax.broadcasted_iota(jnp.int32, (GLA_CHUNK, GLA_CHUNK), 0)
              >= lax.broadcasted_iota(jnp.int32, (GLA_CHUNK, GLA_CHUNK), 1))
    gng = gng_ref[...]

    for c, r0, r1 in chunks:
        for h, ks, vs in heads:
            att = lax.dot_general(qin_scr[r0:r1, ks], kin_scr[r0:r1, ks],
                                  (((1,), (1,)), ((), ())), preferred_element_type=F32)
            att_scr[h, r0:r1, :] = jnp.where(causal, att, 0.0).astype(BF16)
        yield W_SCORES

    for h, ks, vs in heads:
        for c, r0, r1 in chunks:
            ds_scr[c * h_n + h] = lax.dot_general(kst_scr[r0:r1, ks], vb_scr[r0:r1, vs],
                                                  (((0,), (0,)), ((), ())),
                                                  preferred_element_type=F32)
        yield W_INCR
    for h, ks, vs in heads:
        state = s_scr[h]
        for c, r0, r1 in chunks:
            sbf_scr[c * h_n + h] = state.astype(BF16)
            dec = jnp.transpose(jnp.broadcast_to(dec_scr[c:c + 1, ks], (dk, dk)))
            dec = jnp.concatenate([dec] * (dv // dk), axis=1)
            state = dec * state + ds_scr[c * h_n + h]
        s_scr[h] = state
        yield W_RECUR

    for c, r0, r1 in chunks:
        for h, ks, vs in heads:
            lhs = jnp.concatenate([qin_scr[r0:r1, ks], att_scr[h, r0:r1, :]], axis=1)
            rhs = jnp.concatenate([sbf_scr[c * h_n + h], vb_scr[r0:r1, vs]], axis=0)
            z_scr[r0:r1, vs] = jnp.dot(lhs, rhs, preferred_element_type=F32)
        yield W_OUT

    for c, r0, r1 in chunks:
        for h, ks, vs in heads:
            o = z_scr[r0:r1, vs]
            ms = jnp.mean(o * o, axis=-1, keepdims=True)
            on = o * lax.rsqrt(ms + EPS) * gng
            gg = z_scr[r0:r1, lay.off_gg + h * dv:lay.off_gg + (h + 1) * dv]
            og_scr[r0:r1, vs] = (on * _silu(gg)).astype(BF16)
        yield W_NORM

    for c, r0, r1 in chunks:
        for g0 in range(0, cw, LANES):
            gl = slice(g0, g0 + LANES)
            p = (z_scr[r0:r1, lay.off_cc + g0:lay.off_cc + g0 + LANES]
                 * z_scr[r0:r1, lay.off_ch + g0:lay.off_ch + g0 + LANES])
            p_scr[halo + r0:halo + r1, gl] = p
            ext = jnp.concatenate([p_scr[r0:halo + r0, gl], p], axis=0)
            yc = cw_ref[ck - 1:ck, gl] * p
            for i in range(ck - 1):
                lag = ck - 1 - i
                yc = yc + cw_ref[i:i + 1, gl] * pltpu.roll(ext, lag, 0)[halo:, :]
            yc = (z_scr[r0:r1, lay.off_cb + g0:lay.off_cb + g0 + LANES] * yc
                  * _silu(z_scr[r0:r1, lay.off_gc + g0:lay.off_gc + g0 + LANES]))
            yc_scr[r0:r1, gl] = yc.astype(BF16)
        yield W_CONV

    p_scr[0:halo, :] = p_scr[tile:tile + halo, :]

    z_scr[:, 0:d] = jnp.dot(og_scr[...], _unpack_rows(w_pa_ref[...]), preferred_element_type=F32)
    yield 0
    z_scr[:, d:2 * d] = jnp.dot(yc_scr[...], _unpack_rows(w_pb_ref[...]),
                                preferred_element_type=F32)
    yield 0
    for c, r0, r1 in chunks:
        for g0 in range(0, d, 2 * LANES):
            g1 = g0 + 2 * LANES
            merged = (jax.nn.sigmoid(z_scr[r0:r1, lay.off_mg + g0:lay.off_mg + g1])
                      * z_scr[r0:r1, g0:g1]
                      + jax.nn.sigmoid(z_scr[r0:r1, lay.off_mc + g0:lay.off_mc + g1])
                      * z_scr[r0:r1, d + g0:d + g1])
            mg_scr[r0:r1, g0:g1] = merged.astype(BF16)
        yield W_MERGE
    z_scr[:, 2 * d:3 * d] = jnp.dot(mg_scr[...], _unpack_rows(w_o_ref[...]),
                                    preferred_element_type=F32)
    yield 0

    lng = lng_ref[...]
    lnb = lnb_ref[...]
    for c, r0, r1 in chunks:
        hres = alpha * x_ref[0, row0 + r0:row0 + r1, :] + gate * z_scr[r0:r1, 2 * d:3 * d]
        mu = jnp.mean(hres, axis=-1, keepdims=True)
        dev = hres - mu
        var = jnp.mean(dev * dev, axis=-1, keepdims=True)
        y_ref[0, row0 + r0:row0 + r1, :] = dev * lax.rsqrt(var + EPS) * lng + lnb
        yield W_LN


def _emit_overlapped(mix, mix_weights, premix, premix_weights, proj, n_proj, premix_needs_proj,
                     fillers):
    head, tail = mix_weights
    order = [("mix", w) for w in head]
    done = 0
    for i, w in enumerate(tail):
        while done * len(tail) < i * len(premix_weights):
            order.append(("premix", premix_weights[done]))
            done += 1
        order.append(("mix", w))
    order += [("premix", w) for w in premix_weights[done:]]

    total = sum(w for _, w in order)
    matmul_slots = sum(1 for _, w in order if w == 0)
    done_w = done_p = done_f = slot = 0
    for source, weight in order:
        done_w += weight
        while done_p * total < done_w * n_proj or (source == "premix"
                                                    and done_p < premix_needs_proj):
            next(proj)
            done_p += 1
        if weight == 0:
            slot += 1
            while done_f * matmul_slots < slot * len(fillers):
                fillers[done_f]()
                done_f += 1
        assert next(mix if source == "mix" else premix) == weight
    while done_p < n_proj:
        next(proj)
        done_p += 1
    assert done_f == len(fillers) and next(mix, None) is None and next(premix, None) is None


def _step_row_reader(ref, rows, block, sub, steps_per_block):
    base = pl.multiple_of(block * SUBLANES, SUBLANES)

    def read(i, lanes):
        blk = ref[pl.ds(base, SUBLANES), lanes]
        val = blk[i:i + 1, :]
        for s in range(1, steps_per_block):
            val = jnp.where(sub == s, blk[s * rows + i:s * rows + i + 1, :], val)
        return val
    return read


def _project_samples(x_ref, mod_ref, w_in_refs, w_a2_ref, b_a_ref, z_scr, a_scr, qkv_scr,
                     z_hbm_ref, sem, lay):
    n, d = x_ref.shape
    shift = mod_ref[:, 0:d]
    scale = mod_ref[:, d:2 * d]
    u = (x_ref[...] * (1.0 + scale) + shift).astype(BF16)
    for piece, s0, s1, d0 in lay.groups:
        z_scr[0:n, d0:d0 + s1 - s0] = jnp.dot(u, _unpack_rows(w_in_refs[piece][:, s0:s1]),
                                              preferred_element_type=F32)
    copy = pltpu.make_async_copy(z_scr.at[pl.ds(0, n)], z_hbm_ref, sem)
    copy.start()
    a_lr = z_scr[0:n, lay.off_a:lay.off_a + lay.rank_pad].astype(BF16)
    la = jnp.dot(a_lr, _unpack_rows(w_a2_ref[...]), preferred_element_type=F32) + b_a_ref[...]
    a_scr[...] = jnp.exp(_log_sigmoid(la) * (1.0 / GLA_TAU))
    qkv_scr[...] = z_scr[0:n, 0:lay.off_gg]
    copy.wait()


def _sample_state_step(i, a_row, qkv_row, s_ref, s_out_ref, o_ref, lay):
    h_n, dk, dv = lay.h, lay.dk, lay.dv
    q_scale = dk ** -0.5
    rows = [a_row(i, slice(h * dk, (h + 1) * dk)) for h in range(h_n)]
    rows += [qkv_row(i, slice(lay.off_k + h * dk, lay.off_k + (h + 1) * dk)) for h in range(h_n)]
    rows += [qkv_row(i, slice(lay.off_q + h * dk, lay.off_q + (h + 1) * dk)) for h in range(h_n)]
    rows.append(jnp.zeros((dk - len(rows), dk), F32))
    mt = jnp.transpose(jnp.concatenate(rows, axis=0))
    for h in range(h_n):
        a_col = mt[:, h:h + 1]
        k_col = mt[:, h_n + h:h_n + h + 1]
        q_col = mt[:, 2 * h_n + h:2 * h_n + h + 1]
        v = qkv_row(i, slice(lay.off_v + h * dv, lay.off_v + (h + 1) * dv))
        s_new = a_col * s_ref[i, h] + k_col * v
        s_out_ref[i, h] = s_new
        o_ref[i:i + 1, h * dv:(h + 1) * dv] = q_scale * jnp.sum(q_col * s_new, axis=0,
                                                               keepdims=True)


def _prompt_kernel(xc_ref, xn_ref, mod_ref, xs_ref, mods_ref, sst_ref,
                   w_hd_ref, w_lr_ref, w_tl_ref, w_a2_ref, b_a_ref,
                   gng_ref, cw_ref, w_pa_ref, w_pb_ref, w_o_ref, lng_ref, lnb_ref, *rest,
                   lay, layer, tile, alpha, n_t, n_tiles, sample_rows):
    (y_ref, sg_ref, sc_ref, sso_ref, so_ref, zs_hbm_ref,
     ua_scr, ub_scr, za_scr, zb_scr, la_scr, p_scr, s_scr, og_scr, yc_scr, mg_scr,
     qin_scr, kin_scr, kst_scr, vb_scr, dec_scr, att_scr, ds_scr, sbf_scr,
     sa_scr, sqkv_scr, zs_sem) = rest[-27:]
    b_a_ref, gng_ref, lng_ref, lnb_ref = (_layer_rows(r, layer)
                                          for r in (b_a_ref, gng_ref, lng_ref, lnb_ref))
    cw_ref = _layer_taps(cw_ref, layer)
    j = pl.program_id(0)
    steps_per_seq = n_t // 2
    jj = lax.rem(j, steps_per_seq)
    b_cur = lax.div(j, steps_per_seq)
    b_next = lax.div(jnp.minimum(2 * j + 2, n_tiles - 1), n_t)
    d = lay.d
    ck = cw_ref.shape[0]
    halo = SUBLANES
    w_in_refs = (w_hd_ref, w_lr_ref, w_tl_ref)

    @pl.when(jj == 0)
    def _():
        s_scr[...] = jnp.zeros_like(s_scr)
        p_scr[0:halo, :] = jnp.zeros((halo, lay.cw), F32)

    def project(x_rows, brow, u_scr, z_dst):
        return _project_stages(x_rows, mod_ref, brow, w_in_refs, u_scr, z_dst, lay)

    operands = (qin_scr, kin_scr, kst_scr, vb_scr, dec_scr)

    def premix(z_scr):
        return _premix_stages(z_scr, w_a2_ref, b_a_ref, la_scr, *operands, lay, tile)

    def mix(z_scr, row0):
        return _mix_stages(z_scr, xc_ref, y_ref, row0, gate, gng_ref, cw_ref, w_pa_ref, w_pb_ref,
                           w_o_ref, lng_ref, lnb_ref, p_scr, s_scr, og_scr, yc_scr, mg_scr,
                           *operands, att_scr, ds_scr, sbf_scr, lay, tile, alpha)

    @pl.when(j == 0)
    def _():
        _project_samples(xs_ref, mods_ref, w_in_refs, w_a2_ref, b_a_ref, za_scr, sa_scr,
                         sqkv_scr, zs_hbm_ref, zs_sem, lay)
        for _ in project(lambda: xc_ref[0, 0:tile, :], b_cur, ua_scr, za_scr):
            pass
        for _ in premix(za_scr):
            pass

    gate = mod_ref[pl.ds(b_cur, 1), 2 * d:3 * d]
    mix_weights = _mix_weights(lay, tile)
    premix_weights = _premix_weights(lay, tile)
    plan = _project_plan(lay)
    n_proj = 1 + len(plan)
    premix_needs_proj = 1 + sum(1 for _, _, _, d0 in plan if d0 < lay.off_cb)

    steps_per_block = SUBLANES // sample_rows
    sub = lax.rem(j, steps_per_block)
    block = lax.div(j, steps_per_block)
    a_row = _step_row_reader(sa_scr, sample_rows, block, sub, steps_per_block)
    qkv_row = _step_row_reader(sqkv_scr, sample_rows, block, sub, steps_per_block)
    fillers = [functools.partial(_sample_state_step, i, a_row, qkv_row, sst_ref,
                                 sso_ref, so_ref, lay) for i in range(sample_rows)]
    half = sample_rows // 2

    _emit_overlapped(mix(za_scr, 0), mix_weights, premix(zb_scr), premix_weights,
                     project(lambda: xc_ref[0, tile:2 * tile, :], b_cur, ub_scr, zb_scr), n_proj,
                     premix_needs_proj, fillers[:half])
    _emit_overlapped(mix(zb_scr, tile), mix_weights, premix(za_scr), premix_weights,
                     project(lambda: xn_ref[0], b_next, ua_scr, za_scr), n_proj,
                     premix_needs_proj, fillers[half:])

    @pl.when(jj == steps_per_seq - 1)
    def _():
        sg_ref[0] = s_scr[...]
        sc_ref[0] = p_scr[halo - (ck - 1):halo, :]


def _prompt_vmem_bytes(lay, tile, sample_rows, n_s):
    weights = 2 * (lay.d * lay.n_pad + lay.rank_pad * lay.qk + lay.vt * lay.d
                   + lay.cw * lay.d + lay.d * lay.d)
    state = lay.h * lay.dk * lay.dv * 4
    io = (2 * (2 * tile + tile + 2 * tile) * lay.d * 4 + 2 * state
          + 2 * 2 * sample_rows * state + n_s * (3 * lay.d + lay.qk + lay.off_gg) * 4)
    chunk_heads = (tile // GLA_CHUNK) * lay.h
    scratch = (2 * tile * lay.d * 2 + 2 * tile * lay.n_pad * 4 + tile * lay.qk * 4
               + (tile + SUBLANES) * lay.cw * 4 + lay.h * lay.dk * lay.dv * 4
               + tile * lay.vt * 2 + tile * lay.cw * 2 + tile * lay.d * 2
               + 3 * tile * lay.qk * 2 + tile * lay.vt * 2 + SUBLANES * lay.qk * 4
               + lay.h * tile * LANES * 2 + chunk_heads * lay.dk * lay.dv * 6)
    temporaries = tile * lay.n_pad * 2
    return min(weights + io + scratch + temporaries, VMEM_PHYSICAL_BYTES * 15 // 16)


def _prompt_layer(x, mod_p, x_sample, mod_s, sample_state, params, prev_states, layer, lay,
                  alpha):
    (w_hd, w_lr, w_tl, w_a2, b_a, gng, conv_w, w_pa, w_pb, w_o, ln_g, ln_b) = params
    bsz, seq, d = x.shape
    depth = w_hd.shape[0]
    tile = PROMPT_TILE
    assert seq % (2 * tile) == 0 and tile % GLA_CHUNK == 0
    ck = conv_w.shape[0]
    n_t = seq // tile
    n_tiles = bsz * n_t
    n_steps = n_tiles // 2
    steps_per_seq = n_t // 2
    n_chunk_heads = (tile // GLA_CHUNK) * lay.h
    assert tile // GLA_CHUNK <= SUBLANES
    n_s = sample_state.shape[1]
    sample_rows = n_s // n_steps
    assert sample_rows * n_steps == n_s and sample_rows % 2 == 0 and SUBLANES % sample_rows == 0
    steps_per_block = SUBLANES // sample_rows

    def weight(arr):
        return _layer_spec(arr.shape[1:], layer, True)

    def next_tile(j):
        g = jnp.minimum(2 * j + 2, n_tiles - 1)
        return (g // n_t, g % n_t, 0)

    sample_state_spec = pl.BlockSpec((None, sample_rows, lay.h, lay.dk, lay.dv),
                                     lambda j: (layer, j, 0, 0, 0))
    in_specs = [
        pl.BlockSpec((1, 2 * tile, d), lambda j: (j // steps_per_seq, j % steps_per_seq, 0)),
        pl.BlockSpec((1, tile, d), next_tile),
        _layer_spec(mod_p.shape[1:], layer),
        pl.BlockSpec((n_s, d), lambda j: (0, 0), pipeline_mode=pl.Buffered(1)),
        pl.BlockSpec((None, n_s, 2 * d), lambda j: (layer, 0, 0), pipeline_mode=pl.Buffered(1)),
        sample_state_spec,
        weight(w_hd), weight(w_lr), weight(w_tl), weight(w_a2), _whole_spec(b_a),
        _whole_spec(gng), _whole_spec(conv_w), weight(w_pa), weight(w_pb), weight(w_o),
        _whole_spec(ln_g), _whole_spec(ln_b),
    ]
    args = [x, x, mod_p, x_sample, mod_s, sample_state, *params]
    aliases = {}
    if prev_states is not None:
        for i, prev in enumerate(prev_states):
            in_specs.append(pl.BlockSpec(memory_space=pl.ANY))
            aliases[len(args)] = 1 + i
            args.append(prev)

    kern = functools.partial(_prompt_kernel, lay=lay, layer=layer, tile=tile, alpha=alpha,
                             n_t=n_t, n_tiles=n_tiles, sample_rows=sample_rows)
    return pl.pallas_call(
        kern,
        grid=(n_steps,),
        in_specs=in_specs,
        out_specs=[
            pl.BlockSpec((1, 2 * tile, d), lambda j: (j // steps_per_seq, j % steps_per_seq, 0)),
            pl.BlockSpec((None, 1, lay.h, lay.dk, lay.dv),
                         lambda j: (layer, j // steps_per_seq, 0, 0, 0)),
            pl.BlockSpec((None, 1, ck - 1, lay.cw), lambda j: (layer, j // steps_per_seq, 0, 0)),
            sample_state_spec,
            pl.BlockSpec((None, sample_rows, lay.vt), lambda j: (j, 0, 0)),
            pl.BlockSpec(memory_space=pl.ANY),
        ],
        out_shape=[
            jax.ShapeDtypeStruct((bsz, seq, d), F32),
            jax.ShapeDtypeStruct((depth, bsz, lay.h, lay.dk, lay.dv), F32),
            jax.ShapeDtypeStruct((depth, bsz, ck - 1, lay.cw), F32),
            jax.ShapeDtypeStruct(sample_state.shape, F32),
            jax.ShapeDtypeStruct((n_steps, sample_rows, lay.vt), F32),
            jax.ShapeDtypeStruct((n_s, lay.n_pad), F32),
        ],
        scratch_shapes=[
            pltpu.VMEM((tile, d), BF16),
            pltpu.VMEM((tile, d), BF16),
            pltpu.VMEM((tile, lay.n_pad), F32),
            pltpu.VMEM((tile, lay.n_pad), F32),
            pltpu.VMEM((tile, lay.qk), F32),
            pltpu.VMEM((tile + SUBLANES, lay.cw), F32),
            pltpu.VMEM((lay.h, lay.dk, lay.dv), F32),
            pltpu.VMEM((tile, lay.vt), BF16),
            pltpu.VMEM((tile, lay.cw), BF16),
            pltpu.VMEM((tile, d), BF16),
            pltpu.VMEM((tile, lay.qk), BF16),
            pltpu.VMEM((tile, lay.qk), BF16),
            pltpu.VMEM((tile, lay.qk), BF16),
            pltpu.VMEM((tile, lay.vt), BF16),
            pltpu.VMEM((SUBLANES, lay.qk), F32),
            pltpu.VMEM((lay.h, tile, GLA_CHUNK), BF16),
            pltpu.VMEM((n_chunk_heads, lay.dk, lay.dv), F32),
            pltpu.VMEM((n_chunk_heads, lay.dk, lay.dv), BF16),
            pltpu.VMEM((n_s, lay.qk), F32),
            pltpu.VMEM((n_s, lay.off_gg), F32),
            pltpu.SemaphoreType.DMA(()),
        ],
        input_output_aliases=aliases,
        compiler_params=pltpu.CompilerParams(
            dimension_semantics=("arbitrary",),
            vmem_limit_bytes=_prompt_vmem_bytes(lay, tile, sample_rows, n_s)),
        name="prompt_layer",
    )(*args)


def _sample_out_kernel(o_ref, z_ref, x_ref, mod_ref, buf_ref, gng_ref, cw_ref,
                       w_pa_ref, w_pb_ref, w_o_ref, lng_ref, lnb_ref, *rest, lay, layer, alpha):
    y_ref, buf_out_ref, og_scr = rest[-3:]
    d, h_n, dv, cw = lay.d, lay.h, lay.dv, lay.cw
    gng_ref, lng_ref, lnb_ref = (_layer_rows(r, layer) for r in (gng_ref, lng_ref, lnb_ref))
    cw_ref = _layer_taps(cw_ref, layer)
    ck = cw_ref.shape[0]
    gate = mod_ref[:, 2 * d:3 * d]
    gng = gng_ref[...]
    for h in range(h_n):
        o = o_ref[:, h * dv:(h + 1) * dv]
        ms = jnp.mean(o * o, axis=-1, keepdims=True)
        on = o * lax.rsqrt(ms + EPS) * gng
        gg = z_ref[:, lay.off_gg + h * dv:lay.off_gg + (h + 1) * dv]
        og_scr[:, h * dv:(h + 1) * dv] = (on * _silu(gg)).astype(BF16)
    pa = jnp.dot(og_scr[...], _unpack_rows(w_pa_ref[...]), preferred_element_type=F32)

    p = z_ref[:, lay.off_cc:lay.off_cc + cw] * z_ref[:, lay.off_ch:lay.off_ch + cw]
    yc = cw_ref[ck - 1:ck, :] * p
    for i in range(ck - 1):
        yc = yc + cw_ref[i:i + 1, :] * buf_ref[:, i * cw:(i + 1) * cw]
    yc = z_ref[:, lay.off_cb:lay.off_cb + cw] * yc * _silu(z_ref[:, lay.off_gc:lay.off_gc + cw])
    pb = jnp.dot(yc.astype(BF16), _unpack_rows(w_pb_ref[...]), preferred_element_type=F32)
    for i in range(ck - 2):
        buf_out_ref[:, i * cw:(i + 1) * cw] = buf_ref[:, (i + 1) * cw:(i + 2) * cw]
    buf_out_ref[:, (ck - 2) * cw:(ck - 1) * cw] = p

    merged = (jax.nn.sigmoid(z_ref[:, lay.off_mg:lay.off_mg + d]) * pa
              + jax.nn.sigmoid(z_ref[:, lay.off_mc:lay.off_mc + d]) * pb)
    out = jnp.dot(merged.astype(BF16), _unpack_rows(w_o_ref[...]),
                  preferred_element_type=F32)
    hres = alpha * x_ref[...] + gate * out
    mu = jnp.mean(hres, axis=-1, keepdims=True)
    dev = hres - mu
    var = jnp.mean(dev * dev, axis=-1, keepdims=True)
    y_ref[...] = dev * lax.rsqrt(var + EPS) * lng_ref[...] + lnb_ref[...]


def _sample_out(o, z, x, mod_s, conv_state, prev_conv_out, gng, conv_w, w_pa, w_pb, w_o,
                ln_g, ln_b, layer, lay, alpha):
    n, d = x.shape
    kern = functools.partial(_sample_out_kernel, lay=lay, layer=layer, alpha=alpha)
    conv_spec = _layer_spec(conv_state.shape[1:], layer)
    in_specs = [_whole_spec(o), _whole_spec(z), _whole_spec(x),
                _layer_spec(mod_s.shape[1:], layer), conv_spec,
                _whole_spec(gng), _whole_spec(conv_w), _layer_spec(w_pa.shape[1:], layer),
                _layer_spec(w_pb.shape[1:], layer), _layer_spec(w_o.shape[1:], layer),
                _whole_spec(ln_g), _whole_spec(ln_b)]
    args = [o, z, x, mod_s, conv_state, gng, conv_w, w_pa, w_pb, w_o, ln_g, ln_b]
    aliases = {}
    if prev_conv_out is not None:
        in_specs.append(pl.BlockSpec(memory_space=pl.ANY))
        aliases = {len(args): 1}
        args.append(prev_conv_out)
    return pl.pallas_call(
        kern,
        grid=(1,),
        in_specs=in_specs,
        out_specs=[pl.BlockSpec((n, d), lambda i: (0, 0)), conv_spec],
        out_shape=[jax.ShapeDtypeStruct((n, d), F32),
                   jax.ShapeDtypeStruct(conv_state.shape, F32)],
        scratch_shapes=[pltpu.VMEM((n, lay.vt), BF16)],
        input_output_aliases=aliases,
        compiler_params=pltpu.CompilerParams(dimension_semantics=("arbitrary",),
                                             vmem_limit_bytes=STREAMING_VMEM_BYTES),
        name="sample_out",
    )(*args)


def kernel(x_prompt, x_sample, c_prompt, c_sample, state_gla, state_conv, w_ada, b_ada, w_in,
           w_a2, b_a, gla_norm_g, conv_w, w_pa, w_pb, w_o, ln_g, ln_b):
    depth, n_s, heads, dk, dv = state_gla.shape
    d = x_prompt.shape[2]
    rank = w_a2.shape[1]
    ck, cw = conv_w.shape[1:]
    lay = _Layout(d, heads, dk, dv, rank, cw)
    alpha = (2 * depth) ** 0.25

    mod_p, mod_s = _ada_mod(c_prompt, c_sample, w_ada, b_ada)

    w_hd, w_lr, w_tl = _split_w_in(w_in, lay)
    w_pa_b, w_pb_b, w_o_b, w_a2_p = _cast_weights((w_pa, w_pb, w_o), w_a2, lay.rank_pad)
    taps = jnp.swapaxes(conv_w, 0, 1)
    params = (w_hd, w_lr, w_tl, w_a2_p, b_a, gla_norm_g, taps, w_pa_b, w_pb_b, w_o_b, ln_g, ln_b)

    hp, hs = x_prompt, x_sample.reshape(n_s, d)
    states = None
    conv_in = state_conv.reshape(depth, n_s, (ck - 1) * cw)
    conv_s = None
    for l in range(depth):
        hp, *states, o, z = _prompt_layer(hp, mod_p, hs, mod_s, state_gla, params, states, l, lay,
                                          alpha)
        hs, conv_s = _sample_out(o.reshape(n_s, lay.vt), z, hs, mod_s, conv_in, conv_s,
                                 gla_norm_g, taps, w_pa_b, w_pb_b, w_o_b, ln_g, ln_b, l, lay,
                                 alpha)
    return (hp, hs.reshape(x_sample.shape), states[0], states[1], states[2],
            conv_s.reshape(state_conv.shape))
```

```python
import functools

import jax
import jax.numpy as jnp
from jax import lax
from jax.experimental import pallas as pl
from jax.experimental.pallas import tpu as pltpu

F32 = jnp.float32
BF16 = jnp.bfloat16

GLA_TAU = 16.0
GLA_CHUNK = 64
EPS = 1e-5

LANES = 128
SUBLANES = 8
VMEM_PHYSICAL_BYTES = 64 * 1024 * 1024
MIB = 1024 * 1024
STREAMING_VMEM_BYTES = VMEM_PHYSICAL_BYTES * 7 // 8

PROMPT_TILE = 256
PROJ_COLS = 256
PREP_STEPS = 2
ADA_STEPS = 2
CAST_STEPS = 4


LOG2_E = 1.4426950408889634


def _log_sigmoid(x):
    return jnp.minimum(x, 0.0) - jnp.log(1.0 + jnp.exp(-jnp.abs(x)))


def _cumsum_rows(x):
    row = lax.broadcasted_iota(jnp.int32, (SUBLANES, x.shape[1]), 0)
    groups = []
    carry = None
    for g in range(x.shape[0] // SUBLANES):
        t = x[g * SUBLANES:(g + 1) * SUBLANES, :]
        s = 1
        while s < SUBLANES:
            t = t + jnp.where(row >= s, pltpu.roll(t, s, 0), 0.0)
            s *= 2
        if carry is not None:
            t = t + carry
        carry = jnp.broadcast_to(t[SUBLANES - 1:SUBLANES, :], t.shape)
        groups.append(t)
    return jnp.concatenate(groups, axis=0)


def _silu(x):
    return x * jax.nn.sigmoid(x)


PACKED = jnp.uint32
PACK = 2


def _pack_rows(w_bf16):
    return pltpu.bitcast(w_bf16, PACKED)


def _unpack_rows(w_packed):
    return pltpu.bitcast(w_packed, BF16)


class _Layout:
    def __init__(self, d_model, heads, dk, dv, rank, conv_w):
        self.d = d_model
        self.h = heads
        self.dk = dk
        self.dv = dv
        self.rank = rank
        self.cw = conv_w
        self.qk = heads * dk
        self.vt = heads * dv
        self.rank_pad = -(-rank // LANES) * LANES
        self.off_q = 0
        self.off_k = self.off_q + self.qk
        self.off_v = self.off_k + self.qk
        self.off_gg = self.off_v + self.vt
        self.off_a = self.off_gg + self.vt
        self.off_cb = self.off_a + self.rank_pad
        self.off_cc = self.off_cb + conv_w
        self.off_ch = self.off_cc + conv_w
        self.off_gc = self.off_ch + conv_w
        self.off_mg = self.off_gc + conv_w
        self.off_mc = self.off_mg + d_model
        self.n_pad = self.off_mc + d_model
        self.n_head = self.off_a
        self.n_tail = self.n_pad - self.off_cb
        self.groups = (
            (0, self.off_q, self.off_v, self.off_q),
            (0, self.off_v, self.off_gg, self.off_v),
            (0, self.off_gg, self.off_a, self.off_gg),
            (1, 0, self.rank_pad, self.off_a),
            (2, 0, 2 * conv_w, self.off_cb),
            (2, 2 * conv_w, 4 * conv_w, self.off_ch),
            (2, 4 * conv_w, 4 * conv_w + d_model, self.off_mg),
            (2, 4 * conv_w + d_model, self.n_tail, self.off_mc))


def _layer_spec(shape, layer, single_buffer=False):
    zeros = (0,) * len(shape)
    kwargs = {"pipeline_mode": pl.Buffered(1)} if single_buffer else {}
    return pl.BlockSpec((None,) + tuple(shape), lambda *_: (layer,) + zeros, **kwargs)


def _whole_spec(arr):
    return pl.BlockSpec(arr.shape, lambda *_: (0,) * arr.ndim)


def _layer_rows(ref, layer):
    return ref.at[pl.ds(layer, 1)]


def _layer_taps(ref, layer):
    return ref.at[:, layer]


def _transpose_cast_kernel(w_ref, o_ref, *, valid):
    wt = jnp.transpose(w_ref[0])
    if valid < wt.shape[1]:
        lane = lax.broadcasted_iota(jnp.int32, wt.shape, 1)
        wt = jnp.where(lane < valid, wt, 0.0)
    o_ref[...] = _pack_rows(wt.astype(BF16))


def _transpose_cast(w_t, col0, n_valid, n_out, cols):
    depth, n_in, d = w_t.shape
    assert n_out % cols == 0 and col0 % SUBLANES == 0 and col0 + n_out <= n_in
    kern = functools.partial(_transpose_cast_kernel, valid=min(n_valid, cols))
    return pl.pallas_call(
        kern,
        grid=(depth, n_out // cols),
        in_specs=[pl.BlockSpec((pl.Element(1), pl.Element(cols), pl.Element(d)),
                               lambda l, j: (l, pl.multiple_of(col0 + j * cols, SUBLANES), 0))],
        out_specs=pl.BlockSpec((None, d // PACK, cols), lambda l, j: (l, 0, j)),
        out_shape=jax.ShapeDtypeStruct((depth, d // PACK, n_out), PACKED),
        compiler_params=pltpu.CompilerParams(
            dimension_semantics=("arbitrary", "arbitrary"),
            vmem_limit_bytes=STREAMING_VMEM_BYTES),
        name="transpose_cast",
    )(w_t)


def _split_w_in(w_in, lay):
    n_in = w_in.shape[2]
    a_end = lay.off_a + lay.rank
    assert n_in == a_end + lay.n_tail and lay.rank <= lay.rank_pad == LANES
    w_t = jnp.swapaxes(w_in, 1, 2)
    head = _transpose_cast(w_t, 0, lay.n_head, lay.n_head, lay.n_head // PREP_STEPS)
    low = _transpose_cast(w_t, lay.off_a, lay.rank, lay.rank_pad, lay.rank_pad)
    tail = _transpose_cast(w_t, a_end, lay.n_tail, lay.n_tail, lay.n_tail // PREP_STEPS)
    return head, low, tail


def _cast_kernel(*refs, rank_pad):
    n = len(refs) // 2
    w_a2_ref, o_a2_ref = refs[n - 1], refs[-1]
    for w_ref, o_ref in zip(refs[:n - 1], refs[n:-1]):
        o_ref[...] = _pack_rows(w_ref[...].astype(BF16))
    w_a2 = w_a2_ref[...]
    pad = jnp.zeros((rank_pad - w_a2.shape[0], w_a2.shape[1]), F32)
    o_a2_ref[...] = _pack_rows(jnp.concatenate([w_a2, pad], axis=0).astype(BF16))


def _cast_weights(weights, w_a2, rank_pad):
    depth = w_a2.shape[0]
    for w in weights:
        assert w.shape[1] % (CAST_STEPS * PACK * SUBLANES) == 0

    def rows_spec(rows, cols):
        return pl.BlockSpec((None, rows, cols), lambda l, r: (l, r, 0))

    def layer_spec(rows, cols):
        return pl.BlockSpec((None, rows, cols), lambda l, r: (l, 0, 0))

    kern = functools.partial(_cast_kernel, rank_pad=rank_pad)
    return pl.pallas_call(
        kern,
        grid=(depth, CAST_STEPS),
        in_specs=[rows_spec(w.shape[1] // CAST_STEPS, w.shape[2]) for w in weights]
        + [layer_spec(*w_a2.shape[1:])],
        out_specs=[rows_spec(w.shape[1] // (CAST_STEPS * PACK), w.shape[2]) for w in weights]
        + [layer_spec(rank_pad // PACK, w_a2.shape[2])],
        out_shape=[jax.ShapeDtypeStruct((depth, w.shape[1] // PACK, w.shape[2]), PACKED)
                   for w in weights]
        + [jax.ShapeDtypeStruct((depth, rank_pad // PACK, w_a2.shape[2]), PACKED)],
        compiler_params=pltpu.CompilerParams(
            dimension_semantics=("arbitrary", "arbitrary"),
            vmem_limit_bytes=STREAMING_VMEM_BYTES),
        name="cast_weights",
    )(*weights, w_a2)


def _ada_kernel(cp_ref, cs_ref, w_ref, b_ref, *rest):
    op_ref, os_ref = rest[-2:]
    w = w_ref[...].astype(BF16)
    b = b_ref[...]
    op_ref[...] = jnp.dot(cp_ref[...].astype(BF16), w, preferred_element_type=F32) + b
    os_ref[...] = jnp.dot(cs_ref[...].astype(BF16), w, preferred_element_type=F32) + b


def _ada_mod(c_prompt, c_sample, w_ada, b_ada, layer, after=None):
    depth, d, n3 = w_ada.shape
    bp, ns = c_prompt.shape[0], c_sample.shape[0]
    tn = n3 // ADA_STEPS
    in_specs = [
        pl.BlockSpec((bp, d), lambda j: (0, 0)),
        pl.BlockSpec((ns, d), lambda j: (0, 0)),
        pl.BlockSpec((None, d, tn), lambda j: (layer, 0, j)),
        pl.BlockSpec((None, 1, tn), lambda j: (layer, 0, j)),
    ]
    args = [c_prompt, c_sample, w_ada, b_ada.reshape(depth, 1, n3)]
    if after is not None:
        in_specs.append(pl.BlockSpec(memory_space=pl.ANY))
        args.append(after)
    return pl.pallas_call(
        _ada_kernel,
        grid=(n3 // tn,),
        in_specs=in_specs,
        out_specs=[pl.BlockSpec((None, bp, tn), lambda j: (0, 0, j)),
                   pl.BlockSpec((None, ns, tn), lambda j: (0, 0, j))],
        out_shape=[jax.ShapeDtypeStruct((1, bp, n3), F32),
                   jax.ShapeDtypeStruct((1, ns, n3), F32)],
        compiler_params=pltpu.CompilerParams(
            dimension_semantics=("arbitrary",),
            vmem_limit_bytes=STREAMING_VMEM_BYTES),
        name="ada_mod",
    )(*args)


def _project_plan(lay):
    plan = []
    for piece, s0, s1, d0 in lay.groups:
        for c0 in range(s0, s1, PROJ_COLS):
            plan.append((piece, c0, min(c0 + PROJ_COLS, s1), d0 + c0 - s0))
    return plan


def _project_stages(x_rows, mod_ref, brow, w_in_refs, u_scr, z_dst, lay):
    d = lay.d
    shift = mod_ref[pl.ds(brow, 1), 0:d]
    scale = mod_ref[pl.ds(brow, 1), d:2 * d]
    u_scr[...] = (x_rows() * (1.0 + scale) + shift).astype(BF16)
    yield
    for piece, c0, c1, d0 in _project_plan(lay):
        z_dst[:, d0:d0 + c1 - c0] = jnp.dot(u_scr[...],
                                            _unpack_rows(w_in_refs[piece][:, c0:c1]),
                                            preferred_element_type=F32)
        yield


def _tile_chunks(tile):
    return [(c, c * GLA_CHUNK, (c + 1) * GLA_CHUNK) for c in range(tile // GLA_CHUNK)]


def _tile_heads(lay):
    return [(h, slice(h * lay.dk, (h + 1) * lay.dk), slice(h * lay.dv, (h + 1) * lay.dv))
            for h in range(lay.h)]


W_DECAY, W_FACTORS, W_SCORES, W_INCR, W_RECUR, W_OUT, W_NORM, W_CONV, W_MERGE, W_LN = (
    4, 4, 1, 1, 5, 1, 11, 10, 10, 10)


def _premix_weights(lay, tile):
    return [W_DECAY] + [W_FACTORS] * (len(_tile_chunks(tile)) * lay.h)


def _premix_stages(z_scr, w_a2_ref, b_a_ref, la_scr, qin_scr, kin_scr, kst_scr, vb_scr, dec_scr,
                   lay, tile):
    dk, dv = lay.dk, lay.dv
    q_scale = dk ** -0.5

    a_lr = z_scr[:, lay.off_a:lay.off_a + lay.rank_pad].astype(BF16)
    la_scr[...] = (jnp.dot(a_lr, _unpack_rows(w_a2_ref[...]), preferred_element_type=F32)
                   + b_a_ref[...])
    yield W_DECAY

    for c, r0, r1 in _tile_chunks(tile):
        for h, ks, vs in _tile_heads(lay):
            log_a = _log_sigmoid(la_scr[r0:r1, ks]) * (1.0 / GLA_TAU)
            b2 = _cumsum_rows(log_a) * LOG2_E
            bl2 = b2[GLA_CHUNK - 1:GLA_CHUNK, :]
            q_in = z_scr[r0:r1, lay.off_q + h * dk:lay.off_q + (h + 1) * dk] * q_scale
            k = z_scr[r0:r1, lay.off_k + h * dk:lay.off_k + (h + 1) * dk]
            qin_scr[r0:r1, ks] = (q_in * jnp.exp2(b2)).astype(BF16)
            kin_scr[r0:r1, ks] = (k * jnp.exp2(-b2)).astype(BF16)
            kst_scr[r0:r1, ks] = (k * jnp.exp2(bl2 - b2)).astype(BF16)
            dec_scr[c:c + 1, ks] = jnp.exp2(bl2)
            vb_scr[r0:r1, vs] = z_scr[r0:r1, lay.off_v + h * dv:lay.off_v + (h + 1) * dv] \
                .astype(BF16)
            yield W_FACTORS


def _mix_weights(lay, tile):
    chunks = len(_tile_chunks(tile))
    head = ([W_SCORES] * chunks + [W_INCR] * lay.h + [W_RECUR] * lay.h + [W_OUT] * chunks
            + [W_NORM] * chunks + [W_CONV] * chunks)
    tail = [0, 0] + [W_MERGE] * chunks + [0] + [W_LN] * chunks
    return head, tail


def _mix_stages(z_scr, x_ref, y_ref, row0, gate, gng_ref, cw_ref,
                w_pa_ref, w_pb_ref, w_o_ref, lng_ref, lnb_ref,
                p_scr, s_scr, og_scr, yc_scr, mg_scr,
                qin_scr, kin_scr, kst_scr, vb_scr, dec_scr, att_scr, ds_scr, sbf_scr,
                lay, tile, alpha):
    d, h_n, dk, dv, cw = lay.d, lay.h, lay.dk, lay.dv, lay.cw
    ck = cw_ref.shape[0]
    halo = SUBLANES
    chunks = _tile_chunks(tile)
    heads = _tile_heads(lay)
    causal = (lax.broadcasted_iota(jnp.int32, (GLA_CHUNK, GLA_CHUNK), 0)
              >= lax.broadcasted_iota(jnp.int32, (GLA_CHUNK, GLA_CHUNK), 1))
    gng = gng_ref[...]

    for c, r0, r1 in chunks:
        for h, ks, vs in heads:
            att = lax.dot_general(qin_scr[r0:r1, ks], kin_scr[r0:r1, ks],
                                  (((1,), (1,)), ((), ())), preferred_element_type=F32)
            att_scr[h, r0:r1, :] = jnp.where(causal, att, 0.0).astype(BF16)
        yield W_SCORES

    for h, ks, vs in heads:
        for c, r0, r1 in chunks:
            ds_scr[c * h_n + h] = lax.dot_general(kst_scr[r0:r1, ks], vb_scr[r0:r1, vs],
                                                  (((0,), (0,)), ((), ())),
                                                  preferred_element_type=F32)
        yield W_INCR
    for h, ks, vs in heads:
        state = s_scr[h]
        for c, r0, r1 in chunks:
            sbf_scr[c * h_n + h] = state.astype(BF16)
            dec = jnp.transpose(jnp.broadcast_to(dec_scr[c:c + 1, ks], (dk, dk)))
            dec = jnp.concatenate([dec] * (dv // dk), axis=1)
            state = dec * state + ds_scr[c * h_n + h]
        s_scr[h] = state
        yield W_RECUR

    for c, r0, r1 in chunks:
        for h, ks, vs in heads:
            lhs = jnp.concatenate([qin_scr[r0:r1, ks], att_scr[h, r0:r1, :]], axis=1)
            rhs = jnp.concatenate([sbf_scr[c * h_n + h], vb_scr[r0:r1, vs]], axis=0)
            z_scr[r0:r1, vs] = jnp.dot(lhs, rhs, preferred_element_type=F32)
        yield W_OUT

    for c, r0, r1 in chunks:
        for h, ks, vs in heads:
            o = z_scr[r0:r1, vs]
            ms = jnp.mean(o * o, axis=-1, keepdims=True)
            on = o * lax.rsqrt(ms + EPS) * gng
            gg = z_scr[r0:r1, lay.off_gg + h * dv:lay.off_gg + (h + 1) * dv]
            og_scr[r0:r1, vs] = (on * _silu(gg)).astype(BF16)
        yield W_NORM

    for c, r0, r1 in chunks:
        for g0 in range(0, cw, LANES):
            gl = slice(g0, g0 + LANES)
            p = (z_scr[r0:r1, lay.off_cc + g0:lay.off_cc + g0 + LANES]
                 * z_scr[r0:r1, lay.off_ch + g0:lay.off_ch + g0 + LANES])
            p_scr[halo + r0:halo + r1, gl] = p
            ext = jnp.concatenate([p_scr[r0:halo + r0, gl], p], axis=0)
            yc = cw_ref[ck - 1:ck, gl] * p
            for i in range(ck - 1):
                lag = ck - 1 - i
                yc = yc + cw_ref[i:i + 1, gl] * pltpu.roll(ext, lag, 0)[halo:, :]
            yc = (z_scr[r0:r1, lay.off_cb + g0:lay.off_cb + g0 + LANES] * yc
                  * _silu(z_scr[r0:r1, lay.off_gc + g0:lay.off_gc + g0 + LANES]))
            yc_scr[r0:r1, gl] = yc.astype(BF16)
        yield W_CONV

    p_scr[0:halo, :] = p_scr[tile:tile + halo, :]

    z_scr[:, 0:d] = jnp.dot(og_scr[...], _unpack_rows(w_pa_ref[...]), preferred_element_type=F32)
    yield 0
    z_scr[:, d:2 * d] = jnp.dot(yc_scr[...], _unpack_rows(w_pb_ref[...]),
                                preferred_element_type=F32)
    yield 0
    for c, r0, r1 in chunks:
        for g0 in range(0, d, 2 * LANES):
            g1 = g0 + 2 * LANES
            merged = (jax.nn.sigmoid(z_scr[r0:r1, lay.off_mg + g0:lay.off_mg + g1])
                      * z_scr[r0:r1, g0:g1]
                      + jax.nn.sigmoid(z_scr[r0:r1, lay.off_mc + g0:lay.off_mc + g1])
                      * z_scr[r0:r1, d + g0:d + g1])
            mg_scr[r0:r1, g0:g1] = merged.astype(BF16)
        yield W_MERGE
    z_scr[:, 2 * d:3 * d] = jnp.dot(mg_scr[...], _unpack_rows(w_o_ref[...]),
                                    preferred_element_type=F32)
    yield 0

    lng = lng_ref[...]
    lnb = lnb_ref[...]
    for c, r0, r1 in chunks:
        hres = alpha * x_ref[0, row0 + r0:row0 + r1, :] + gate * z_scr[r0:r1, 2 * d:3 * d]
        mu = jnp.mean(hres, axis=-1, keepdims=True)
        dev = hres - mu
        var = jnp.mean(dev * dev, axis=-1, keepdims=True)
        y_ref[0, row0 + r0:row0 + r1, :] = dev * lax.rsqrt(var + EPS) * lng + lnb
        yield W_LN


def _emit_overlapped(mix, mix_weights, premix, premix_weights, proj, n_proj, premix_needs_proj,
                     fillers):
    head, tail = mix_weights
    order = [("mix", w) for w in head]
    done = 0
    for i, w in enumerate(tail):
        while done * len(tail) < i * len(premix_weights):
            order.append(("premix", premix_weights[done]))
            done += 1
        order.append(("mix", w))
    order += [("premix", w) for w in premix_weights[done:]]

    total = sum(w for _, w in order)
    matmul_slots = sum(1 for _, w in order if w == 0)
    done_w = done_p = done_f = slot = 0
    for source, weight in order:
        done_w += weight
        while done_p * total < done_w * n_proj or (source == "premix"
                                                    and done_p < premix_needs_proj):
            next(proj)
            done_p += 1
        if weight == 0:
            slot += 1
            while done_f * matmul_slots < slot * len(fillers):
                fillers[done_f]()
                done_f += 1
        assert next(mix if source == "mix" else premix) == weight
    while done_p < n_proj:
        next(proj)
        done_p += 1
    assert done_f == len(fillers) and next(mix, None) is None and next(premix, None) is None


def _step_row_reader(ref, rows, block, sub, steps_per_block):
    base = pl.multiple_of(block * SUBLANES, SUBLANES)

    def read(i, lanes):
        blk = ref[pl.ds(base, SUBLANES), lanes]
        val = blk[i:i + 1, :]
        for s in range(1, steps_per_block):
            val = jnp.where(sub == s, blk[s * rows + i:s * rows + i + 1, :], val)
        return val
    return read


def _project_samples(x_ref, mod_ref, w_in_refs, w_a2_ref, b_a_ref, z_scr, a_scr, qkv_scr,
                     z_hbm_ref, sem, lay):
    n, d = x_ref.shape
    shift = mod_ref[:, 0:d]
    scale = mod_ref[:, d:2 * d]
    u = (x_ref[...] * (1.0 + scale) + shift).astype(BF16)
    for piece, s0, s1, d0 in lay.groups:
        z_scr[0:n, d0:d0 + s1 - s0] = jnp.dot(u, _unpack_rows(w_in_refs[piece][:, s0:s1]),
                                              preferred_element_type=F32)
    copy = pltpu.make_async_copy(z_scr.at[pl.ds(0, n)], z_hbm_ref, sem)
    copy.start()
    a_lr = z_scr[0:n, lay.off_a:lay.off_a + lay.rank_pad].astype(BF16)
    la = jnp.dot(a_lr, _unpack_rows(w_a2_ref[...]), preferred_element_type=F32) + b_a_ref[...]
    a_scr[...] = jnp.exp(_log_sigmoid(la) * (1.0 / GLA_TAU))
    qkv_scr[...] = z_scr[0:n, 0:lay.off_gg]
    copy.wait()


def _sample_state_step(i, a_row, qkv_row, s_ref, s_out_ref, o_ref, lay):
    h_n, dk, dv = lay.h, lay.dk, lay.dv
    q_scale = dk ** -0.5
    rows = [a_row(i, slice(h * dk, (h + 1) * dk)) for h in range(h_n)]
    rows += [qkv_row(i, slice(lay.off_k + h * dk, lay.off_k + (h + 1) * dk)) for h in range(h_n)]
    rows += [qkv_row(i, slice(lay.off_q + h * dk, lay.off_q + (h + 1) * dk)) for h in range(h_n)]
    rows.append(jnp.zeros((dk - len(rows), dk), F32))
    mt = jnp.transpose(jnp.concatenate(rows, axis=0))
    for h in range(h_n):
        a_col = mt[:, h:h + 1]
        k_col = mt[:, h_n + h:h_n + h + 1]
        q_col = mt[:, 2 * h_n + h:2 * h_n + h + 1]
        v = qkv_row(i, slice(lay.off_v + h * dv, lay.off_v + (h + 1) * dv))
        s_new = a_col * s_ref[i, h] + k_col * v
        s_out_ref[i, h] = s_new
        o_ref[i:i + 1, h * dv:(h + 1) * dv] = q_scale * jnp.sum(q_col * s_new, axis=0,
                                                               keepdims=True)


def _prompt_kernel(xc_ref, xn_ref, mod_ref, xs_ref, mods_ref, sst_ref,
                   w_hd_ref, w_lr_ref, w_tl_ref, w_a2_ref, b_a_ref,
                   gng_ref, cw_ref, w_pa_ref, w_pb_ref, w_o_ref, lng_ref, lnb_ref, *rest,
                   lay, layer, tile, alpha, n_t, n_tiles, sample_rows):
    (y_ref, sg_ref, sc_ref, sso_ref, so_ref, zs_hbm_ref,
     ua_scr, ub_scr, za_scr, zb_scr, la_scr, p_scr, s_scr, og_scr, yc_scr, mg_scr,
     qin_scr, kin_scr, kst_scr, vb_scr, dec_scr, att_scr, ds_scr, sbf_scr,
     sa_scr, sqkv_scr, zs_sem) = rest[-27:]
    b_a_ref, gng_ref, lng_ref, lnb_ref = (_layer_rows(r, layer)
                                          for r in (b_a_ref, gng_ref, lng_ref, lnb_ref))
    cw_ref = _layer_taps(cw_ref, layer)
    j = pl.program_id(0)
    steps_per_seq = n_t // 2
    jj = lax.rem(j, steps_per_seq)
    b_cur = lax.div(j, steps_per_seq)
    b_next = lax.div(jnp.minimum(2 * j + 2, n_tiles - 1), n_t)
    d = lay.d
    ck = cw_ref.shape[0]
    halo = SUBLANES
    w_in_refs = (w_hd_ref, w_lr_ref, w_tl_ref)

    @pl.when(jj == 0)
    def _():
        s_scr[...] = jnp.zeros_like(s_scr)
        p_scr[0:halo, :] = jnp.zeros((halo, lay.cw), F32)

    def project(x_rows, brow, u_scr, z_dst):
        return _project_stages(x_rows, mod_ref, brow, w_in_refs, u_scr, z_dst, lay)

    operands = (qin_scr, kin_scr, kst_scr, vb_scr, dec_scr)

    def premix(z_scr):
        return _premix_stages(z_scr, w_a2_ref, b_a_ref, la_scr, *operands, lay, tile)

    def mix(z_scr, row0):
        return _mix_stages(z_scr, xc_ref, y_ref, row0, gate, gng_ref, cw_ref, w_pa_ref, w_pb_ref,
                           w_o_ref, lng_ref, lnb_ref, p_scr, s_scr, og_scr, yc_scr, mg_scr,
                           *operands, att_scr, ds_scr, sbf_scr, lay, tile, alpha)

    @pl.when(j == 0)
    def _():
        _project_samples(xs_ref, mods_ref, w_in_refs, w_a2_ref, b_a_ref, za_scr, sa_scr,
                         sqkv_scr, zs_hbm_ref, zs_sem, lay)
        for _ in project(lambda: xc_ref[0, 0:tile, :], b_cur, ua_scr, za_scr):
            pass
        for _ in premix(za_scr):
            pass

    gate = mod_ref[pl.ds(b_cur, 1), 2 * d:3 * d]
    mix_weights = _mix_weights(lay, tile)
    premix_weights = _premix_weights(lay, tile)
    plan = _project_plan(lay)
    n_proj = 1 + len(plan)
    premix_needs_proj = 1 + sum(1 for _, _, _, d0 in plan if d0 < lay.off_cb)

    steps_per_block = SUBLANES // sample_rows
    sub = lax.rem(j, steps_per_block)
    block = lax.div(j, steps_per_block)
    a_row = _step_row_reader(sa_scr, sample_rows, block, sub, steps_per_block)
    qkv_row = _step_row_reader(sqkv_scr, sample_rows, block, sub, steps_per_block)
    fillers = [functools.partial(_sample_state_step, i, a_row, qkv_row, sst_ref,
                                 sso_ref, so_ref, lay) for i in range(sample_rows)]
    half = sample_rows // 2

    _emit_overlapped(mix(za_scr, 0), mix_weights, premix(zb_scr), premix_weights,
                     project(lambda: xc_ref[0, tile:2 * tile, :], b_cur, ub_scr, zb_scr), n_proj,
                     premix_needs_proj, fillers[:half])
    _emit_overlapped(mix(zb_scr, tile), mix_weights, premix(za_scr), premix_weights,
                     project(lambda: xn_ref[0], b_next, ua_scr, za_scr), n_proj,
                     premix_needs_proj, fillers[half:])

    @pl.when(jj == steps_per_seq - 1)
    def _():
        sg_ref[0] = s_scr[...]
        sc_ref[0] = p_scr[halo - (ck - 1):halo, :]


def _prompt_vmem_bytes(lay, tile, sample_rows, n_s):
    weights = 2 * (lay.d * lay.n_pad + lay.rank_pad * lay.qk + lay.vt * lay.d
                   + lay.cw * lay.d + lay.d * lay.d)
    state = lay.h * lay.dk * lay.dv * 4
    io = (2 * (2 * tile + tile + 2 * tile) * lay.d * 4 + 2 * state
          + 2 * 2 * sample_rows * state + n_s * (3 * lay.d + lay.qk + lay.off_gg) * 4)
    chunk_heads = (tile // GLA_CHUNK) * lay.h
    scratch = (2 * tile * lay.d * 2 + 2 * tile * lay.n_pad * 4 + tile * lay.qk * 4
               + (tile + SUBLANES) * lay.cw * 4 + lay.h * lay.dk * lay.dv * 4
               + tile * lay.vt * 2 + tile * lay.cw * 2 + tile * lay.d * 2
               + 3 * tile * lay.qk * 2 + tile * lay.vt * 2 + SUBLANES * lay.qk * 4
               + lay.h * tile * LANES * 2 + chunk_heads * lay.dk * lay.dv * 6)
    temporaries = tile * lay.n_pad * 2
    return min(weights + io + scratch + temporaries, VMEM_PHYSICAL_BYTES * 15 // 16)


def _prompt_layer(x, mod_p, x_sample, mod_s, sample_state, params, prev_states, layer, lay,
                  alpha):
    (w_hd, w_lr, w_tl, w_a2, b_a, gng, conv_w, w_pa, w_pb, w_o, ln_g, ln_b) = params
    bsz, seq, d = x.shape
    depth = w_hd.shape[0]
    tile = PROMPT_TILE
    assert seq % (2 * tile) == 0 and tile % GLA_CHUNK == 0
    ck = conv_w.shape[0]
    n_t = seq // tile
    n_tiles = bsz * n_t
    n_steps = n_tiles // 2
    steps_per_seq = n_t // 2
    n_chunk_heads = (tile // GLA_CHUNK) * lay.h
    assert tile // GLA_CHUNK <= SUBLANES
    n_s = sample_state.shape[1]
    sample_rows = n_s // n_steps
    assert sample_rows * n_steps == n_s and sample_rows % 2 == 0 and SUBLANES % sample_rows == 0
    steps_per_block = SUBLANES // sample_rows

    def weight(arr):
        return _layer_spec(arr.shape[1:], layer, True)

    def next_tile(j):
        g = jnp.minimum(2 * j + 2, n_tiles - 1)
        return (g // n_t, g % n_t, 0)

    sample_state_spec = pl.BlockSpec((None, sample_rows, lay.h, lay.dk, lay.dv),
                                     lambda j: (layer, j, 0, 0, 0))
    in_specs = [
        pl.BlockSpec((1, 2 * tile, d), lambda j: (j // steps_per_seq, j % steps_per_seq, 0)),
        pl.BlockSpec((1, tile, d), next_tile),
        _layer_spec(mod_p.shape[1:], 0),
        pl.BlockSpec((n_s, d), lambda j: (0, 0), pipeline_mode=pl.Buffered(1)),
        pl.BlockSpec((None, n_s, 2 * d), lambda j: (0, 0, 0), pipeline_mode=pl.Buffered(1)),
        sample_state_spec,
        weight(w_hd), weight(w_lr), weight(w_tl), weight(w_a2), _whole_spec(b_a),
        _whole_spec(gng), _whole_spec(conv_w), weight(w_pa), weight(w_pb), weight(w_o),
        _whole_spec(ln_g), _whole_spec(ln_b),
    ]
    args = [x, x, mod_p, x_sample, mod_s, sample_state, *params]
    aliases = {}
    if prev_states is not None:
        for i, prev in enumerate(prev_states):
            in_specs.append(pl.BlockSpec(memory_space=pl.ANY))
            aliases[len(args)] = 1 + i
            args.append(prev)

    kern = functools.partial(_prompt_kernel, lay=lay, layer=layer, tile=tile, alpha=alpha,
                             n_t=n_t, n_tiles=n_tiles, sample_rows=sample_rows)
    return pl.pallas_call(
        kern,
        grid=(n_steps,),
        in_specs=in_specs,
        out_specs=[
            pl.BlockSpec((1, 2 * tile, d), lambda j: (j // steps_per_seq, j % steps_per_seq, 0)),
            pl.BlockSpec((None, 1, lay.h, lay.dk, lay.dv),
                         lambda j: (layer, j // steps_per_seq, 0, 0, 0)),
            pl.BlockSpec((None, 1, ck - 1, lay.cw), lambda j: (layer, j // steps_per_seq, 0, 0)),
            sample_state_spec,
            pl.BlockSpec((None, sample_rows, lay.vt), lambda j: (j, 0, 0)),
            pl.BlockSpec(memory_space=pl.ANY),
        ],
        out_shape=[
            jax.ShapeDtypeStruct((bsz, seq, d), F32),
            jax.ShapeDtypeStruct((depth, bsz, lay.h, lay.dk, lay.dv), F32),
            jax.ShapeDtypeStruct((depth, bsz, ck - 1, lay.cw), F32),
            jax.ShapeDtypeStruct(sample_state.shape, F32),
            jax.ShapeDtypeStruct((n_steps, sample_rows, lay.vt), F32),
            jax.ShapeDtypeStruct((n_s, lay.n_pad), F32),
        ],
        scratch_shapes=[
            pltpu.VMEM((tile, d), BF16),
            pltpu.VMEM((tile, d), BF16),
            pltpu.VMEM((tile, lay.n_pad), F32),
            pltpu.VMEM((tile, lay.n_pad), F32),
            pltpu.VMEM((tile, lay.qk), F32),
            pltpu.VMEM((tile + SUBLANES, lay.cw), F32),
            pltpu.VMEM((lay.h, lay.dk, lay.dv), F32),
            pltpu.VMEM((tile, lay.vt), BF16),
            pltpu.VMEM((tile, lay.cw), BF16),
            pltpu.VMEM((tile, d), BF16),
            pltpu.VMEM((tile, lay.qk), BF16),
            pltpu.VMEM((tile, lay.qk), BF16),
            pltpu.VMEM((tile, lay.qk), BF16),
            pltpu.VMEM((tile, lay.vt), BF16),
            pltpu.VMEM((SUBLANES, lay.qk), F32),
            pltpu.VMEM((lay.h, tile, GLA_CHUNK), BF16),
            pltpu.VMEM((n_chunk_heads, lay.dk, lay.dv), F32),
            pltpu.VMEM((n_chunk_heads, lay.dk, lay.dv), BF16),
            pltpu.VMEM((n_s, lay.qk), F32),
            pltpu.VMEM((n_s, lay.off_gg), F32),
            pltpu.SemaphoreType.DMA(()),
        ],
        input_output_aliases=aliases,
        compiler_params=pltpu.CompilerParams(
            dimension_semantics=("arbitrary",),
            vmem_limit_bytes=_prompt_vmem_bytes(lay, tile, sample_rows, n_s)),
        name="prompt_layer",
    )(*args)


def _sample_out_kernel(o_ref, z_ref, x_ref, mod_ref, buf_ref, gng_ref, cw_ref,
                       w_pa_ref, w_pb_ref, w_o_ref, lng_ref, lnb_ref, *rest, lay, layer, alpha):
    y_ref, buf_out_ref, og_scr = rest[-3:]
    d, h_n, dv, cw = lay.d, lay.h, lay.dv, lay.cw
    gng_ref, lng_ref, lnb_ref = (_layer_rows(r, layer) for r in (gng_ref, lng_ref, lnb_ref))
    cw_ref = _layer_taps(cw_ref, layer)
    ck = cw_ref.shape[0]
    gate = mod_ref[:, 2 * d:3 * d]
    gng = gng_ref[...]
    for h in range(h_n):
        o = o_ref[:, h * dv:(h + 1) * dv]
        ms = jnp.mean(o * o, axis=-1, keepdims=True)
        on = o * lax.rsqrt(ms + EPS) * gng
        gg = z_ref[:, lay.off_gg + h * dv:lay.off_gg + (h + 1) * dv]
        og_scr[:, h * dv:(h + 1) * dv] = (on * _silu(gg)).astype(BF16)
    pa = jnp.dot(og_scr[...], _unpack_rows(w_pa_ref[...]), preferred_element_type=F32)

    p = z_ref[:, lay.off_cc:lay.off_cc + cw] * z_ref[:, lay.off_ch:lay.off_ch + cw]
    yc = cw_ref[ck - 1:ck, :] * p
    for i in range(ck - 1):
        yc = yc + cw_ref[i:i + 1, :] * buf_ref[:, i * cw:(i + 1) * cw]
    yc = z_ref[:, lay.off_cb:lay.off_cb + cw] * yc * _silu(z_ref[:, lay.off_gc:lay.off_gc + cw])
    pb = jnp.dot(yc.astype(BF16), _unpack_rows(w_pb_ref[...]), preferred_element_type=F32)
    for i in range(ck - 2):
        buf_out_ref[:, i * cw:(i + 1) * cw] = buf_ref[:, (i + 1) * cw:(i + 2) * cw]
    buf_out_ref[:, (ck - 2) * cw:(ck - 1) * cw] = p

    merged = (jax.nn.sigmoid(z_ref[:, lay.off_mg:lay.off_mg + d]) * pa
              + jax.nn.sigmoid(z_ref[:, lay.off_mc:lay.off_mc + d]) * pb)
    out = jnp.dot(merged.astype(BF16), _unpack_rows(w_o_ref[...]),
                  preferred_element_type=F32)
    hres = alpha * x_ref[...] + gate * out
    mu = jnp.mean(hres, axis=-1, keepdims=True)
    dev = hres - mu
    var = jnp.mean(dev * dev, axis=-1, keepdims=True)
    y_ref[...] = dev * lax.rsqrt(var + EPS) * lng_ref[...] + lnb_ref[...]


def _sample_out(o, z, x, mod_s, conv_state, prev_conv_out, gng, conv_w, w_pa, w_pb, w_o,
                ln_g, ln_b, layer, lay, alpha):
    n, d = x.shape
    kern = functools.partial(_sample_out_kernel, lay=lay, layer=layer, alpha=alpha)
    conv_spec = _layer_spec(conv_state.shape[1:], layer)
    in_specs = [_whole_spec(o), _whole_spec(z), _whole_spec(x),
                _layer_spec(mod_s.shape[1:], 0), conv_spec,
                _whole_spec(gng), _whole_spec(conv_w), _layer_spec(w_pa.shape[1:], layer),
                _layer_spec(w_pb.shape[1:], layer), _layer_spec(w_o.shape[1:], layer),
                _whole_spec(ln_g), _whole_spec(ln_b)]
    args = [o, z, x, mod_s, conv_state, gng, conv_w, w_pa, w_pb, w_o, ln_g, ln_b]
    aliases = {}
    if prev_conv_out is not None:
        in_specs.append(pl.BlockSpec(memory_space=pl.ANY))
        aliases = {len(args): 1}
        args.append(prev_conv_out)
    return pl.pallas_call(
        kern,
        grid=(1,),
        in_specs=in_specs,
        out_specs=[pl.BlockSpec((n, d), lambda i: (0, 0)), conv_spec],
        out_shape=[jax.ShapeDtypeStruct((n, d), F32),
                   jax.ShapeDtypeStruct(conv_state.shape, F32)],
        scratch_shapes=[pltpu.VMEM((n, lay.vt), BF16)],
        input_output_aliases=aliases,
        compiler_params=pltpu.CompilerParams(dimension_semantics=("arbitrary",),
                                             vmem_limit_bytes=STREAMING_VMEM_BYTES),
        name="sample_out",
    )(*args)


def kernel(x_prompt, x_sample, c_prompt, c_sample, state_gla, state_conv, w_ada, b_ada, w_in,
           w_a2, b_a, gla_norm_g, conv_w, w_pa, w_pb, w_o, ln_g, ln_b):
    depth, n_s, heads, dk, dv = state_gla.shape
    d = x_prompt.shape[2]
    rank = w_a2.shape[1]
    ck, cw = conv_w.shape[1:]
    lay = _Layout(d, heads, dk, dv, rank, cw)
    alpha = (2 * depth) ** 0.25

    w_hd, w_lr, w_tl = _split_w_in(w_in, lay)
    w_pa_b, w_pb_b, w_o_b, w_a2_p = _cast_weights((w_pa, w_pb, w_o), w_a2, lay.rank_pad)
    taps = jnp.swapaxes(conv_w, 0, 1)
    params = (w_hd, w_lr, w_tl, w_a2_p, b_a, gla_norm_g, taps, w_pa_b, w_pb_b, w_o_b, ln_g, ln_b)

    hp, hs = x_prompt, x_sample.reshape(n_s, d)
    states = None
    conv_in = state_conv.reshape(depth, n_s, (ck - 1) * cw)
    conv_s = None
    z = None
    for l in range(depth):
        mod_p, mod_s = _ada_mod(c_prompt, c_sample, w_ada, b_ada, l, after=z)
        hp, *states, o, z = _prompt_layer(hp, mod_p, hs, mod_s, state_gla, params, states, l, lay,
                                          alpha)
        hs, conv_s = _sample_out(o.reshape(n_s, lay.vt), z, hs, mod_s, conv_in, conv_s,
                                 gla_norm_g, taps, w_pa_b, w_pb_b, w_o_b, ln_g, ln_b, l, lay,
                                 alpha)
    return (hp, hs.reshape(x_sample.shape), states[0], states[1], states[2],
            conv_s.reshape(state_conv.shape))
```

```python
import functools

import jax
import jax.numpy as jnp
from jax import lax
from jax.experimental import pallas as pl
from jax.experimental.pallas import tpu as pltpu

F32 = jnp.float32
BF16 = jnp.bfloat16

GLA_TAU = 16.0
GLA_CHUNK = 64
EPS = 1e-5

LANES = 128
SUBLANES = 8
VMEM_PHYSICAL_BYTES = 64 * 1024 * 1024
MIB = 1024 * 1024
STREAMING_VMEM_BYTES = VMEM_PHYSICAL_BYTES * 7 // 8

PROMPT_TILE = 256
PROJ_COLS = 256
PREP_STEPS = 2
ADA_STEPS = 2
CAST_STEPS = 4


LOG2_E = 1.4426950408889634


def _log_sigmoid(x):
    return jnp.minimum(x, 0.0) - jnp.log(1.0 + jnp.exp(-jnp.abs(x)))


def _cumsum_rows(x):
    row = lax.broadcasted_iota(jnp.int32, (SUBLANES, x.shape[1]), 0)
    groups = []
    carry = None
    for g in range(x.shape[0] // SUBLANES):
        t = x[g * SUBLANES:(g + 1) * SUBLANES, :]
        s = 1
        while s < SUBLANES:
            t = t + jnp.where(row >= s, pltpu.roll(t, s, 0), 0.0)
            s *= 2
        if carry is not None:
            t = t + carry
        carry = jnp.broadcast_to(t[SUBLANES - 1:SUBLANES, :], t.shape)
        groups.append(t)
    return jnp.concatenate(groups, axis=0)


def _silu(x):
    return x * jax.nn.sigmoid(x)


PACKED = jnp.uint32
PACK = 2


def _pack_rows(w_bf16):
    return pltpu.bitcast(w_bf16, PACKED)


def _unpack_rows(w_packed):
    return pltpu.bitcast(w_packed, BF16)


class _Layout:
    def __init__(self, d_model, heads, dk, dv, rank, conv_w):
        self.d = d_model
        self.h = heads
        self.dk = dk
        self.dv = dv
        self.rank = rank
        self.cw = conv_w
        self.qk = heads * dk
        self.vt = heads * dv
        self.rank_pad = -(-rank // LANES) * LANES
        self.off_q = 0
        self.off_k = self.off_q + self.qk
        self.off_v = self.off_k + self.qk
        self.off_gg = self.off_v + self.vt
        self.off_a = self.off_gg + self.vt
        self.off_cb = self.off_a + self.rank_pad
        self.off_cc = self.off_cb + conv_w
        self.off_ch = self.off_cc + conv_w
        self.off_gc = self.off_ch + conv_w
        self.off_mg = self.off_gc + conv_w
        self.off_mc = self.off_mg + d_model
        self.n_pad = self.off_mc + d_model
        self.n_head = self.off_a
        self.n_tail = self.n_pad - self.off_cb
        self.groups = (
            (0, self.off_q, self.off_v, self.off_q),
            (0, self.off_v, self.off_gg, self.off_v),
            (0, self.off_gg, self.off_a, self.off_gg),
            (1, 0, self.rank_pad, self.off_a),
            (2, 0, 2 * conv_w, self.off_cb),
            (2, 2 * conv_w, 4 * conv_w, self.off_ch),
            (2, 4 * conv_w, 4 * conv_w + d_model, self.off_mg),
            (2, 4 * conv_w + d_model, self.n_tail, self.off_mc))


def _layer_spec(shape, layer, single_buffer=False):
    zeros = (0,) * len(shape)
    kwargs = {"pipeline_mode": pl.Buffered(1)} if single_buffer else {}
    return pl.BlockSpec((None,) + tuple(shape), lambda *_: (layer,) + zeros, **kwargs)


def _whole_spec(arr):
    return pl.BlockSpec(arr.shape, lambda *_: (0,) * arr.ndim)


def _layer_rows(ref, layer):
    return ref.at[pl.ds(layer, 1)]


def _layer_taps(ref, layer):
    return ref.at[:, layer]


def _transpose_cast_kernel(w_ref, o_ref, *, valid):
    wt = jnp.transpose(w_ref[0])
    if valid < wt.shape[1]:
        lane = lax.broadcasted_iota(jnp.int32, wt.shape, 1)
        wt = jnp.where(lane < valid, wt, 0.0)
    o_ref[...] = _pack_rows(wt.astype(BF16))


def _transpose_cast(w_t, col0, n_valid, n_out, cols):
    depth, n_in, d = w_t.shape
    assert n_out % cols == 0 and col0 % SUBLANES == 0 and col0 + n_out <= n_in
    kern = functools.partial(_transpose_cast_kernel, valid=min(n_valid, cols))
    return pl.pallas_call(
        kern,
        grid=(depth, n_out // cols),
        in_specs=[pl.BlockSpec((pl.Element(1), pl.Element(cols), pl.Element(d)),
                               lambda l, j: (l, pl.multiple_of(col0 + j * cols, SUBLANES), 0))],
        out_specs=pl.BlockSpec((None, d // PACK, cols), lambda l, j: (l, 0, j)),
        out_shape=jax.ShapeDtypeStruct((depth, d // PACK, n_out), PACKED),
        compiler_params=pltpu.CompilerParams(
            dimension_semantics=("arbitrary", "arbitrary"),
            vmem_limit_bytes=STREAMING_VMEM_BYTES),
        name="transpose_cast",
    )(w_t)


def _split_w_in(w_in, lay):
    n_in = w_in.shape[2]
    a_end = lay.off_a + lay.rank
    assert n_in == a_end + lay.n_tail and lay.rank <= lay.rank_pad == LANES
    w_t = jnp.swapaxes(w_in, 1, 2)
    head = _transpose_cast(w_t, 0, lay.n_head, lay.n_head, lay.n_head // PREP_STEPS)
    low = _transpose_cast(w_t, lay.off_a, lay.rank, lay.rank_pad, lay.rank_pad)
    tail = _transpose_cast(w_t, a_end, lay.n_tail, lay.n_tail, lay.n_tail // PREP_STEPS)
    return head, low, tail


def _cast_kernel(*refs, rank_pad):
    n = len(refs) // 2
    w_a2_ref, o_a2_ref = refs[n - 1], refs[-1]
    for w_ref, o_ref in zip(refs[:n - 1], refs[n:-1]):
        o_ref[...] = _pack_rows(w_ref[...].astype(BF16))
    w_a2 = w_a2_ref[...]
    pad = jnp.zeros((rank_pad - w_a2.shape[0], w_a2.shape[1]), F32)
    o_a2_ref[...] = _pack_rows(jnp.concatenate([w_a2, pad], axis=0).astype(BF16))


def _cast_weights(weights, w_a2, rank_pad):
    depth = w_a2.shape[0]
    for w in weights:
        assert w.shape[1] % (CAST_STEPS * PACK * SUBLANES) == 0

    def rows_spec(rows, cols):
        return pl.BlockSpec((None, rows, cols), lambda l, r: (l, r, 0))

    def layer_spec(rows, cols):
        return pl.BlockSpec((None, rows, cols), lambda l, r: (l, 0, 0))

    kern = functools.partial(_cast_kernel, rank_pad=rank_pad)
    return pl.pallas_call(
        kern,
        grid=(depth, CAST_STEPS),
        in_specs=[rows_spec(w.shape[1] // CAST_STEPS, w.shape[2]) for w in weights]
        + [layer_spec(*w_a2.shape[1:])],
        out_specs=[rows_spec(w.shape[1] // (CAST_STEPS * PACK), w.shape[2]) for w in weights]
        + [layer_spec(rank_pad // PACK, w_a2.shape[2])],
        out_shape=[jax.ShapeDtypeStruct((depth, w.shape[1] // PACK, w.shape[2]), PACKED)
                   for w in weights]
        + [jax.ShapeDtypeStruct((depth, rank_pad // PACK, w_a2.shape[2]), PACKED)],
        compiler_params=pltpu.CompilerParams(
            dimension_semantics=("arbitrary", "arbitrary"),
            vmem_limit_bytes=STREAMING_VMEM_BYTES),
        name="cast_weights",
    )(*weights, w_a2)


def _ada_kernel(cp_ref, cs_ref, w_ref, b_ref, op_ref, os_ref):
    bp = cp_ref.shape[0]
    c = jnp.concatenate([cp_ref[...], cs_ref[...]], axis=0).astype(BF16)
    mod = jnp.dot(c, w_ref[...].astype(BF16), preferred_element_type=F32) + b_ref[...]
    op_ref[...] = mod[0:bp]
    os_ref[...] = mod[bp:]


def _ada_mod(c_prompt, c_sample, w_ada, b_ada):
    depth, d, n3 = w_ada.shape
    bp, ns = c_prompt.shape[0], c_sample.shape[0]
    tn = n3 // ADA_STEPS
    return pl.pallas_call(
        _ada_kernel,
        grid=(depth, n3 // tn),
        in_specs=[
            pl.BlockSpec((bp, d), lambda l, j: (0, 0)),
            pl.BlockSpec((ns, d), lambda l, j: (0, 0)),
            pl.BlockSpec((None, d, tn), lambda l, j: (l, 0, j)),
            pl.BlockSpec((None, 1, tn), lambda l, j: (l, 0, j)),
        ],
        out_specs=[pl.BlockSpec((None, bp, tn), lambda l, j: (l, 0, j)),
                   pl.BlockSpec((None, ns, tn), lambda l, j: (l, 0, j))],
        out_shape=[jax.ShapeDtypeStruct((depth, bp, n3), F32),
                   jax.ShapeDtypeStruct((depth, ns, n3), F32)],
        compiler_params=pltpu.CompilerParams(
            dimension_semantics=("arbitrary", "arbitrary"),
            vmem_limit_bytes=STREAMING_VMEM_BYTES),
        name="ada_mod",
    )(c_prompt, c_sample, w_ada, b_ada.reshape(depth, 1, n3))


def _project_plan(lay):
    plan = []
    for piece, s0, s1, d0 in lay.groups:
        for c0 in range(s0, s1, PROJ_COLS):
            plan.append((piece, c0, min(c0 + PROJ_COLS, s1), d0 + c0 - s0))
    return plan


def _project_stages(x_rows, mod_ref, brow, w_in_refs, u_scr, z_dst, lay):
    d = lay.d
    shift = mod_ref[pl.ds(brow, 1), 0:d]
    scale = mod_ref[pl.ds(brow, 1), d:2 * d]
    u_scr[...] = (x_rows() * (1.0 + scale) + shift).astype(BF16)
    yield
    for piece, c0, c1, d0 in _project_plan(lay):
        z_dst[:, d0:d0 + c1 - c0] = jnp.dot(u_scr[...],
                                            _unpack_rows(w_in_refs[piece][:, c0:c1]),
                                            preferred_element_type=F32)
        yield


def _tile_chunks(tile):
    return [(c, c * GLA_CHUNK, (c + 1) * GLA_CHUNK) for c in range(tile // GLA_CHUNK)]


def _tile_heads(lay):
    return [(h, slice(h * lay.dk, (h + 1) * lay.dk), slice(h * lay.dv, (h + 1) * lay.dv))
            for h in range(lay.h)]


W_DECAY, W_FACTORS, W_SCORES, W_INCR, W_RECUR, W_OUT, W_NORM, W_CONV, W_MERGE, W_LN = (
    4, 4, 1, 1, 5, 1, 11, 10, 10, 10)


def _premix_weights(lay, tile):
    return [W_DECAY] + [W_FACTORS] * (len(_tile_chunks(tile)) * lay.h)


def _premix_stages(z_scr, w_a2_ref, b_a_ref, la_scr, qin_scr, kin_scr, kst_scr, vb_scr, dec_scr,
                   lay, tile):
    dk, dv = lay.dk, lay.dv
    q_scale = dk ** -0.5

    a_lr = z_scr[:, lay.off_a:lay.off_a + lay.rank_pad].astype(BF16)
    la_scr[...] = (jnp.dot(a_lr, _unpack_rows(w_a2_ref[...]), preferred_element_type=F32)
                   + b_a_ref[...])
    yield W_DECAY

    for c, r0, r1 in _tile_chunks(tile):
        for h, ks, vs in _tile_heads(lay):
            log_a = _log_sigmoid(la_scr[r0:r1, ks]) * (1.0 / GLA_TAU)
            b2 = _cumsum_rows(log_a) * LOG2_E
            bl2 = b2[GLA_CHUNK - 1:GLA_CHUNK, :]
            q_in = z_scr[r0:r1, lay.off_q + h * dk:lay.off_q + (h + 1) * dk] * q_scale
            k = z_scr[r0:r1, lay.off_k + h * dk:lay.off_k + (h + 1) * dk]
            qin_scr[r0:r1, ks] = (q_in * jnp.exp2(b2)).astype(BF16)
            kin_scr[r0:r1, ks] = (k * jnp.exp2(-b2)).astype(BF16)
            kst_scr[r0:r1, ks] = (k * jnp.exp2(bl2 - b2)).astype(BF16)
            dec_scr[c:c + 1, ks] = jnp.exp2(bl2)
            vb_scr[r0:r1, vs] = z_scr[r0:r1, lay.off_v + h * dv:lay.off_v + (h + 1) * dv] \
                .astype(BF16)
            yield W_FACTORS


def _mix_weights(lay, tile):
    chunks = len(_tile_chunks(tile))
    head = ([W_SCORES] * chunks + [W_INCR] * lay.h + [W_RECUR] * lay.h + [W_OUT] * chunks
            + [W_NORM] * chunks + [W_CONV] * chunks)
    tail = [0, 0] + [W_MERGE] * chunks + [0] + [W_LN] * chunks
    return head, tail


def _mix_stages(z_scr, x_ref, y_ref, row0, gate, gng_ref, cw_ref,
                w_pa_ref, w_pb_ref, w_o_ref, lng_ref, lnb_ref,
                p_scr, s_scr, og_scr, yc_scr, mg_scr,
                qin_scr, kin_scr, kst_scr, vb_scr, dec_scr, att_scr, ds_scr, sbf_scr,
                lay, tile, alpha):
    d, h_n, dk, dv, cw = lay.d, lay.h, lay.dk, lay.dv, lay.cw
    ck = cw_ref.shape[0]
    halo = SUBLANES
    chunks = _tile_chunks(tile)
    heads = _tile_heads(lay)
    causal = (lax.broadcasted_iota(jnp.int32, (GLA_CHUNK, GLA_CHUNK), 0)
              >= lax.broadcasted_iota(jnp.int32, (GLA_CHUNK, GLA_CHUNK), 1))
    gng = gng_ref[...]

    for c, r0, r1 in chunks:
        for h, ks, vs in heads:
            att = lax.dot_general(qin_scr[r0:r1, ks], kin_scr[r0:r1, ks],
                                  (((1,), (1,)), ((), ())), preferred_element_type=F32)
            att_scr[h, r0:r1, :] = jnp.where(causal, att, 0.0).astype(BF16)
        yield W_SCORES

    for h, ks, vs in heads:
        for c, r0, r1 in chunks:
            ds_scr[c * h_n + h] = lax.dot_general(kst_scr[r0:r1, ks], vb_scr[r0:r1, vs],
                                                  (((0,), (0,)), ((), ())),
                                                  preferred_element_type=F32)
        yield W_INCR
    for h, ks, vs in heads:
        state = s_scr[h]
        for c, r0, r1 in chunks:
            sbf_scr[c * h_n + h] = state.astype(BF16)
            dec = jnp.transpose(jnp.broadcast_to(dec_scr[c:c + 1, ks], (dk, dk)))
            dec = jnp.concatenate([dec] * (dv // dk), axis=1)
            state = dec * state + ds_scr[c * h_n + h]
        s_scr[h] = state
        yield W_RECUR

    for c, r0, r1 in chunks:
        for h, ks, vs in heads:
            lhs = jnp.concatenate([qin_scr[r0:r1, ks], att_scr[h, r0:r1, :]], axis=1)
            rhs = jnp.concatenate([sbf_scr[c * h_n + h], vb_scr[r0:r1, vs]], axis=0)
            z_scr[r0:r1, vs] = jnp.dot(lhs, rhs, preferred_element_type=F32)
        yield W_OUT

    for c, r0, r1 in chunks:
        for h, ks, vs in heads:
            o = z_scr[r0:r1, vs]
            ms = jnp.mean(o * o, axis=-1, keepdims=True)
            on = o * lax.rsqrt(ms + EPS) * gng
            gg = z_scr[r0:r1, lay.off_gg + h * dv:lay.off_gg + (h + 1) * dv]
            og_scr[r0:r1, vs] = (on * _silu(gg)).astype(BF16)
        yield W_NORM

    for c, r0, r1 in chunks:
        for g0 in range(0, cw, LANES):
            gl = slice(g0, g0 + LANES)
            p = (z_scr[r0:r1, lay.off_cc + g0:lay.off_cc + g0 + LANES]
                 * z_scr[r0:r1, lay.off_ch + g0:lay.off_ch + g0 + LANES])
            p_scr[halo + r0:halo + r1, gl] = p
            ext = jnp.concatenate([p_scr[r0:halo + r0, gl], p], axis=0)
            yc = cw_ref[ck - 1:ck, gl] * p
            for i in range(ck - 1):
                lag = ck - 1 - i
                yc = yc + cw_ref[i:i + 1, gl] * pltpu.roll(ext, lag, 0)[halo:, :]
            yc = (z_scr[r0:r1, lay.off_cb + g0:lay.off_cb + g0 + LANES] * yc
                  * _silu(z_scr[r0:r1, lay.off_gc + g0:lay.off_gc + g0 + LANES]))
            yc_scr[r0:r1, gl] = yc.astype(BF16)
        yield W_CONV

    p_scr[0:halo, :] = p_scr[tile:tile + halo, :]

    z_scr[:, 0:d] = jnp.dot(og_scr[...], _unpack_rows(w_pa_ref[...]), preferred_element_type=F32)
    yield 0
    z_scr[:, d:2 * d] = jnp.dot(yc_scr[...], _unpack_rows(w_pb_ref[...]),
                                preferred_element_type=F32)
    yield 0
    for c, r0, r1 in chunks:
        for g0 in range(0, d, 2 * LANES):
            g1 = g0 + 2 * LANES
            merged = (jax.nn.sigmoid(z_scr[r0:r1, lay.off_mg + g0:lay.off_mg + g1])
                      * z_scr[r0:r1, g0:g1]
                      + jax.nn.sigmoid(z_scr[r0:r1, lay.off_mc + g0:lay.off_mc + g1])
                      * z_scr[r0:r1, d + g0:d + g1])
            mg_scr[r0:r1, g0:g1] = merged.astype(BF16)
        yield W_MERGE
    z_scr[:, 2 * d:3 * d] = jnp.dot(mg_scr[...], _unpack_rows(w_o_ref[...]),
                                    preferred_element_type=F32)
    yield 0

    lng = lng_ref[...]
    lnb = lnb_ref[...]
    for c, r0, r1 in chunks:
        hres = alpha * x_ref[0, row0 + r0:row0 + r1, :] + gate * z_scr[r0:r1, 2 * d:3 * d]
        mu = jnp.mean(hres, axis=-1, keepdims=True)
        dev = hres - mu
        var = jnp.mean(dev * dev, axis=-1, keepdims=True)
        y_ref[0, row0 + r0:row0 + r1, :] = dev * lax.rsqrt(var + EPS) * lng + lnb
        yield W_LN


def _emit_overlapped(mix, mix_weights, premix, premix_weights, proj, n_proj, premix_needs_proj,
                     fillers):
    head, tail = mix_weights
    order = [("mix", w) for w in head]
    done = 0
    for i, w in enumerate(tail):
        while done * len(tail) < i * len(premix_weights):
            order.append(("premix", premix_weights[done]))
            done += 1
        order.append(("mix", w))
    order += [("premix", w) for w in premix_weights[done:]]

    total = sum(w for _, w in order)
    matmul_slots = sum(1 for _, w in order if w == 0)
    done_w = done_p = done_f = slot = 0
    for source, weight in order:
        done_w += weight
        while done_p * total < done_w * n_proj or (source == "premix"
                                                    and done_p < premix_needs_proj):
            next(proj)
            done_p += 1
        if weight == 0:
            slot += 1
            while done_f * matmul_slots < slot * len(fillers):
                fillers[done_f]()
                done_f += 1
        assert next(mix if source == "mix" else premix) == weight
    while done_p < n_proj:
        next(proj)
        done_p += 1
    assert done_f == len(fillers) and next(mix, None) is None and next(premix, None) is None


def _step_row_reader(ref, rows, block, sub, steps_per_block):
    base = pl.multiple_of(block * SUBLANES, SUBLANES)

    def read(i, lanes):
        blk = ref[pl.ds(base, SUBLANES), lanes]
        val = blk[i:i + 1, :]
        for s in range(1, steps_per_block):
            val = jnp.where(sub == s, blk[s * rows + i:s * rows + i + 1, :], val)
        return val
    return read


def _project_samples(x_ref, mod_ref, w_in_refs, w_a2_ref, b_a_ref, z_scr, a_scr, qkv_scr,
                     z_hbm_ref, sem, lay):
    n, d = x_ref.shape
    shift = mod_ref[:, 0:d]
    scale = mod_ref[:, d:2 * d]
    u = (x_ref[...] * (1.0 + scale) + shift).astype(BF16)
    for piece, s0, s1, d0 in lay.groups:
        z_scr[0:n, d0:d0 + s1 - s0] = jnp.dot(u, _unpack_rows(w_in_refs[piece][:, s0:s1]),
                                              preferred_element_type=F32)
    copy = pltpu.make_async_copy(z_scr.at[pl.ds(0, n)], z_hbm_ref, sem)
    copy.start()
    a_lr = z_scr[0:n, lay.off_a:lay.off_a + lay.rank_pad].astype(BF16)
    la = jnp.dot(a_lr, _unpack_rows(w_a2_ref[...]), preferred_element_type=F32) + b_a_ref[...]
    a_scr[...] = jnp.exp(_log_sigmoid(la) * (1.0 / GLA_TAU))
    qkv_scr[...] = z_scr[0:n, 0:lay.off_gg]
    copy.wait()


def _sample_state_step(i, a_row, qkv_row, s_ref, s_out_ref, o_ref, lay):
    h_n, dk, dv = lay.h, lay.dk, lay.dv
    q_scale = dk ** -0.5
    rows = [a_row(i, slice(h * dk, (h + 1) * dk)) for h in range(h_n)]
    rows += [qkv_row(i, slice(lay.off_k + h * dk, lay.off_k + (h + 1) * dk)) for h in range(h_n)]
    rows += [qkv_row(i, slice(lay.off_q + h * dk, lay.off_q + (h + 1) * dk)) for h in range(h_n)]
    rows.append(jnp.zeros((dk - len(rows), dk), F32))
    mt = jnp.transpose(jnp.concatenate(rows, axis=0))
    for h in range(h_n):
        a_col = mt[:, h:h + 1]
        k_col = mt[:, h_n + h:h_n + h + 1]
        q_col = mt[:, 2 * h_n + h:2 * h_n + h + 1]
        v = qkv_row(i, slice(lay.off_v + h * dv, lay.off_v + (h + 1) * dv))
        s_new = a_col * s_ref[i, h] + k_col * v
        s_out_ref[i, h] = s_new
        o_ref[i:i + 1, h * dv:(h + 1) * dv] = q_scale * jnp.sum(q_col * s_new, axis=0,
                                                               keepdims=True)


def _prompt_kernel(xc_ref, xn_ref, mod_ref, xs_ref, mods_ref, sst_ref,
                   w_hd_ref, w_lr_ref, w_tl_ref, w_a2_ref, b_a_ref,
                   gng_ref, cw_ref, w_pa_ref, w_pb_ref, w_o_ref, lng_ref, lnb_ref, *rest,
                   lay, layer, tile, alpha, n_t, n_tiles, sample_rows):
    (y_ref, sg_ref, sc_ref, sso_ref, so_ref, zs_hbm_ref,
     ua_scr, ub_scr, za_scr, zb_scr, la_scr, p_scr, s_scr, og_scr, yc_scr, mg_scr,
     qin_scr, kin_scr, kst_scr, vb_scr, dec_scr, att_scr, ds_scr, sbf_scr,
     sa_scr, sqkv_scr, zs_sem) = rest[-27:]
    b_a_ref, gng_ref, lng_ref, lnb_ref = (_layer_rows(r, layer)
                                          for r in (b_a_ref, gng_ref, lng_ref, lnb_ref))
    cw_ref = _layer_taps(cw_ref, layer)
    j = pl.program_id(0)
    steps_per_seq = n_t // 2
    jj = lax.rem(j, steps_per_seq)
    b_cur = lax.div(j, steps_per_seq)
    b_next = lax.div(jnp.minimum(2 * j + 2, n_tiles - 1), n_t)
    d = lay.d
    ck = cw_ref.shape[0]
    halo = SUBLANES
    w_in_refs = (w_hd_ref, w_lr_ref, w_tl_ref)

    @pl.when(jj == 0)
    def _():
        s_scr[...] = jnp.zeros_like(s_scr)
        p_scr[0:halo, :] = jnp.zeros((halo, lay.cw), F32)

    def project(x_rows, brow, u_scr, z_dst):
        return _project_stages(x_rows, mod_ref, brow, w_in_refs, u_scr, z_dst, lay)

    operands = (qin_scr, kin_scr, kst_scr, vb_scr, dec_scr)

    def premix(z_scr):
        return _premix_stages(z_scr, w_a2_ref, b_a_ref, la_scr, *operands, lay, tile)

    def mix(z_scr, row0):
        return _mix_stages(z_scr, xc_ref, y_ref, row0, gate, gng_ref, cw_ref, w_pa_ref, w_pb_ref,
                           w_o_ref, lng_ref, lnb_ref, p_scr, s_scr, og_scr, yc_scr, mg_scr,
                           *operands, att_scr, ds_scr, sbf_scr, lay, tile, alpha)

    @pl.when(j == 0)
    def _():
        _project_samples(xs_ref, mods_ref, w_in_refs, w_a2_ref, b_a_ref, za_scr, sa_scr,
                         sqkv_scr, zs_hbm_ref, zs_sem, lay)
        for _ in project(lambda: xc_ref[0, 0:tile, :], b_cur, ua_scr, za_scr):
            pass
        for _ in premix(za_scr):
            pass

    gate = mod_ref[pl.ds(b_cur, 1), 2 * d:3 * d]
    mix_weights = _mix_weights(lay, tile)
    premix_weights = _premix_weights(lay, tile)
    plan = _project_plan(lay)
    n_proj = 1 + len(plan)
    premix_needs_proj = 1 + sum(1 for _, _, _, d0 in plan if d0 < lay.off_cb)

    steps_per_block = SUBLANES // sample_rows
    sub = lax.rem(j, steps_per_block)
    block = lax.div(j, steps_per_block)
    a_row = _step_row_reader(sa_scr, sample_rows, block, sub, steps_per_block)
    qkv_row = _step_row_reader(sqkv_scr, sample_rows, block, sub, steps_per_block)
    fillers = [functools.partial(_sample_state_step, i, a_row, qkv_row, sst_ref,
                                 sso_ref, so_ref, lay) for i in range(sample_rows)]
    half = sample_rows // 2

    _emit_overlapped(mix(za_scr, 0), mix_weights, premix(zb_scr), premix_weights,
                     project(lambda: xc_ref[0, tile:2 * tile, :], b_cur, ub_scr, zb_scr), n_proj,
                     premix_needs_proj, fillers[:half])
    _emit_overlapped(mix(zb_scr, tile), mix_weights, premix(za_scr), premix_weights,
                     project(lambda: xn_ref[0], b_next, ua_scr, za_scr), n_proj,
                     premix_needs_proj, fillers[half:])

    @pl.when(jj == steps_per_seq - 1)
    def _():
        sg_ref[0] = s_scr[...]
        sc_ref[0] = p_scr[halo - (ck - 1):halo, :]


def _prompt_vmem_bytes(lay, tile, sample_rows, n_s):
    weights = 2 * (lay.d * lay.n_pad + lay.rank_pad * lay.qk + lay.vt * lay.d
                   + lay.cw * lay.d + lay.d * lay.d)
    state = lay.h * lay.dk * lay.dv * 4
    io = (2 * (2 * tile + tile + 2 * tile) * lay.d * 4 + 2 * state
          + 2 * 2 * sample_rows * state + n_s * (3 * lay.d + lay.qk + lay.off_gg) * 4)
    chunk_heads = (tile // GLA_CHUNK) * lay.h
    scratch = (2 * tile * lay.d * 2 + 2 * tile * lay.n_pad * 4 + tile * lay.qk * 4
               + (tile + SUBLANES) * lay.cw * 4 + lay.h * lay.dk * lay.dv * 4
               + tile * lay.vt * 2 + tile * lay.cw * 2 + tile * lay.d * 2
               + 3 * tile * lay.qk * 2 + tile * lay.vt * 2 + SUBLANES * lay.qk * 4
               + lay.h * tile * LANES * 2 + chunk_heads * lay.dk * lay.dv * 6)
    temporaries = tile * lay.n_pad * 2
    return min(weights + io + scratch + temporaries, VMEM_PHYSICAL_BYTES * 15 // 16)


def _prompt_layer(x, mod_p, x_sample, mod_s, sample_state, params, prev_states, layer, lay,
                  alpha):
    (w_hd, w_lr, w_tl, w_a2, b_a, gng, conv_w, w_pa, w_pb, w_o, ln_g, ln_b) = params
    bsz, seq, d = x.shape
    depth = w_hd.shape[0]
    tile = PROMPT_TILE
    assert seq % (2 * tile) == 0 and tile % GLA_CHUNK == 0
    ck = conv_w.shape[0]
    n_t = seq // tile
    n_tiles = bsz * n_t
    n_steps = n_tiles // 2
    steps_per_seq = n_t // 2
    n_chunk_heads = (tile // GLA_CHUNK) * lay.h
    assert tile // GLA_CHUNK <= SUBLANES
    n_s = sample_state.shape[1]
    sample_rows = n_s // n_steps
    assert sample_rows * n_steps == n_s and sample_rows % 2 == 0 and SUBLANES % sample_rows == 0
    steps_per_block = SUBLANES // sample_rows

    def weight(arr):
        return _layer_spec(arr.shape[1:], layer, True)

    def next_tile(j):
        g = jnp.minimum(2 * j + 2, n_tiles - 1)
        return (g // n_t, g % n_t, 0)

    sample_state_spec = pl.BlockSpec((None, sample_rows, lay.h, lay.dk, lay.dv),
                                     lambda j: (layer, j, 0, 0, 0))
    in_specs = [
        pl.BlockSpec((1, 2 * tile, d), lambda j: (j // steps_per_seq, j % steps_per_seq, 0)),
        pl.BlockSpec((1, tile, d), next_tile),
        _layer_spec(mod_p.shape[1:], layer),
        pl.BlockSpec((n_s, d), lambda j: (0, 0), pipeline_mode=pl.Buffered(1)),
        pl.BlockSpec((None, n_s, 2 * d), lambda j: (layer, 0, 0), pipeline_mode=pl.Buffered(1)),
        sample_state_spec,
        weight(w_hd), weight(w_lr), weight(w_tl), weight(w_a2), _whole_spec(b_a),
        _whole_spec(gng), _whole_spec(conv_w), weight(w_pa), weight(w_pb), weight(w_o),
        _whole_spec(ln_g), _whole_spec(ln_b),
    ]
    args = [x, x, mod_p, x_sample, mod_s, sample_state, *params]
    aliases = {}
    if prev_states is not None:
        for i, prev in enumerate(prev_states):
            in_specs.append(pl.BlockSpec(memory_space=pl.ANY))
            aliases[len(args)] = 1 + i
            args.append(prev)

    kern = functools.partial(_prompt_kernel, lay=lay, layer=layer, tile=tile, alpha=alpha,
                             n_t=n_t, n_tiles=n_tiles, sample_rows=sample_rows)
    return pl.pallas_call(
        kern,
        grid=(n_steps,),
        in_specs=in_specs,
        out_specs=[
            pl.BlockSpec((1, 2 * tile, d), lambda j: (j // steps_per_seq, j % steps_per_seq, 0)),
            pl.BlockSpec((None, 1, lay.h, lay.dk, lay.dv),
                         lambda j: (layer, j // steps_per_seq, 0, 0, 0)),
            pl.BlockSpec((None, 1, ck - 1, lay.cw), lambda j: (layer, j // steps_per_seq, 0, 0)),
            sample_state_spec,
            pl.BlockSpec((None, sample_rows, lay.vt), lambda j: (j, 0, 0)),
            pl.BlockSpec(memory_space=pl.ANY),
        ],
        out_shape=[
            jax.ShapeDtypeStruct((bsz, seq, d), F32),
            jax.ShapeDtypeStruct((depth, bsz, lay.h, lay.dk, lay.dv), F32),
            jax.ShapeDtypeStruct((depth, bsz, ck - 1, lay.cw), F32),
            jax.ShapeDtypeStruct(sample_state.shape, F32),
            jax.ShapeDtypeStruct((n_steps, sample_rows, lay.vt), F32),
            jax.ShapeDtypeStruct((n_s, lay.n_pad), F32),
        ],
        scratch_shapes=[
            pltpu.VMEM((tile, d), BF16),
            pltpu.VMEM((tile, d), BF16),
            pltpu.VMEM((tile, lay.n_pad), F32),
            pltpu.VMEM((tile, lay.n_pad), F32),
            pltpu.VMEM((tile, lay.qk), F32),
            pltpu.VMEM((tile + SUBLANES, lay.cw), F32),
            pltpu.VMEM((lay.h, lay.dk, lay.dv), F32),
            pltpu.VMEM((tile, lay.vt), BF16),
            pltpu.VMEM((tile, lay.cw), BF16),
            pltpu.VMEM((tile, d), BF16),
            pltpu.VMEM((tile, lay.qk), BF16),
            pltpu.VMEM((tile, lay.qk), BF16),
            pltpu.VMEM((tile, lay.qk), BF16),
            pltpu.VMEM((tile, lay.vt), BF16),
            pltpu.VMEM((SUBLANES, lay.qk), F32),
            pltpu.VMEM((lay.h, tile, GLA_CHUNK), BF16),
            pltpu.VMEM((n_chunk_heads, lay.dk, lay.dv), F32),
            pltpu.VMEM((n_chunk_heads, lay.dk, lay.dv), BF16),
            pltpu.VMEM((n_s, lay.qk), F32),
            pltpu.VMEM((n_s, lay.off_gg), F32),
            pltpu.SemaphoreType.DMA(()),
        ],
        input_output_aliases=aliases,
        compiler_params=pltpu.CompilerParams(
            dimension_semantics=("arbitrary",),
            vmem_limit_bytes=_prompt_vmem_bytes(lay, tile, sample_rows, n_s)),
        name="prompt_layer",
    )(*args)


def _sample_out_kernel(o_ref, z_ref, x_ref, mod_ref, buf_ref, gng_ref, cw_ref,
                       w_pa_ref, w_pb_ref, w_o_ref, lng_ref, lnb_ref, *rest, lay, layer, alpha):
    y_ref, buf_out_ref, og_scr = rest[-3:]
    d, h_n, dv, cw = lay.d, lay.h, lay.dv, lay.cw
    gng_ref, lng_ref, lnb_ref = (_layer_rows(r, layer) for r in (gng_ref, lng_ref, lnb_ref))
    cw_ref = _layer_taps(cw_ref, layer)
    ck = cw_ref.shape[0]
    gate = mod_ref[:, 2 * d:3 * d]
    gng = gng_ref[...]
    for h in range(h_n):
        o = o_ref[:, h * dv:(h + 1) * dv]
        ms = jnp.mean(o * o, axis=-1, keepdims=True)
        on = o * lax.rsqrt(ms + EPS) * gng
        gg = z_ref[:, lay.off_gg + h * dv:lay.off_gg + (h + 1) * dv]
        og_scr[:, h * dv:(h + 1) * dv] = (on * _silu(gg)).astype(BF16)
    pa = jnp.dot(og_scr[...], _unpack_rows(w_pa_ref[...]), preferred_element_type=F32)

    p = z_ref[:, lay.off_cc:lay.off_cc + cw] * z_ref[:, lay.off_ch:lay.off_ch + cw]
    yc = cw_ref[ck - 1:ck, :] * p
    for i in range(ck - 1):
        yc = yc + cw_ref[i:i + 1, :] * buf_ref[:, i * cw:(i + 1) * cw]
    yc = z_ref[:, lay.off_cb:lay.off_cb + cw] * yc * _silu(z_ref[:, lay.off_gc:lay.off_gc + cw])
    pb = jnp.dot(yc.astype(BF16), _unpack_rows(w_pb_ref[...]), preferred_element_type=F32)
    for i in range(ck - 2):
        buf_out_ref[:, i * cw:(i + 1) * cw] = buf_ref[:, (i + 1) * cw:(i + 2) * cw]
    buf_out_ref[:, (ck - 2) * cw:(ck - 1) * cw] = p

    merged = (jax.nn.sigmoid(z_ref[:, lay.off_mg:lay.off_mg + d]) * pa
              + jax.nn.sigmoid(z_ref[:, lay.off_mc:lay.off_mc + d]) * pb)
    out = jnp.dot(merged.astype(BF16), _unpack_rows(w_o_ref[...]),
                  preferred_element_type=F32)
    hres = alpha * x_ref[...] + gate * out
    mu = jnp.mean(hres, axis=-1, keepdims=True)
    dev = hres - mu
    var = jnp.mean(dev * dev, axis=-1, keepdims=True)
    y_ref[...] = dev * lax.rsqrt(var + EPS) * lng_ref[...] + lnb_ref[...]


def _sample_out(o, z, x, mod_s, conv_state, prev_conv_out, gng, conv_w, w_pa, w_pb, w_o,
                ln_g, ln_b, layer, lay, alpha):
    n, d = x.shape
    kern = functools.partial(_sample_out_kernel, lay=lay, layer=layer, alpha=alpha)
    conv_spec = _layer_spec(conv_state.shape[1:], layer)
    in_specs = [_whole_spec(o), _whole_spec(z), _whole_spec(x),
                _layer_spec(mod_s.shape[1:], layer), conv_spec,
                _whole_spec(gng), _whole_spec(conv_w), _layer_spec(w_pa.shape[1:], layer),
                _layer_spec(w_pb.shape[1:], layer), _layer_spec(w_o.shape[1:], layer),
                _whole_spec(ln_g), _whole_spec(ln_b)]
    args = [o, z, x, mod_s, conv_state, gng, conv_w, w_pa, w_pb, w_o, ln_g, ln_b]
    aliases = {}
    if prev_conv_out is not None:
        in_specs.append(pl.BlockSpec(memory_space=pl.ANY))
        aliases = {len(args): 1}
        args.append(prev_conv_out)
    return pl.pallas_call(
        kern,
        grid=(1,),
        in_specs=in_specs,
        out_specs=[pl.BlockSpec((n, d), lambda i: (0, 0)), conv_spec],
        out_shape=[jax.ShapeDtypeStruct((n, d), F32),
                   jax.ShapeDtypeStruct(conv_state.shape, F32)],
        scratch_shapes=[pltpu.VMEM((n, lay.vt), BF16)],
        input_output_aliases=aliases,
        compiler_params=pltpu.CompilerParams(dimension_semantics=("arbitrary",),
                                             vmem_limit_bytes=STREAMING_VMEM_BYTES),
        name="sample_out",
    )(*args)


def kernel(x_prompt, x_sample, c_prompt, c_sample, state_gla, state_conv, w_ada, b_ada, w_in,
           w_a2, b_a, gla_norm_g, conv_w, w_pa, w_pb, w_o, ln_g, ln_b):
    depth, n_s, heads, dk, dv = state_gla.shape
    d = x_prompt.shape[2]
    rank = w_a2.shape[1]
    ck, cw = conv_w.shape[1:]
    lay = _Layout(d, heads, dk, dv, rank, cw)
    alpha = (2 * depth) ** 0.25

    mod_p, mod_s = _ada_mod(c_prompt, c_sample, w_ada, b_ada)

    w_hd, w_lr, w_tl = _split_w_in(w_in, lay)
    w_pa_b, w_pb_b, w_o_b, w_a2_p = _cast_weights((w_pa, w_pb, w_o), w_a2, lay.rank_pad)
    taps = jnp.swapaxes(conv_w, 0, 1)
    params = (w_hd, w_lr, w_tl, w_a2_p, b_a, gla_norm_g, taps, w_pa_b, w_pb_b, w_o_b, ln_g, ln_b)

    hp, hs = x_prompt, x_sample.reshape(n_s, d)
    states = None
    conv_in = state_conv.reshape(depth, n_s, (ck - 1) * cw)
    conv_s = None
    for l in range(depth):
        hp, *states, o, z = _prompt_layer(hp, mod_p, hs, mod_s, state_gla, params, states, l, lay,
                                          alpha)
        hs, conv_s = _sample_out(o.reshape(n_s, lay.vt), z, hs, mod_s, conv_in, conv_s,
                                 gla_norm_g, taps, w_pa_b, w_pb_b, w_o_b, ln_g, ln_b, l, lay,
                                 alpha)
    return (hp, hs.reshape(x_sample.shape), states[0], states[1], states[2],
            conv_s.reshape(state_conv.shape))
```
